```python
import jax, jax.numpy as jnp
from jax import lax
import numpy as np

D_MODEL = 1024
BATCH = 16
SEQ = 4096
DEPTH = 1

MEM_LEN = 256
CONV_CH = 512
CONV_GROUPS = 8
CONV_WIDTH = 3
MOBA_HEADS = 8
MOBA_HEAD_DIM = 64
MOBA_WIDTH = MOBA_HEADS * MOBA_HEAD_DIM
MOBA_BLOCK = 256
MOBA_TOPK = 3
MOBA_QCHUNK = 16
MEM_HEADS = 4
MEM_HEAD_DIM = 128
MEM_WIDTH = MEM_HEADS * MEM_HEAD_DIM
N_BRANCH = 3
IN_SPLITS = [CONV_CH, CONV_CH, CONV_CH, MOBA_WIDTH, MOBA_WIDTH, MOBA_WIDTH, MEM_WIDTH,
             D_MODEL, D_MODEL, D_MODEL]
IN_COLS = sum(IN_SPLITS)
D_FF = 2816
FFN_CONV_WIDTH = 3
EPS = 1e-6

kernel_name = "hybrid_gated_conv_moba_memxattn_block"


def rmsnorm(x, g):
    xf = x.astype(jnp.float32)
    y = xf * lax.rsqrt(jnp.mean(xf * xf, axis=-1, keepdims=True) + EPS)
    return y.astype(x.dtype) * g


def causal_dwconv3(u, w, b):
    s = u.shape[1]
    up = jnp.pad(u, ((0, 0), (2, 0), (0, 0)))
    return up[:, :s] * w[0] + up[:, 1:s + 1] * w[1] + up[:, 2:] * w[2] + b


def moba_attention(q, k, v):
    bsz, s, h, dh = q.shape
    L = MOBA_BLOCK
    nb = -(-s // L)
    sp = nb * L
    pad = ((0, 0), (0, sp - s), (0, 0), (0, 0))
    q, k, v = jnp.pad(q, pad), jnp.pad(k, pad), jnp.pad(v, pad)
    qh = q.transpose(0, 2, 1, 3)
    kb = k.reshape(bsz, nb, L, h, dh).transpose(0, 3, 1, 2, 4)
    vb = v.reshape(bsz, nb, L, h, dh).transpose(0, 3, 1, 2, 4)
    kbar = jnp.mean(kb, axis=3)
    n_sel = min(MOBA_TOPK, nb - 1)
    scale = dh ** -0.5
    gather = jax.vmap(jax.vmap(lambda blocks, idx: blocks[idx]))

    def one_chunk(ci):
        start = ci * MOBA_QCHUNK
        own = start // L
        qc = lax.dynamic_slice_in_dim(qh, start, MOBA_QCHUNK, axis=2)
        qpos = start + jnp.arange(MOBA_QCHUNK)
        kpos = own * L + jnp.arange(L)
        k_own = lax.dynamic_index_in_dim(kb, own, axis=2, keepdims=False)
        v_own = lax.dynamic_index_in_dim(vb, own, axis=2, keepdims=False)
        s_own = jnp.einsum('bhqd,bhkd->bhqk', qc, k_own).astype(jnp.float32) * scale
        s_own = jnp.where(kpos[None, :] <= qpos[:, None], s_own, -jnp.inf)
        if n_sel == 0:
            p = jax.nn.softmax(s_own, axis=-1).astype(v.dtype)
            return jnp.einsum('bhqk,bhkd->bhqd', p, v_own)
        gate = jnp.einsum('bhqd,bhnd->bhqn', qc, kbar)
        gate = jnp.where(jnp.arange(nb) < own, gate, -jnp.inf)
        _, idx = lax.top_k(gate, n_sel)
        valid = idx < own
        k_sel = gather(kb, idx)
        v_sel = gather(vb, idx)
        s_sel = jnp.einsum('bhqd,bhqnkd->bhqnk', qc, k_sel).astype(jnp.float32) * scale
        s_sel = jnp.where(valid[..., None], s_sel, -jnp.inf)
        s_sel = s_sel.reshape(bsz, h, MOBA_QCHUNK, n_sel * L)
        p = jax.nn.softmax(jnp.concatenate([s_sel, s_own], axis=-1), axis=-1).astype(v.dtype)
        p_sel = p[..., :n_sel * L].reshape(bsz, h, MOBA_QCHUNK, n_sel, L)
        p_own = p[..., n_sel * L:]
        return (jnp.einsum('bhqnk,bhqnkd->bhqd', p_sel, v_sel)
                + jnp.einsum('bhqk,bhkd->bhqd', p_own, v_own))

    out = lax.map(one_chunk, jnp.arange(sp // MOBA_QCHUNK))
    out = out.transpose(1, 0, 3, 2, 4).reshape(bsz, sp, h * dh)
    return out[:, :s]


def memory_cross_attention(q_m, mem_n, w_mem_kv, memq_gain, memk_gain):
    bsz, s, _ = q_m.shape
    m = mem_n.shape[1]
    q = rmsnorm(q_m.reshape(bsz, s, MEM_HEADS, MEM_HEAD_DIM), memq_gain)
    k_m, v_m = jnp.split(mem_n @ w_mem_kv, 2, axis=-1)
    k = rmsnorm(k_m.reshape(bsz, m, MEM_HEADS, MEM_HEAD_DIM), memk_gain)
    v = v_m.reshape(bsz, m, MEM_HEADS, MEM_HEAD_DIM)
    sc = jnp.einsum('bshd,bmhd->bhsm', q, k).astype(jnp.float32) * (MEM_HEAD_DIM ** -0.5)
    p = jax.nn.softmax(sc, axis=-1).astype(v.dtype)
    o = jnp.einsum('bhsm,bmhd->bshd', p, v)
    return o.reshape(bsz, s, MEM_WIDTH)


def setup_inputs(seed: int = 0) -> dict:
    key = jax.random.key(seed)
    ks = jax.random.split(key, 24)
    nrm = lambda k, shape, scale: jax.random.normal(k, shape, jnp.float32) * scale
    D = D_MODEL
    return {
        "x": nrm(ks[0], (BATCH, SEQ, D), 1.0),
        "mem": nrm(ks[1], (BATCH, MEM_LEN, D), 1.0),
        "g_mix": 1.0 + nrm(ks[2], (D,), 0.02),
        "w_in": nrm(ks[3], (D, IN_COLS), D ** -0.5),
        "b_gate": nrm(ks[4], (N_BRANCH * D,), 0.02),
        "conv_w": nrm(ks[5], (CONV_WIDTH, CONV_CH), CONV_WIDTH ** -0.5),
        "conv_b": nrm(ks[6], (CONV_CH,), 0.02),
        "moba_q_gain": 1.0 + nrm(ks[7], (MOBA_HEAD_DIM,), 0.02),
        "moba_k_gain": 1.0 + nrm(ks[8], (MOBA_HEAD_DIM,), 0.02),
        "g_mem": 1.0 + nrm(ks[9], (D,), 0.02),
        "w_mem_kv": nrm(ks[10], (D, 2 * MEM_WIDTH), D ** -0.5),
        "memq_gain": 1.0 + nrm(ks[11], (MEM_HEAD_DIM,), 0.02),
        "memk_gain": 1.0 + nrm(ks[12], (MEM_HEAD_DIM,), 0.02),
        "w_br_conv": nrm(ks[13], (CONV_CH, D), CONV_CH ** -0.5),
        "w_br_moba": nrm(ks[14], (MOBA_WIDTH, D), MOBA_WIDTH ** -0.5),
        "w_br_mem": nrm(ks[15], (MEM_WIDTH, D), MEM_WIDTH ** -0.5),
        "w_o": nrm(ks[16], (D, D), D ** -0.5),
        "g_ffn": 1.0 + nrm(ks[17], (D,), 0.02),
        "w_up": nrm(ks[18], (D, 2 * D_FF), D ** -0.5),
        "ffn_conv_w": nrm(ks[19], (FFN_CONV_WIDTH, D_FF), FFN_CONV_WIDTH ** -0.5),
        "ffn_conv_b": nrm(ks[20], (D_FF,), 0.02),
        "w_down": nrm(ks[21], (D_FF, D), D_FF ** -0.5),
    }


def reference(x, mem, g_mix, w_in, b_gate, conv_w, conv_b, moba_q_gain, moba_k_gain,
              g_mem, w_mem_kv, memq_gain, memk_gain, w_br_conv, w_br_moba, w_br_mem,
              w_o, g_ffn, w_up, ffn_conv_w, ffn_conv_b, w_down):
    bsz, s, D = x.shape
    mem_n = rmsnorm(mem, g_mem)
    for _ in range(DEPTH):
        h = rmsnorm(x, g_mix)
        proj = h @ w_in
        cuts = list(np.cumsum(IN_SPLITS)[:-1])
        (c_b, c_c, c_x, q, k, v, q_m, g1, g2, g3) = jnp.split(proj, cuts, axis=-1)
        bg1, bg2, bg3 = jnp.split(b_gate, N_BRANCH)

        y_conv = c_b * causal_dwconv3(c_c * c_x, conv_w, conv_b)

        qh = rmsnorm(q.reshape(bsz, s, MOBA_HEADS, MOBA_HEAD_DIM), moba_q_gain)
        kh = rmsnorm(k.reshape(bsz, s, MOBA_HEADS, MOBA_HEAD_DIM), moba_k_gain)
        vh = v.reshape(bsz, s, MOBA_HEADS, MOBA_HEAD_DIM)
        y_moba = moba_attention(qh, kh, vh)

        y_mem = memory_cross_attention(q_m, mem_n, w_mem_kv, memq_gain, memk_gain)

        merged = (jax.nn.sigmoid(g1 + bg1) * (y_conv @ w_br_conv)
                  + jax.nn.sigmoid(g2 + bg2) * (y_moba @ w_br_moba)
                  + jax.nn.sigmoid(g3 + bg3) * (y_mem @ w_br_mem))
        x = x + merged @ w_o

        h2 = rmsnorm(x, g_ffn)
        a, b = jnp.split(h2 @ w_up, 2, axis=-1)
        a = causal_dwconv3(a, ffn_conv_w, ffn_conv_b)
        x = x + (jax.nn.silu(a) * b) @ w_down
    return x
```

```python
import functools

import jax
import jax.numpy as jnp
from jax import lax
from jax.experimental import pallas as pl
from jax.experimental.pallas import tpu as pltpu

D_MODEL = 1024
MEM_LEN = 256
CONV_CH = 512
MOBA_HEADS = 8
MOBA_HEAD_DIM = 64
MOBA_WIDTH = MOBA_HEADS * MOBA_HEAD_DIM
MOBA_BLOCK = 256
MOBA_TOPK = 3
MEM_HEADS = 4
MEM_HEAD_DIM = 128
MEM_WIDTH = MEM_HEADS * MEM_HEAD_DIM
D_FF = 2816
EPS = 1e-6

C_CONV = 0
C_QKV = 3 * CONV_CH
C_QMEM = C_QKV + 3 * MOBA_WIDTH
C_GATE = C_QMEM + MEM_WIDTH
IN_COLS = C_GATE + 3 * D_MODEL

LANES = 128
SUBLANES = 8
PAIR = 2 * MOBA_HEAD_DIM
N_PAIRS = MOBA_HEADS // 2
KBAR_ROWS = LANES
MASKED = -1e30
VMEM_LIMIT_MIXER = 56 * 1024 * 1024
VMEM_LIMIT_FFN = 48 * 1024 * 1024
FFN_TILE = 256
FFN_CHUNKS = ((0, 1024), (1024, 1024), (2048, 768))

_NT = (((1,), (1,)), ((), ()))


def _dot(a, b):
    return jnp.dot(a, b, preferred_element_type=jnp.float32)


def _dot_nt(a, b):
    return lax.dot_general(a, b, _NT, preferred_element_type=jnp.float32)


def _rmsnorm(x, g):
    return x * lax.rsqrt(jnp.mean(x * x, axis=-1, keepdims=True) + EPS) * g


def _bf16(x):
    return x.astype(jnp.bfloat16)


def _memkv_kernel(mem_ref, g_ref, w_ref, kg_ref, k_ref, v_ref):
    mn = _rmsnorm(mem_ref[0], g_ref[...])
    kv = _dot(_bf16(mn), w_ref[...])
    for h in range(MEM_HEADS):
        lo = h * MEM_HEAD_DIM
        kh = _rmsnorm(kv[:, lo:lo + MEM_HEAD_DIM], kg_ref[...])
        k_ref[0, :, lo:lo + MEM_HEAD_DIM] = _bf16(kh)
    v_ref[0] = _bf16(kv[:, MEM_WIDTH:])


def _memkv(mem, g_mem, w_mem_kv, memk_gain):
    bsz, m, d = mem.shape
    const = lambda b: (0, 0)
    return pl.pallas_call(
        _memkv_kernel,
        grid=(bsz,),
        in_specs=[
            pl.BlockSpec((1, m, d), lambda b: (b, 0, 0)),
            pl.BlockSpec((1, d), const),
            pl.BlockSpec((d, 2 * MEM_WIDTH), const),
            pl.BlockSpec((1, MEM_HEAD_DIM), const),
        ],
        out_specs=[
            pl.BlockSpec((1, m, MEM_WIDTH), lambda b: (b, 0, 0)),
            pl.BlockSpec((1, m, MEM_WIDTH), lambda b: (b, 0, 0)),
        ],
        out_shape=[
            jax.ShapeDtypeStruct((bsz, m, MEM_WIDTH), jnp.bfloat16),
            jax.ShapeDtypeStruct((bsz, m, MEM_WIDTH), jnp.bfloat16),
        ],
        name="memkv",
    )(mem, g_mem.reshape(1, d), _bf16(w_mem_kv), memk_gain.reshape(1, MEM_HEAD_DIM))


def _head_norm(t, gsum_ref, gain):
    t2 = t * t
    hi = _bf16(t2)
    lo = _bf16(t2 - hi.astype(jnp.float32))
    ss = _dot(hi, gsum_ref[...]) + _dot(lo, gsum_ref[...])
    return t * lax.rsqrt(ss * (1.0 / MOBA_HEAD_DIM) + EPS) * gain


def _mixer_kernel(x_ref, gmix_ref, win_ref, bgate_ref, convw_ref, convb_ref, qg_ref, kg_ref, gsum_ref,
                  memk_ref, memv_ref, mqg_ref, wbc_ref, wbm_ref, wbx_ref, wo_ref,
                  out_ref,
                  k_s, v_s, kbar_s, uext_s, qaug_s, acc_s, m_s):
    i = pl.program_id(1)
    blk = MOBA_BLOCK
    x = x_ref[0]
    hb = _bf16(_rmsnorm(x, gmix_ref[...]))

    def gate(n):
        lo = C_GATE + n * D_MODEL
        g = _dot(hb, win_ref[:, lo:lo + D_MODEL]) + bgate_ref[:, n * D_MODEL:(n + 1) * D_MODEL]
        return jax.nn.sigmoid(g)

    pc = _dot(hb, win_ref[:, C_CONV:C_CONV + 3 * CONV_CH])
    c_b = pc[:, :CONV_CH]
    u = pc[:, CONV_CH:2 * CONV_CH] * pc[:, 2 * CONV_CH:]

    @pl.when(i == 0)
    def _():
        uext_s[0:SUBLANES, :] = jnp.zeros((SUBLANES, CONV_CH), jnp.float32)

    uext_s[SUBLANES:SUBLANES + blk, :] = u
    u1 = uext_s[SUBLANES - 1:SUBLANES - 1 + blk, :]
    u2 = uext_s[SUBLANES - 2:SUBLANES - 2 + blk, :]
    uext_s[0:SUBLANES, :] = u[blk - SUBLANES:, :]
    conv = u2 * convw_ref[0:1, :] + u1 * convw_ref[1:2, :] + u * convw_ref[2:3, :] + convb_ref[...]
    merged = gate(0) * _dot(_bf16(c_b * conv), wbc_ref[...])

    qm = _dot(hb, win_ref[:, C_QMEM:C_QMEM + MEM_WIDTH])
    outs = []
    for h in range(MEM_HEADS):
        lo = h * MEM_HEAD_DIM
        qh = _rmsnorm(qm[:, lo:lo + MEM_HEAD_DIM], mqg_ref[...])
        s = _dot_nt(_bf16(qh), memk_ref[0, :, lo:lo + MEM_HEAD_DIM])
        p = jnp.exp(s - jnp.max(s, axis=-1, keepdims=True))
        l = jnp.sum(p, axis=-1, keepdims=True)
        outs.append(_dot(_bf16(p), memv_ref[0, :, lo:lo + MEM_HEAD_DIM]) / l)
    y_mem = jnp.concatenate(outs, axis=-1)
    merged = merged + gate(2) * _dot(_bf16(y_mem), wbx_ref[...])

    qkv = _dot(hb, win_ref[:, C_QKV:C_QKV + 3 * MOBA_WIDTH])
    qb = _bf16(_head_norm(qkv[:, :MOBA_WIDTH], gsum_ref, qg_ref[...]))
    kn = _head_norm(qkv[:, MOBA_WIDTH:2 * MOBA_WIDTH], gsum_ref, kg_ref[...])
    kb = _bf16(kn)
    vb = _bf16(qkv[:, 2 * MOBA_WIDTH:])
    row0 = pl.multiple_of(i * blk, blk)
    k_s[pl.ds(row0, blk), :] = kb
    v_s[pl.ds(row0, blk), :] = vb

    @pl.when(i == 0)
    def _():
        kbar_s[...] = jnp.zeros(kbar_s.shape, jnp.float32)

    kbar_row = jnp.mean(kn, axis=0, keepdims=True)
    kb_rows = lax.broadcasted_iota(jnp.int32, kbar_s.shape, 0)
    kbar_prev = kbar_s[...]
    kbar_s[...] = jnp.where(kb_rows == i, kbar_row, kbar_prev)

    lane = lax.broadcasted_iota(jnp.int32, (blk, LANES), 1)
    lanef = lane.astype(jnp.float32)
    rows2 = lax.broadcasted_iota(jnp.int32, (blk, blk), 0)
    cols2 = lax.broadcasted_iota(jnp.int32, (blk, blk), 1)
    causal = cols2 <= rows2
    ones_b = jnp.ones((blk, LANES), jnp.bfloat16)
    neg_inf = jnp.float32(-jnp.inf)

    for h in range(MOBA_HEADS):
        pr = h // 2
        cs = slice(pr * PAIR, (pr + 1) * PAIR)
        own = (lane < MOBA_HEAD_DIM) if h % 2 == 0 else (lane >= MOBA_HEAD_DIM)
        qh = jnp.where(own, qb[:, cs], jnp.zeros((), jnp.bfloat16))
        g = _dot_nt(qh, _bf16(kbar_prev[:, cs]))
        g = jnp.where(lane < i, g, neg_inf)
        selbias = jnp.full((blk, LANES), MASKED, jnp.float32)
        for _ in range(MOBA_TOPK):
            mx = jnp.max(g, axis=-1, keepdims=True)
            idx = jnp.min(jnp.where(g == mx, lanef, float(LANES)), axis=-1, keepdims=True)
            pick = lanef == idx
            selbias = jnp.where(pick & (mx > neg_inf), 0.0, selbias)
            g = jnp.where(pick, neg_inf, g)
        qaug_s[h, :, 0:LANES] = qh
        qaug_s[h, :, LANES:2 * LANES] = _bf16(selbias)
        s = _dot_nt(qh, kb[:, cs])
        s = jnp.where(causal, s, MASKED)
        m0 = jnp.max(s, axis=-1, keepdims=True)
        p = jnp.exp(s - m0)
        acc_s[h] = _dot(_bf16(p), jnp.concatenate([vb[:, cs], ones_b], axis=1))
        m_s[h] = jnp.broadcast_to(m0, (blk, LANES))

    def kv_block(j, carry):
        r0 = pl.multiple_of(j * blk, blk)
        ej = jnp.where(lane == j, 1.0, 0.0).astype(jnp.bfloat16)
        for pr in range(N_PAIRS):
            cs = slice(pr * PAIR, (pr + 1) * PAIR)
            kaug = jnp.concatenate([k_s[pl.ds(r0, blk), cs], ej], axis=1)
            vaug = jnp.concatenate([v_s[pl.ds(r0, blk), cs], ones_b], axis=1)
            for h in (2 * pr, 2 * pr + 1):
                s = _dot_nt(qaug_s[h], kaug)
                m_old = m_s[h]
                m_new = jnp.maximum(m_old, jnp.max(s, axis=-1, keepdims=True))
                alpha = jnp.exp(m_old - m_new)
                p = jnp.exp(s - jnp.concatenate([m_new, m_new], axis=1))
                pv = _dot(_bf16(p), vaug)
                acc_s[h] = acc_s[h] * jnp.concatenate([alpha, alpha], axis=1) + pv
                m_s[h] = m_new
        return carry

    lax.fori_loop(0, i, kv_block, 0)

    pairs = []
    for pr in range(N_PAIRS):
        a_e = acc_s[2 * pr]
        a_o = acc_s[2 * pr + 1]
        o_e = a_e[:, :LANES] / a_e[:, LANES:]
        o_o = a_o[:, :LANES] / a_o[:, LANES:]
        pairs.append(jnp.where(lane < MOBA_HEAD_DIM, o_e, o_o))
    y_moba = jnp.concatenate(pairs, axis=-1)
    merged = merged + gate(1) * _dot(_bf16(y_moba), wbm_ref[...])

    out_ref[0] = x + _dot(_bf16(merged), wo_ref[...])


def _resident(shape):
    return pl.BlockSpec(shape, lambda b, i: (0,) * len(shape), pipeline_mode=pl.Buffered(1))


def _mixer(x, memk, memv, g_mix, w_in, b_gate, conv_w, conv_b, moba_q_gain, moba_k_gain, memq_gain,
           w_br_conv, w_br_moba, w_br_mem, w_o):
    bsz, s, d = x.shape
    blk = MOBA_BLOCK
    assert s % blk == 0 and s // blk <= KBAR_ROWS and d == D_MODEL
    nb = s // blk
    qg = jnp.tile(moba_q_gain, MOBA_HEADS).reshape(1, MOBA_WIDTH) * (MOBA_HEAD_DIM ** -0.5)
    kg = jnp.tile(moba_k_gain, MOBA_HEADS).reshape(1, MOBA_WIDTH)
    mqg = memq_gain.reshape(1, MEM_HEAD_DIM) * (MEM_HEAD_DIM ** -0.5)
    head_of = jnp.arange(MOBA_WIDTH) // MOBA_HEAD_DIM
    gsum = (head_of[:, None] == head_of[None, :]).astype(jnp.bfloat16)
    return pl.pallas_call(
        _mixer_kernel,
        grid=(bsz, nb),
        in_specs=[
            pl.BlockSpec((1, blk, d), lambda b, i: (b, i, 0)),
            _resident((1, d)),
            _resident((d, IN_COLS)),
            _resident((1, 3 * d)),
            _resident((3, CONV_CH)),
            _resident((1, CONV_CH)),
            _resident((1, MOBA_WIDTH)),
            _resident((1, MOBA_WIDTH)),
            _resident((MOBA_WIDTH, MOBA_WIDTH)),
            pl.BlockSpec((1, MEM_LEN, MEM_WIDTH), lambda b, i: (b, 0, 0)),
            pl.BlockSpec((1, MEM_LEN, MEM_WIDTH), lambda b, i: (b, 0, 0)),
            _resident((1, MEM_HEAD_DIM)),
            _resident((CONV_CH, d)),
            _resident((MOBA_WIDTH, d)),
            _resident((MEM_WIDTH, d)),
            _resident((d, d)),
        ],
        out_specs=pl.BlockSpec((1, blk, d), lambda b, i: (b, i, 0)),
        out_shape=jax.ShapeDtypeStruct((bsz, s, d), jnp.float32),
        scratch_shapes=[
            pltpu.VMEM((s, MOBA_WIDTH), jnp.bfloat16),
            pltpu.VMEM((s, MOBA_WIDTH), jnp.bfloat16),
            pltpu.VMEM((KBAR_ROWS, MOBA_WIDTH), jnp.float32),
            pltpu.VMEM((SUBLANES + blk, CONV_CH), jnp.float32),
            pltpu.VMEM((MOBA_HEADS, blk, 2 * LANES), jnp.bfloat16),
            pltpu.VMEM((MOBA_HEADS, blk, 2 * LANES), jnp.float32),
            pltpu.VMEM((MOBA_HEADS, blk, LANES), jnp.float32),
        ],
        compiler_params=pltpu.CompilerParams(
            dimension_semantics=("arbitrary", "arbitrary"),
            vmem_limit_bytes=VMEM_LIMIT_MIXER),
        name="mixer",
    )(x, g_mix.reshape(1, d), _bf16(w_in), b_gate.reshape(1, 3 * d), conv_w, conv_b.reshape(1, CONV_CH),
      qg, kg, gsum, memk, memv, mqg, _bf16(w_br_conv), _bf16(w_br_moba), _bf16(w_br_mem), _bf16(w_o))


def _ffn_kernel(x_ref, g_ref, wup_ref, cw_ref, cb_ref, wdown_ref, out_ref, aext_s, act_s):
    t = pl.program_id(1)
    tm = FFN_TILE
    x = x_ref[0]
    hb = _bf16(_rmsnorm(x, g_ref[...]))

    @pl.when(t == 0)
    def _():
        aext_s[0:SUBLANES, :] = jnp.zeros((SUBLANES, D_FF), jnp.float32)

    for c0, cw in FFN_CHUNKS:
        cs = slice(c0, c0 + cw)
        a = _dot(hb, wup_ref[:, c0:c0 + cw])
        b = _dot(hb, wup_ref[:, D_FF + c0:D_FF + c0 + cw])
        aext_s[SUBLANES:SUBLANES + tm, cs] = a
        a1 = aext_s[SUBLANES - 1:SUBLANES - 1 + tm, cs]
        a2 = aext_s[SUBLANES - 2:SUBLANES - 2 + tm, cs]
        aext_s[0:SUBLANES, cs] = a[tm - SUBLANES:, :]
        ac = a2 * cw_ref[0:1, cs] + a1 * cw_ref[1:2, cs] + a * cw_ref[2:3, cs] + cb_ref[:, cs]
        act_s[:, cs] = _bf16(ac * jax.nn.sigmoid(ac) * b)
    out_ref[0] = x + _dot(act_s[...], wdown_ref[...])


def _ffn(x, g_ffn, w_up, ffn_conv_w, ffn_conv_b, w_down):
    bsz, s, d = x.shape
    tm = FFN_TILE
    assert s % tm == 0
    return pl.pallas_call(
        _ffn_kernel,
        grid=(bsz, s // tm),
        in_specs=[
            pl.BlockSpec((1, tm, d), lambda b, t: (b, t, 0)),
            _resident((1, d)),
            _resident((d, 2 * D_FF)),
            _resident((3, D_FF)),
            _resident((1, D_FF)),
            _resident((D_FF, d)),
        ],
        out_specs=pl.BlockSpec((1, tm, d), lambda b, t: (b, t, 0)),
        out_shape=jax.ShapeDtypeStruct((bsz, s, d), jnp.float32),
        scratch_shapes=[
            pltpu.VMEM((SUBLANES + tm, D_FF), jnp.float32),
            pltpu.VMEM((tm, D_FF), jnp.bfloat16),
        ],
        compiler_params=pltpu.CompilerParams(
            dimension_semantics=("arbitrary", "arbitrary"),
            vmem_limit_bytes=VMEM_LIMIT_FFN),
        name="ffn",
    )(x, g_ffn.reshape(1, d), _bf16(w_up), ffn_conv_w, ffn_conv_b.reshape(1, D_FF), _bf16(w_down))


def kernel(x, mem, g_mix, w_in, b_gate, conv_w, conv_b, moba_q_gain, moba_k_gain, g_mem, w_mem_kv, memq_gain,
           memk_gain, w_br_conv, w_br_moba, w_br_mem, w_o, g_ffn, w_up, ffn_conv_w, ffn_conv_b, w_down):
    memk, memv = _memkv(mem, g_mem, w_mem_kv, memk_gain)
    x = _mixer(x, memk, memv, g_mix, w_in, b_gate, conv_w, conv_b, moba_q_gain, moba_k_gain, memq_gain,
               w_br_conv, w_br_moba, w_br_mem, w_o)
    return _ffn(x, g_ffn, w_up, ffn_conv_w, ffn_conv_b, w_down)
```

```python
import math

import jax
import jax.numpy as jnp
from jax import lax
from jax.experimental import pallas as pl
from jax.experimental.pallas import tpu as pltpu

D_MODEL = 1024
MEM_LEN = 256
CONV_CH = 512
MOBA_HEADS = 8
MOBA_HEAD_DIM = 64
MOBA_WIDTH = MOBA_HEADS * MOBA_HEAD_DIM
MOBA_BLOCK = 256
MOBA_TOPK = 3
MEM_HEADS = 4
MEM_HEAD_DIM = 128
MEM_WIDTH = MEM_HEADS * MEM_HEAD_DIM
D_FF = 2816
EPS = 1e-6

W_CONV = 0
W_Q = 3 * CONV_CH
W_K = W_Q + MOBA_WIDTH
W_V = W_K + MOBA_WIDTH
W_QMEM = W_V + MOBA_WIDTH
W_GATE = W_QMEM + MEM_WIDTH
C_CONV = 0
C_K = 3 * CONV_CH
C_QMEM = C_K + MOBA_WIDTH
C_GATE = C_QMEM + MEM_WIDTH
MAIN_COLS = C_GATE + 3 * D_MODEL

LANES = 128
SUBLANES = 8
BF16_ROWS = 16
PAIR = 2 * MOBA_HEAD_DIM
N_PAIRS = MOBA_HEADS // 2
VT_ROWS = MOBA_HEAD_DIM + BF16_ROWS
MASKED = -1e30
VMEM_LIMIT_MIXER = 56 * 1024 * 1024
VMEM_LIMIT_FFN = 48 * 1024 * 1024
FFN_TILE = 256
FFN_CHUNKS = ((0, 1024), (1024, 1024), (2048, 768))

_NT = (((1,), (1,)), ((), ()))


def _dot(a, b):
    return jnp.dot(a, b, preferred_element_type=jnp.float32)


def _dot_nt(a, b):
    return lax.dot_general(a, b, _NT, preferred_element_type=jnp.float32)


def _rmsnorm(x, g):
    return x * lax.rsqrt(jnp.mean(x * x, axis=-1, keepdims=True) + EPS) * g


def _bf16(x):
    return x.astype(jnp.bfloat16)


def _memkv_kernel(mem_ref, g_ref, w_ref, kg_ref, k_ref, v_ref):
    mn = _rmsnorm(mem_ref[0], g_ref[...])
    kv = _dot(_bf16(mn), w_ref[...])
    for h in range(MEM_HEADS):
        lo = h * MEM_HEAD_DIM
        kh = _rmsnorm(kv[:, lo:lo + MEM_HEAD_DIM], kg_ref[...])
        k_ref[0, :, lo:lo + MEM_HEAD_DIM] = _bf16(kh)
    v_ref[0] = _bf16(kv[:, MEM_WIDTH:])


def _memkv(mem, g_mem, w_mem_kv, memk_gain):
    bsz, m, d = mem.shape
    const = lambda b: (0, 0)
    return pl.pallas_call(
        _memkv_kernel,
        grid=(bsz,),
        in_specs=[
            pl.BlockSpec((1, m, d), lambda b: (b, 0, 0)),
            pl.BlockSpec((1, d), const),
            pl.BlockSpec((d, 2 * MEM_WIDTH), const),
            pl.BlockSpec((1, MEM_HEAD_DIM), const),
        ],
        out_specs=[
            pl.BlockSpec((1, m, MEM_WIDTH), lambda b: (b, 0, 0)),
            pl.BlockSpec((1, m, MEM_WIDTH), lambda b: (b, 0, 0)),
        ],
        out_shape=[
            jax.ShapeDtypeStruct((bsz, m, MEM_WIDTH), jnp.bfloat16),
            jax.ShapeDtypeStruct((bsz, m, MEM_WIDTH), jnp.bfloat16),
        ],
        name="memkv",
    )(mem, g_mem.reshape(1, d), _bf16(w_mem_kv), memk_gain.reshape(1, MEM_HEAD_DIM))


def _head_norm(t, gsum_ref, gain):
    t2 = t * t
    hi = _bf16(t2)
    lo = _bf16(t2 - hi.astype(jnp.float32))
    ss = _dot(hi, gsum_ref[...]) + _dot(lo, gsum_ref[...])
    return t * lax.rsqrt(ss * (1.0 / MOBA_HEAD_DIM) + EPS) * gain


def _mixer_kernel(x_ref, gmix_ref, wmain_ref, wqvt_ref, bgate_ref, convw_ref, convb_ref, qgt_ref, kg_ref, gsum_ref,
                  memk_ref, memv_ref, mqg_ref, wbc_ref, wbm_ref, wbx_ref, wo_ref,
                  out_ref,
                  k_s, vt_s, kbar_s, uext_s, rhs_s, sa_s, sb_s):
    i = pl.program_id(1)
    blk = MOBA_BLOCK
    nbr = kbar_s.shape[0]
    x = x_ref[0]
    hb = _bf16(_rmsnorm(x, gmix_ref[...]))

    def gate(n):
        lo = C_GATE + n * D_MODEL
        g = _dot(hb, wmain_ref[:, lo:lo + D_MODEL]) + bgate_ref[:, n * D_MODEL:(n + 1) * D_MODEL]
        return jax.nn.sigmoid(g)

    pc = _dot(hb, wmain_ref[:, C_CONV:C_CONV + 3 * CONV_CH])
    c_b = pc[:, :CONV_CH]
    u = pc[:, CONV_CH:2 * CONV_CH] * pc[:, 2 * CONV_CH:]

    @pl.when(i == 0)
    def _():
        uext_s[0:SUBLANES, :] = jnp.zeros((SUBLANES, CONV_CH), jnp.float32)

    uext_s[SUBLANES:SUBLANES + blk, :] = u
    u1 = uext_s[SUBLANES - 1:SUBLANES - 1 + blk, :]
    u2 = uext_s[SUBLANES - 2:SUBLANES - 2 + blk, :]
    uext_s[0:SUBLANES, :] = u[blk - SUBLANES:, :]
    conv = u2 * convw_ref[0:1, :] + u1 * convw_ref[1:2, :] + u * convw_ref[2:3, :] + convb_ref[...]
    merged = gate(0) * _dot(_bf16(c_b * conv), wbc_ref[...])

    qm = _dot(hb, wmain_ref[:, C_QMEM:C_QMEM + MEM_WIDTH])
    outs = []
    for h in range(MEM_HEADS):
        lo = h * MEM_HEAD_DIM
        qh = _rmsnorm(qm[:, lo:lo + MEM_HEAD_DIM], mqg_ref[...])
        s = _dot_nt(_bf16(qh), memk_ref[0, :, lo:lo + MEM_HEAD_DIM])
        p = jnp.exp(s - jnp.max(s, axis=-1, keepdims=True))
        l = jnp.sum(p, axis=-1, keepdims=True)
        outs.append(_dot(_bf16(p), memv_ref[0, :, lo:lo + MEM_HEAD_DIM]) / l)
    y_mem = jnp.concatenate(outs, axis=-1)
    merged = merged + gate(2) * _dot(_bf16(y_mem), wbx_ref[...])

    kn = _head_norm(_dot(hb, wmain_ref[:, C_K:C_K + MOBA_WIDTH]), gsum_ref, kg_ref[...])
    kb = _bf16(kn)
    k_s[i] = kb
    qvt = _dot_nt(wqvt_ref[...], hb)
    q3 = qvt[:MOBA_WIDTH].reshape(MOBA_HEADS, MOBA_HEAD_DIM, blk)
    ssq = jnp.sum(q3 * q3, axis=1, keepdims=True)
    qnt = (q3 * lax.rsqrt(ssq * (1.0 / MOBA_HEAD_DIM) + EPS)).reshape(MOBA_WIDTH, blk) * qgt_ref[...]
    qbt = _bf16(qnt)
    vbt = _bf16(qvt[MOBA_WIDTH:])

    @pl.when(i == 0)
    def _():
        kbar_s[...] = jnp.zeros(kbar_s.shape, jnp.float32)

    kbar_row = jnp.mean(kn, axis=0, keepdims=True)
    kb_rows = lax.broadcasted_iota(jnp.int32, kbar_s.shape, 0)
    kbar_prev = kbar_s[...]
    kbar_s[...] = jnp.where(kb_rows == i, kbar_row, kbar_prev)

    lane = lax.broadcasted_iota(jnp.int32, (blk, LANES), 1)
    brow = lax.broadcasted_iota(jnp.int32, (nbr, blk), 0)
    browf = brow.astype(jnp.float32)
    keys2 = lax.broadcasted_iota(jnp.int32, (blk, blk), 0)
    qrys2 = lax.broadcasted_iota(jnp.int32, (blk, blk), 1)
    causal = keys2 <= qrys2
    zeros_h = jnp.zeros((MOBA_HEAD_DIM, blk), jnp.bfloat16)
    ones_r = jnp.ones((BF16_ROWS, blk), jnp.bfloat16)
    sel_pad = jnp.full((LANES - nbr, blk), MASKED, jnp.bfloat16)
    neg_inf = jnp.float32(-jnp.inf)
    mx0, acc0 = [], []

    for h in range(MOBA_HEADS):
        pr = h // 2
        cs = slice(pr * PAIR, (pr + 1) * PAIR)
        qh = qbt[h * MOBA_HEAD_DIM:(h + 1) * MOBA_HEAD_DIM]
        qpair = jnp.concatenate([qh, zeros_h] if h % 2 == 0 else [zeros_h, qh], axis=0)
        g = _dot(_bf16(kbar_prev[:, cs]), qpair)
        g = jnp.where(brow < i, g, neg_inf)
        sel = jnp.full((nbr, blk), MASKED, jnp.float32)
        for _ in range(MOBA_TOPK):
            mx = jnp.max(g, axis=0, keepdims=True)
            idx = jnp.min(jnp.where(g == mx, browf, float(nbr)), axis=0, keepdims=True)
            pick = browf == idx
            sel = jnp.where(pick & (mx > neg_inf), 0.0, sel)
            g = jnp.where(pick, neg_inf, g)
        rhs_s[h] = jnp.concatenate([qpair, _bf16(sel), sel_pad], axis=0)
        vt = jnp.concatenate([vbt[h * MOBA_HEAD_DIM:(h + 1) * MOBA_HEAD_DIM], ones_r], axis=0)
        vt_s[i, h] = vt
        s = _dot(kb[:, cs], qpair)
        s = jnp.where(causal, s, MASKED)
        m0 = jnp.max(s, axis=0, keepdims=True)
        acc0.append(_dot(vt, _bf16(jnp.exp2(s - m0))))
        mx0.append(m0)

    def scores(dst_s, h, jn):
        jc = jnp.minimum(jn, i)
        en = jnp.where(lane == jnp.where(jn < i, jn, LANES - 1), 1.0, 0.0).astype(jnp.bfloat16)
        kaug = jnp.concatenate([k_s[jc, :, (h // 2) * PAIR:(h // 2 + 1) * PAIR], en], axis=1)
        dst_s[h] = _dot(kaug, rhs_s[h])

    def attend(src_s, dst_s, jcur, jnext, carry):
        ms, accs = carry
        jv = jnp.minimum(jcur, i)
        ms_new, accs_new = [], []
        for h in range(MOBA_HEADS):
            scores(dst_s, h, jnext)
            s = src_s[h]
            m_new = jnp.maximum(ms[h], jnp.max(s, axis=0, keepdims=True))
            pv = _dot(vt_s[jv, h], _bf16(jnp.exp2(s - m_new)))
            accs_new.append(accs[h] * jnp.exp2(ms[h] - m_new) + pv)
            ms_new.append(m_new)
        return tuple(ms_new), tuple(accs_new)

    @pl.when(i > 0)
    def _():
        for h in range(MOBA_HEADS):
            scores(sa_s, h, 0)

    def kv_pair(t, carry):
        carry = attend(sa_s, sb_s, 2 * t, 2 * t + 1, carry)
        return attend(sb_s, sa_s, 2 * t + 1, 2 * t + 2, carry)

    _, accs = lax.fori_loop(0, (i + 1) // 2, kv_pair, (tuple(mx0), tuple(acc0)))

    outs = []
    for h in range(MOBA_HEADS):
        a = accs[h]
        outs.append(a[:MOBA_HEAD_DIM] / a[MOBA_HEAD_DIM:MOBA_HEAD_DIM + 1])
    y_moba = jnp.concatenate(outs, axis=0).T
    merged = merged + gate(1) * _dot(_bf16(y_moba), wbm_ref[...])

    out_ref[0] = x + _dot(_bf16(merged), wo_ref[...])


def _resident(shape):
    return pl.BlockSpec(shape, lambda b, i: (0,) * len(shape), pipeline_mode=pl.Buffered(1))


def _mixer(x, memk, memv, g_mix, w_in, b_gate, conv_w, conv_b, moba_q_gain, moba_k_gain, memq_gain,
           w_br_conv, w_br_moba, w_br_mem, w_o):
    bsz, s, d = x.shape
    blk = MOBA_BLOCK
    assert s % blk == 0 and d == D_MODEL
    nb = s // blk
    nbr = pl.cdiv(nb, BF16_ROWS) * BF16_ROWS
    assert nbr <= LANES
    w_main = _bf16(jnp.concatenate([w_in[:, W_CONV:W_Q], w_in[:, W_K:W_V], w_in[:, W_QMEM:]], axis=1))
    w_qvt = _bf16(jnp.concatenate([w_in[:, W_Q:W_K], w_in[:, W_V:W_QMEM]], axis=1).T)
    qgt = jnp.broadcast_to(
        (jnp.tile(moba_q_gain, MOBA_HEADS) * (MOBA_HEAD_DIM ** -0.5 * math.log2(math.e)))[:, None], (MOBA_WIDTH, blk))
    kg = jnp.tile(moba_k_gain, MOBA_HEADS).reshape(1, MOBA_WIDTH)
    mqg = memq_gain.reshape(1, MEM_HEAD_DIM) * (MEM_HEAD_DIM ** -0.5)
    head_of = jnp.arange(MOBA_WIDTH) // MOBA_HEAD_DIM
    gsum = (head_of[:, None] == head_of[None, :]).astype(jnp.bfloat16)
    return pl.pallas_call(
        _mixer_kernel,
        grid=(bsz, nb),
        in_specs=[
            pl.BlockSpec((1, blk, d), lambda b, i: (b, i, 0)),
            _resident((1, d)),
            _resident((d, MAIN_COLS)),
            _resident((2 * MOBA_WIDTH, d)),
            _resident((1, 3 * d)),
            _resident((3, CONV_CH)),
            _resident((1, CONV_CH)),
            _resident((MOBA_WIDTH, blk)),
            _resident((1, MOBA_WIDTH)),
            _resident((MOBA_WIDTH, MOBA_WIDTH)),
            pl.BlockSpec((1, MEM_LEN, MEM_WIDTH), lambda b, i: (b, 0, 0)),
            pl.BlockSpec((1, MEM_LEN, MEM_WIDTH), lambda b, i: (b, 0, 0)),
            _resident((1, MEM_HEAD_DIM)),
            _resident((CONV_CH, d)),
            _resident((MOBA_WIDTH, d)),
            _resident((MEM_WIDTH, d)),
            _resident((d, d)),
        ],
        out_specs=pl.BlockSpec((1, blk, d), lambda b, i: (b, i, 0)),
        out_shape=jax.ShapeDtypeStruct((bsz, s, d), jnp.float32),
        scratch_shapes=[
            pltpu.VMEM((nb, blk, MOBA_WIDTH), jnp.bfloat16),
            pltpu.VMEM((nb, MOBA_HEADS, VT_ROWS, blk), jnp.bfloat16),
            pltpu.VMEM((nbr, MOBA_WIDTH), jnp.float32),
            pltpu.VMEM((SUBLANES + blk, CONV_CH), jnp.float32),
            pltpu.VMEM((MOBA_HEADS, 2 * LANES, blk), jnp.bfloat16),
            pltpu.VMEM((MOBA_HEADS, blk, blk), jnp.float32),
            pltpu.VMEM((MOBA_HEADS, blk, blk), jnp.float32),
        ],
        compiler_params=pltpu.CompilerParams(
            dimension_semantics=("arbitrary", "arbitrary"),
            vmem_limit_bytes=VMEM_LIMIT_MIXER),
        name="mixer",
    )(x, g_mix.reshape(1, d), w_main, w_qvt, b_gate.reshape(1, 3 * d), conv_w, conv_b.reshape(1, CONV_CH),
      qgt, kg, gsum, memk, memv, mqg, _bf16(w_br_conv), _bf16(w_br_moba), _bf16(w_br_mem), _bf16(w_o))


def _ffn_kernel(x_ref, g_ref, wup_ref, cw_ref, cb_ref, wdown_ref, out_ref, aext_s, act_s):
    t = pl.program_id(1)
    tm = FFN_TILE
    x = x_ref[0]
    hb = _bf16(_rmsnorm(x, g_ref[...]))

    @pl.when(t == 0)
    def _():
        aext_s[0:SUBLANES, :] = jnp.zeros((SUBLANES, D_FF), jnp.float32)

    for c0, cw in FFN_CHUNKS:
        cs = slice(c0, c0 + cw)
        a = _dot(hb, wup_ref[:, c0:c0 + cw])
        b = _dot(hb, wup_ref[:, D_FF + c0:D_FF + c0 + cw])
        aext_s[SUBLANES:SUBLANES + tm, cs] = a
        a1 = aext_s[SUBLANES - 1:SUBLANES - 1 + tm, cs]
        a2 = aext_s[SUBLANES - 2:SUBLANES - 2 + tm, cs]
        aext_s[0:SUBLANES, cs] = a[tm - SUBLANES:, :]
        ac = a2 * cw_ref[0:1, cs] + a1 * cw_ref[1:2, cs] + a * cw_ref[2:3, cs] + cb_ref[:, cs]
        act_s[:, cs] = _bf16(ac * jax.nn.sigmoid(ac) * b)
    out_ref[0] = x + _dot(act_s[...], wdown_ref[...])


def _ffn(x, g_ffn, w_up, ffn_conv_w, ffn_conv_b, w_down):
    bsz, s, d = x.shape
    tm = FFN_TILE
    assert s % tm == 0
    return pl.pallas_call(
        _ffn_kernel,
        grid=(bsz, s // tm),
        in_specs=[
            pl.BlockSpec((1, tm, d), lambda b, t: (b, t, 0)),
            _resident((1, d)),
            _resident((d, 2 * D_FF)),
            _resident((3, D_FF)),
            _resident((1, D_FF)),
            _resident((D_FF, d)),
        ],
        out_specs=pl.BlockSpec((1, tm, d), lambda b, t: (b, t, 0)),
        out_shape=jax.ShapeDtypeStruct((bsz, s, d), jnp.float32),
        scratch_shapes=[
            pltpu.VMEM((SUBLANES + tm, D_FF), jnp.float32),
            pltpu.VMEM((tm, D_FF), jnp.bfloat16),
        ],
        compiler_params=pltpu.CompilerParams(
            dimension_semantics=("arbitrary", "arbitrary"),
            vmem_limit_bytes=VMEM_LIMIT_FFN),
        name="ffn",
    )(x, g_ffn.reshape(1, d), _bf16(w_up), ffn_conv_w, ffn_conv_b.reshape(1, D_FF), _bf16(w_down))


def kernel(x, mem, g_mix, w_in, b_gate, conv_w, conv_b, moba_q_gain, moba_k_gain, g_mem, w_mem_kv, memq_gain,
           memk_gain, w_br_conv, w_br_moba, w_br_mem, w_o, g_ffn, w_up, ffn_conv_w, ffn_conv_b, w_down):
    memk, memv = _memkv(mem, g_mem, w_mem_kv, memk_gain)
    x = _mixer(x, memk, memv, g_mix, w_in, b_gate, conv_w, conv_b, moba_q_gain, moba_k_gain, memq_gain,
               w_br_conv, w_br_moba, w_br_mem, w_o)
    return _ffn(x, g_ffn, w_up, ffn_conv_w, ffn_conv_b, w_down)
```

```python
import math

import jax
import jax.numpy as jnp
from jax import lax
from jax.experimental import pallas as pl
from jax.experimental.pallas import tpu as pltpu

D_MODEL = 1024
MEM_LEN = 256
CONV_CH = 512
MOBA_HEADS = 8
MOBA_HEAD_DIM = 64
MOBA_WIDTH = MOBA_HEADS * MOBA_HEAD_DIM
MOBA_BLOCK = 256
MOBA_TOPK = 3
MEM_HEADS = 4
MEM_HEAD_DIM = 128
MEM_WIDTH = MEM_HEADS * MEM_HEAD_DIM
D_FF = 2816
EPS = 1e-6

W_CONV = 0
W_Q = 3 * CONV_CH
W_K = W_Q + MOBA_WIDTH
W_V = W_K + MOBA_WIDTH
W_QMEM = W_V + MOBA_WIDTH
W_GATE = W_QMEM + MEM_WIDTH
C_CONV = 0
C_K = 3 * CONV_CH
C_QMEM = C_K + MOBA_WIDTH
C_GATE = C_QMEM + MEM_WIDTH
MAIN_COLS = C_GATE + 3 * D_MODEL

LANES = 128
SUBLANES = 8
BF16_ROWS = 16
PAIR = 2 * MOBA_HEAD_DIM
VT_ROWS = MOBA_HEAD_DIM + BF16_ROWS
MASKED = -1e30
VMEM_LIMIT_MIXER = 56 * 1024 * 1024
VMEM_LIMIT_FFN = 48 * 1024 * 1024
FFN_TILE = 256
FFN_CHUNKS = ((0, 1024), (1024, 1024), (2048, 768))

_NT = (((1,), (1,)), ((), ()))


def _dot(a, b):
    return jnp.dot(a, b, preferred_element_type=jnp.float32)


def _dot_nt(a, b):
    return lax.dot_general(a, b, _NT, preferred_element_type=jnp.float32)


def _rmsnorm(x, g):
    return x * lax.rsqrt(jnp.mean(x * x, axis=-1, keepdims=True) + EPS) * g


def _bf16(x):
    return x.astype(jnp.bfloat16)


def _memkv_kernel(mem_ref, g_ref, w_ref, kg_ref, k_ref, v_ref):
    mn = _rmsnorm(mem_ref[0], g_ref[...])
    kv = _dot(_bf16(mn), w_ref[...])
    for h in range(MEM_HEADS):
        lo = h * MEM_HEAD_DIM
        kh = _rmsnorm(kv[:, lo:lo + MEM_HEAD_DIM], kg_ref[...])
        k_ref[0, :, lo:lo + MEM_HEAD_DIM] = _bf16(kh)
    v_ref[0] = _bf16(kv[:, MEM_WIDTH:])


def _memkv(mem, g_mem, w_mem_kv, memk_gain):
    bsz, m, d = mem.shape
    const = lambda b: (0, 0)
    return pl.pallas_call(
        _memkv_kernel,
        grid=(bsz,),
        in_specs=[
            pl.BlockSpec((1, m, d), lambda b: (b, 0, 0)),
            pl.BlockSpec((1, d), const),
            pl.BlockSpec((d, 2 * MEM_WIDTH), const),
            pl.BlockSpec((1, MEM_HEAD_DIM), const),
        ],
        out_specs=[
            pl.BlockSpec((1, m, MEM_WIDTH), lambda b: (b, 0, 0)),
            pl.BlockSpec((1, m, MEM_WIDTH), lambda b: (b, 0, 0)),
        ],
        out_shape=[
            jax.ShapeDtypeStruct((bsz, m, MEM_WIDTH), jnp.bfloat16),
            jax.ShapeDtypeStruct((bsz, m, MEM_WIDTH), jnp.bfloat16),
        ],
        name="memkv",
    )(mem, g_mem.reshape(1, d), _bf16(w_mem_kv), memk_gain.reshape(1, MEM_HEAD_DIM))


def _mixer_kernel(x_ref, gmix_ref, wmain_ref, wqvt_ref, bgate_ref, convw_ref, convb_ref, qgt_ref, kg_ref, gsum_ref,
                  memk_ref, memv_ref, mqg_ref, wbc_ref, wbm_ref, wbx_ref, wo_ref,
                  out_ref,
                  k_s, vt_s, kbar_s, uext_s, rhs_s, sa_s, sb_s):
    i = pl.program_id(1)
    blk = MOBA_BLOCK
    nbr = kbar_s.shape[0]

    @pl.when(i == 0)
    def _():
        uext_s[0:SUBLANES, :] = jnp.zeros((SUBLANES, CONV_CH), jnp.float32)
        kbar_s[...] = jnp.zeros(kbar_s.shape, jnp.float32)

    x = x_ref[0]
    hb = _bf16(_rmsnorm(x, gmix_ref[...]))

    def proj(lo, width):
        return _dot(hb, wmain_ref[:, lo:lo + width])

    def gate(g, n):
        return jax.nn.sigmoid(g + bgate_ref[:, n * D_MODEL:(n + 1) * D_MODEL])

    pc = proj(C_CONV, 3 * CONV_CH)
    k_raw = proj(C_K, MOBA_WIDTH)
    qvt = _dot_nt(wqvt_ref[...], hb)
    qm = proj(C_QMEM, MEM_WIDTH)

    u = pc[:, CONV_CH:2 * CONV_CH] * pc[:, 2 * CONV_CH:]
    uext_s[SUBLANES:SUBLANES + blk, :] = u
    u1 = uext_s[SUBLANES - 1:SUBLANES - 1 + blk, :]
    u2 = uext_s[SUBLANES - 2:SUBLANES - 2 + blk, :]
    uext_s[0:SUBLANES, :] = u[blk - SUBLANES:, :]
    conv = u2 * convw_ref[0:1, :] + u1 * convw_ref[1:2, :] + u * convw_ref[2:3, :] + convb_ref[...]
    y_conv = _bf16(pc[:, :CONV_CH] * conv)

    k2 = k_raw * k_raw
    k2_hi = _bf16(k2)
    k2_lo = _bf16(k2 - k2_hi.astype(jnp.float32))

    q3 = qvt[:MOBA_WIDTH].reshape(MOBA_HEADS, MOBA_HEAD_DIM, blk)
    ssq = jnp.sum(q3 * q3, axis=1, keepdims=True)
    qbt = _bf16((q3 * lax.rsqrt(ssq * (1.0 / MOBA_HEAD_DIM) + EPS)).reshape(MOBA_WIDTH, blk) * qgt_ref[...])
    vbt = _bf16(qvt[MOBA_WIDTH:])

    qmb = [_bf16(_rmsnorm(qm[:, h * MEM_HEAD_DIM:(h + 1) * MEM_HEAD_DIM], mqg_ref[...])) for h in range(MEM_HEADS)]

    g0 = proj(C_GATE, D_MODEL)
    kss = _dot(k2_hi, gsum_ref[...]) + _dot(k2_lo, gsum_ref[...])
    p_conv = _dot(y_conv, wbc_ref[...])
    s_mem = [_dot_nt(qmb[h], memk_ref[0, :, h * MEM_HEAD_DIM:(h + 1) * MEM_HEAD_DIM]) for h in range(MEM_HEADS)]
    g2 = proj(C_GATE + 2 * D_MODEL, D_MODEL)

    merged = gate(g0, 0) * p_conv
    kn = k_raw * lax.rsqrt(kss * (1.0 / MOBA_HEAD_DIM) + EPS) * kg_ref[...]
    kb = _bf16(kn)
    k_s[i] = kb
    kbar_row = jnp.mean(kn, axis=0, keepdims=True)
    kb_rows = lax.broadcasted_iota(jnp.int32, kbar_s.shape, 0)
    kbar_prev = kbar_s[...]
    kbar_s[...] = jnp.where(kb_rows == i, kbar_row, kbar_prev)
    kbar_b = _bf16(kbar_prev)

    zeros_h = jnp.zeros((MOBA_HEAD_DIM, blk), jnp.bfloat16)
    qpairs = []
    for h in range(MOBA_HEADS):
        qh = qbt[h * MOBA_HEAD_DIM:(h + 1) * MOBA_HEAD_DIM]
        qpairs.append(jnp.concatenate([qh, zeros_h] if h % 2 == 0 else [zeros_h, qh], axis=0))

    p_mem = [jnp.exp(s - jnp.max(s, axis=-1, keepdims=True)) for s in s_mem]

    pair_cols = lambda h: slice((h // 2) * PAIR, (h // 2 + 1) * PAIR)
    gates = [_dot(kbar_b[:, pair_cols(h)], qpairs[h]) for h in range(MOBA_HEADS)]
    s_own = [_dot(kb[:, pair_cols(h)], qpairs[h]) for h in range(MOBA_HEADS)]
    o_mem = [_dot(_bf16(p_mem[h]), memv_ref[0, :, h * MEM_HEAD_DIM:(h + 1) * MEM_HEAD_DIM])
             / jnp.sum(p_mem[h], axis=-1, keepdims=True) for h in range(MEM_HEADS)]
    g1 = proj(C_GATE + D_MODEL, D_MODEL)
    merged = merged + gate(g2, 2) * _dot(_bf16(jnp.concatenate(o_mem, axis=-1)), wbx_ref[...])
    gate1 = gate(g1, 1)

    lane = lax.broadcasted_iota(jnp.int32, (blk, LANES), 1)
    brow = lax.broadcasted_iota(jnp.int32, (nbr, blk), 0)
    browf = brow.astype(jnp.float32)
    neg_inf = jnp.float32(-jnp.inf)
    sel_pad = jnp.full((LANES - nbr, blk), MASKED, jnp.bfloat16)
    ones_r = jnp.ones((BF16_ROWS, blk), jnp.bfloat16)
    vts = []
    for h in range(MOBA_HEADS):
        g = jnp.where(brow < i, gates[h], neg_inf)
        sel = jnp.full((nbr, blk), MASKED, jnp.float32)
        for _ in range(MOBA_TOPK):
            mx = jnp.max(g, axis=0, keepdims=True)
            idx = jnp.min(jnp.where(g == mx, browf, float(nbr)), axis=0, keepdims=True)
            pick = browf == idx
            sel = jnp.where(pick & (mx > neg_inf), 0.0, sel)
            g = jnp.where(pick, neg_inf, g)
        rhs_s[h] = jnp.concatenate([qpairs[h], _bf16(sel), sel_pad], axis=0)
        vt = jnp.concatenate([vbt[h * MOBA_HEAD_DIM:(h + 1) * MOBA_HEAD_DIM], ones_r], axis=0)
        vt_s[i, h] = vt
        vts.append(vt)

    def scores(dst_s, h, jn):
        jc = jnp.minimum(jn, i)
        en = jnp.where(lane == jnp.where(jn < i, jn, LANES - 1), 1.0, 0.0).astype(jnp.bfloat16)
        kaug = jnp.concatenate([k_s[jc, :, pair_cols(h)], en], axis=1)
        dst_s[h] = _dot(kaug, rhs_s[h])

    causal = (lax.broadcasted_iota(jnp.int32, (blk, blk), 0) <= lax.broadcasted_iota(jnp.int32, (blk, blk), 1))
    mx0, acc0 = [], []
    for h in range(MOBA_HEADS):
        scores(sa_s, h, 0)
        s = jnp.where(causal, s_own[h], MASKED)
        m0 = jnp.max(s, axis=0, keepdims=True)
        acc0.append(_dot(vts[h], _bf16(jnp.exp2(s - m0))))
        mx0.append(m0)

    def attend(src_s, dst_s, jcur, jnext, carry):
        ms, accs = carry
        jv = jnp.minimum(jcur, i)
        ms_new, accs_new = [], []
        for h in range(MOBA_HEADS):
            scores(dst_s, h, jnext)
            s = src_s[h]
            m_new = jnp.maximum(ms[h], jnp.max(s, axis=0, keepdims=True))
            pv = _dot(vt_s[jv, h], _bf16(jnp.exp2(s - m_new)))
            accs_new.append(accs[h] * jnp.exp2(ms[h] - m_new) + pv)
            ms_new.append(m_new)
        return tuple(ms_new), tuple(accs_new)

    def kv_pair(t, carry):
        carry = attend(sa_s, sb_s, 2 * t, 2 * t + 1, carry)
        return attend(sb_s, sa_s, 2 * t + 1, 2 * t + 2, carry)

    _, accs = lax.fori_loop(0, (i + 1) // 2, kv_pair, (tuple(mx0), tuple(acc0)))

    outs = []
    for h in range(MOBA_HEADS):
        a = accs[h]
        outs.append(a[:MOBA_HEAD_DIM] / a[MOBA_HEAD_DIM:MOBA_HEAD_DIM + 1])
    y_moba = jnp.concatenate(outs, axis=0).T
    merged = merged + gate1 * _dot(_bf16(y_moba), wbm_ref[...])

    out_ref[0] = x + _dot(_bf16(merged), wo_ref[...])


def _resident(shape):
    return pl.BlockSpec(shape, lambda b, i: (0,) * len(shape), pipeline_mode=pl.Buffered(1))


def _mixer(x, memk, memv, g_mix, w_in, b_gate, conv_w, conv_b, moba_q_gain, moba_k_gain, memq_gain,
           w_br_conv, w_br_moba, w_br_mem, w_o):
    bsz, s, d = x.shape
    blk = MOBA_BLOCK
    assert s % blk == 0 and d == D_MODEL
    nb = s // blk
    nbr = pl.cdiv(nb, BF16_ROWS) * BF16_ROWS
    assert nbr < LANES
    w_main = _bf16(jnp.concatenate([w_in[:, W_CONV:W_Q], w_in[:, W_K:W_V], w_in[:, W_QMEM:]], axis=1))
    w_qvt = _bf16(jnp.concatenate([w_in[:, W_Q:W_K], w_in[:, W_V:W_QMEM]], axis=1).T)
    qgt = jnp.broadcast_to(
        (jnp.tile(moba_q_gain, MOBA_HEADS) * (MOBA_HEAD_DIM ** -0.5 * math.log2(math.e)))[:, None], (MOBA_WIDTH, blk))
    kg = jnp.tile(moba_k_gain, MOBA_HEADS).reshape(1, MOBA_WIDTH)
    mqg = memq_gain.reshape(1, MEM_HEAD_DIM) * (MEM_HEAD_DIM ** -0.5)
    head_of = jnp.arange(MOBA_WIDTH) // MOBA_HEAD_DIM
    gsum = (head_of[:, None] == head_of[None, :]).astype(jnp.bfloat16)
    return pl.pallas_call(
        _mixer_kernel,
        grid=(bsz, nb),
        in_specs=[
            pl.BlockSpec((1, blk, d), lambda b, i: (b, i, 0)),
            _resident((1, d)),
            _resident((d, MAIN_COLS)),
            _resident((2 * MOBA_WIDTH, d)),
            _resident((1, 3 * d)),
            _resident((3, CONV_CH)),
            _resident((1, CONV_CH)),
            _resident((MOBA_WIDTH, blk)),
            _resident((1, MOBA_WIDTH)),
            _resident((MOBA_WIDTH, MOBA_WIDTH)),
            pl.BlockSpec((1, MEM_LEN, MEM_WIDTH), lambda b, i: (b, 0, 0)),
            pl.BlockSpec((1, MEM_LEN, MEM_WIDTH), lambda b, i: (b, 0, 0)),
            _resident((1, MEM_HEAD_DIM)),
            _resident((CONV_CH, d)),
            _resident((MOBA_WIDTH, d)),
            _resident((MEM_WIDTH, d)),
            _resident((d, d)),
        ],
        out_specs=pl.BlockSpec((1, blk, d), lambda b, i: (b, i, 0)),
        out_shape=jax.ShapeDtypeStruct((bsz, s, d), jnp.float32),
        scratch_shapes=[
            pltpu.VMEM((nb, blk, MOBA_WIDTH), jnp.bfloat16),
            pltpu.VMEM((nb, MOBA_HEADS, VT_ROWS, blk), jnp.bfloat16),
            pltpu.VMEM((nbr, MOBA_WIDTH), jnp.float32),
            pltpu.VMEM((SUBLANES + blk, CONV_CH), jnp.float32),
            pltpu.VMEM((MOBA_HEADS, 2 * LANES, blk), jnp.bfloat16),
            pltpu.VMEM((MOBA_HEADS, blk, blk), jnp.float32),
            pltpu.VMEM((MOBA_HEADS, blk, blk), jnp.float32),
        ],
        compiler_params=pltpu.CompilerParams(
            dimension_semantics=("arbitrary", "arbitrary"),
            vmem_limit_bytes=VMEM_LIMIT_MIXER),
        name="mixer",
    )(x, g_mix.reshape(1, d), w_main, w_qvt, b_gate.reshape(1, 3 * d), conv_w, conv_b.reshape(1, CONV_CH),
      qgt, kg, gsum, memk, memv, mqg, _bf16(w_br_conv), _bf16(w_br_moba), _bf16(w_br_mem), _bf16(w_o))


def _ffn_kernel(x_ref, g_ref, wup_ref, cw_ref, cb_ref, wdown_ref, out_ref, aext_s, act_s):
    t = pl.program_id(1)
    tm = FFN_TILE
    x = x_ref[0]
    hb = _bf16(_rmsnorm(x, g_ref[...]))

    @pl.when(t == 0)
    def _():
        aext_s[0:SUBLANES, :] = jnp.zeros((SUBLANES, D_FF), jnp.float32)

    for c0, cw in FFN_CHUNKS:
        cs = slice(c0, c0 + cw)
        a = _dot(hb, wup_ref[:, c0:c0 + cw])
        b = _dot(hb, wup_ref[:, D_FF + c0:D_FF + c0 + cw])
        aext_s[SUBLANES:SUBLANES + tm, cs] = a
        a1 = aext_s[SUBLANES - 1:SUBLANES - 1 + tm, cs]
        a2 = aext_s[SUBLANES - 2:SUBLANES - 2 + tm, cs]
        aext_s[0:SUBLANES, cs] = a[tm - SUBLANES:, :]
        ac = a2 * cw_ref[0:1, cs] + a1 * cw_ref[1:2, cs] + a * cw_ref[2:3, cs] + cb_ref[:, cs]
        act_s[:, cs] = _bf16(ac * jax.nn.sigmoid(ac) * b)
    out_ref[0] = x + _dot(act_s[...], wdown_ref[...])


def _ffn(x, g_ffn, w_up, ffn_conv_w, ffn_conv_b, w_down):
    bsz, s, d = x.shape
    tm = FFN_TILE
    assert s % tm == 0
    return pl.pallas_call(
        _ffn_kernel,
        grid=(bsz, s // tm),
        in_specs=[
            pl.BlockSpec((1, tm, d), lambda b, t: (b, t, 0)),
            _resident((1, d)),
            _resident((d, 2 * D_FF)),
            _resident((3, D_FF)),
            _resident((1, D_FF)),
            _resident((D_FF, d)),
        ],
        out_specs=pl.BlockSpec((1, tm, d), lambda b, t: (b, t, 0)),
        out_shape=jax.ShapeDtypeStruct((bsz, s, d), jnp.float32),
        scratch_shapes=[
            pltpu.VMEM((SUBLANES + tm, D_FF), jnp.float32),
            pltpu.VMEM((tm, D_FF), jnp.bfloat16),
        ],
        compiler_params=pltpu.CompilerParams(
            dimension_semantics=("arbitrary", "arbitrary"),
            vmem_limit_bytes=VMEM_LIMIT_FFN),
        name="ffn",
    )(x, g_ffn.reshape(1, d), _bf16(w_up), ffn_conv_w, ffn_conv_b.reshape(1, D_FF), _bf16(w_down))


def kernel(x, mem, g_mix, w_in, b_gate, conv_w, conv_b, moba_q_gain, moba_k_gain, g_mem, w_mem_kv, memq_gain,
           memk_gain, w_br_conv, w_br_moba, w_br_mem, w_o, g_ffn, w_up, ffn_conv_w, ffn_conv_b, w_down):
    memk, memv = _memkv(mem, g_mem, w_mem_kv, memk_gain)
    x = _mixer(x, memk, memv, g_mix, w_in, b_gate, conv_w, conv_b, moba_q_gain, moba_k_gain, memq_gain,
               w_br_conv, w_br_moba, w_br_mem, w_o)
    return _ffn(x, g_ffn, w_up, ffn_conv_w, ffn_conv_b, w_down)
```

```python
import math

import jax
import jax.numpy as jnp
from jax import lax
from jax.experimental import pallas as pl
from jax.experimental.pallas import tpu as pltpu

D_MODEL = 1024
MEM_LEN = 256
CONV_CH = 512
MOBA_HEADS = 8
MOBA_HEAD_DIM = 64
MOBA_WIDTH = MOBA_HEADS * MOBA_HEAD_DIM
MOBA_BLOCK = 256
MOBA_TOPK = 3
MEM_HEADS = 4
MEM_HEAD_DIM = 128
MEM_WIDTH = MEM_HEADS * MEM_HEAD_DIM
D_FF = 2816
EPS = 1e-6

W_CONV = 0
W_Q = 3 * CONV_CH
W_K = W_Q + MOBA_WIDTH
W_V = W_K + MOBA_WIDTH
W_QMEM = W_V + MOBA_WIDTH
W_GATE = W_QMEM + MEM_WIDTH
C_CONV = 0
C_K = 3 * CONV_CH
C_QMEM = C_K + MOBA_WIDTH
C_GATE = C_QMEM + MEM_WIDTH
MAIN_COLS = C_GATE + 3 * D_MODEL

LANES = 128
SUBLANES = 8
BF16_ROWS = 16
PAIR = 2 * MOBA_HEAD_DIM
VT_ROWS = MOBA_HEAD_DIM + BF16_ROWS
MASKED = -1e30
VMEM_LIMIT_MIXER = 56 * 1024 * 1024
VMEM_LIMIT_FFN = 48 * 1024 * 1024
FFN_TILE = 512
FFN_CHUNKS = ((0, 1024), (1024, 1024), (2048, 768))

_NT = (((1,), (1,)), ((), ()))


def _dot(a, b):
    return jnp.dot(a, b, preferred_element_type=jnp.float32)


def _dot_nt(a, b):
    return lax.dot_general(a, b, _NT, preferred_element_type=jnp.float32)


def _rmsnorm(x, g):
    return x * lax.rsqrt(jnp.mean(x * x, axis=-1, keepdims=True) + EPS) * g


def _bf16(x):
    return x.astype(jnp.bfloat16)


def _memkv_kernel(mem_ref, g_ref, w_ref, kg_ref, k_ref, v_ref):
    mn = _rmsnorm(mem_ref[0], g_ref[...])
    kv = _dot(_bf16(mn), w_ref[...])
    for h in range(MEM_HEADS):
        lo = h * MEM_HEAD_DIM
        kh = _rmsnorm(kv[:, lo:lo + MEM_HEAD_DIM], kg_ref[...])
        k_ref[0, :, lo:lo + MEM_HEAD_DIM] = _bf16(kh)
    v_ref[0] = _bf16(kv[:, MEM_WIDTH:])


def _memkv(mem, g_mem, w_mem_kv, memk_gain):
    bsz, m, d = mem.shape
    const = lambda b: (0, 0)
    return pl.pallas_call(
        _memkv_kernel,
        grid=(bsz,),
        in_specs=[
            pl.BlockSpec((1, m, d), lambda b: (b, 0, 0)),
            pl.BlockSpec((1, d), const),
            pl.BlockSpec((d, 2 * MEM_WIDTH), const),
            pl.BlockSpec((1, MEM_HEAD_DIM), const),
        ],
        out_specs=[
            pl.BlockSpec((1, m, MEM_WIDTH), lambda b: (b, 0, 0)),
            pl.BlockSpec((1, m, MEM_WIDTH), lambda b: (b, 0, 0)),
        ],
        out_shape=[
            jax.ShapeDtypeStruct((bsz, m, MEM_WIDTH), jnp.bfloat16),
            jax.ShapeDtypeStruct((bsz, m, MEM_WIDTH), jnp.bfloat16),
        ],
        name="memkv",
    )(mem, g_mem.reshape(1, d), _bf16(w_mem_kv), memk_gain.reshape(1, MEM_HEAD_DIM))


def _mixer_kernel(x_ref, gmix_ref, wmain_ref, wqvt_ref, bgate_ref, convw_ref, convb_ref, qgt_ref, kg_ref, gsum_ref,
                  memk_ref, memv_ref, mqg_ref, wbc_ref, wbm_ref, wbx_ref, wo_ref,
                  out_ref,
                  k_s, vt_s, kbar_s, uext_s, rhs_s, sa_s, sb_s, ma_s, mb_s, acc_s, m_s):
    i = pl.program_id(1)
    blk = MOBA_BLOCK
    nbr = kbar_s.shape[0]

    @pl.when(i == 0)
    def _():
        uext_s[0:SUBLANES, :] = jnp.zeros((SUBLANES, CONV_CH), jnp.float32)
        kbar_s[...] = jnp.zeros(kbar_s.shape, jnp.float32)

    x = x_ref[0]
    hb = _bf16(_rmsnorm(x, gmix_ref[...]))

    def proj(lo, width):
        return _dot(hb, wmain_ref[:, lo:lo + width])

    def gate(g, n):
        return jax.nn.sigmoid(g + bgate_ref[:, n * D_MODEL:(n + 1) * D_MODEL])

    pc = proj(C_CONV, 3 * CONV_CH)
    k_raw = proj(C_K, MOBA_WIDTH)
    qvt = _dot_nt(wqvt_ref[...], hb)
    qm = proj(C_QMEM, MEM_WIDTH)

    u = pc[:, CONV_CH:2 * CONV_CH] * pc[:, 2 * CONV_CH:]
    uext_s[SUBLANES:SUBLANES + blk, :] = u
    u1 = uext_s[SUBLANES - 1:SUBLANES - 1 + blk, :]
    u2 = uext_s[SUBLANES - 2:SUBLANES - 2 + blk, :]
    uext_s[0:SUBLANES, :] = u[blk - SUBLANES:, :]
    conv = u2 * convw_ref[0:1, :] + u1 * convw_ref[1:2, :] + u * convw_ref[2:3, :] + convb_ref[...]
    y_conv = _bf16(pc[:, :CONV_CH] * conv)

    k2 = k_raw * k_raw
    k2_hi = _bf16(k2)
    k2_lo = _bf16(k2 - k2_hi.astype(jnp.float32))

    q3 = qvt[:MOBA_WIDTH].reshape(MOBA_HEADS, MOBA_HEAD_DIM, blk)
    ssq = jnp.sum(q3 * q3, axis=1, keepdims=True)
    qbt = _bf16((q3 * lax.rsqrt(ssq * (1.0 / MOBA_HEAD_DIM) + EPS)).reshape(MOBA_WIDTH, blk) * qgt_ref[...])
    vbt = _bf16(qvt[MOBA_WIDTH:])

    qmb = [_bf16(_rmsnorm(qm[:, h * MEM_HEAD_DIM:(h + 1) * MEM_HEAD_DIM], mqg_ref[...])) for h in range(MEM_HEADS)]

    g0 = proj(C_GATE, D_MODEL)
    kss = _dot(k2_hi, gsum_ref[...]) + _dot(k2_lo, gsum_ref[...])
    p_conv = _dot(y_conv, wbc_ref[...])
    s_mem = [_dot_nt(qmb[h], memk_ref[0, :, h * MEM_HEAD_DIM:(h + 1) * MEM_HEAD_DIM]) for h in range(MEM_HEADS)]
    g2 = proj(C_GATE + 2 * D_MODEL, D_MODEL)

    merged = gate(g0, 0) * p_conv
    kn = k_raw * lax.rsqrt(kss * (1.0 / MOBA_HEAD_DIM) + EPS) * kg_ref[...]
    kb = _bf16(kn)
    k_s[i] = kb
    kbar_row = jnp.mean(kn, axis=0, keepdims=True)
    kb_rows = lax.broadcasted_iota(jnp.int32, kbar_s.shape, 0)
    kbar_prev = kbar_s[...]
    kbar_s[...] = jnp.where(kb_rows == i, kbar_row, kbar_prev)
    kbar_b = _bf16(kbar_prev)

    zeros_h = jnp.zeros((MOBA_HEAD_DIM, blk), jnp.bfloat16)
    qpairs = []
    for h in range(MOBA_HEADS):
        qh = qbt[h * MOBA_HEAD_DIM:(h + 1) * MOBA_HEAD_DIM]
        qpairs.append(jnp.concatenate([qh, zeros_h] if h % 2 == 0 else [zeros_h, qh], axis=0))

    p_mem = [jnp.exp(s - jnp.max(s, axis=-1, keepdims=True)) for s in s_mem]

    pair_cols = lambda h: slice((h // 2) * PAIR, (h // 2 + 1) * PAIR)
    gates = [_dot(kbar_b[:, pair_cols(h)], qpairs[h]) for h in range(MOBA_HEADS)]
    s_own = [_dot(kb[:, pair_cols(h)], qpairs[h]) for h in range(MOBA_HEADS)]
    o_mem = [_dot(_bf16(p_mem[h]), memv_ref[0, :, h * MEM_HEAD_DIM:(h + 1) * MEM_HEAD_DIM])
             / jnp.sum(p_mem[h], axis=-1, keepdims=True) for h in range(MEM_HEADS)]
    g1 = proj(C_GATE + D_MODEL, D_MODEL)
    merged = merged + gate(g2, 2) * _dot(_bf16(jnp.concatenate(o_mem, axis=-1)), wbx_ref[...])
    gate1 = gate(g1, 1)

    lane = lax.broadcasted_iota(jnp.int32, (blk, LANES), 1)
    brow = lax.broadcasted_iota(jnp.int32, (nbr, blk), 0)
    browf = brow.astype(jnp.float32)
    neg_inf = jnp.float32(-jnp.inf)
    sel_pad = jnp.full((LANES - nbr, blk), MASKED, jnp.bfloat16)
    ones_r = jnp.ones((BF16_ROWS, blk), jnp.bfloat16)
    vts = []
    for h in range(MOBA_HEADS):
        g = jnp.where(brow < i, gates[h], neg_inf)
        sel = jnp.full((nbr, blk), MASKED, jnp.float32)
        for _ in range(MOBA_TOPK):
            mx = jnp.max(g, axis=0, keepdims=True)
            idx = jnp.min(jnp.where(g == mx, browf, float(nbr)), axis=0, keepdims=True)
            pick = browf == idx
            sel = jnp.where(pick & (mx > neg_inf), 0.0, sel)
            g = jnp.where(pick, neg_inf, g)
        rhs_s[h] = jnp.concatenate([qpairs[h], _bf16(sel), sel_pad], axis=0)
        vt = jnp.concatenate([vbt[h * MOBA_HEAD_DIM:(h + 1) * MOBA_HEAD_DIM], ones_r], axis=0)
        vt_s[i, h] = vt
        vts.append(vt)

    def pair_scores(dst, h, tp):
        dst_s, dst_mx = dst
        parts = []
        for jn in (2 * tp, 2 * tp + 1):
            en = jnp.where(lane == jnp.where(jn < i, jn, LANES - 1), 1.0, 0.0).astype(jnp.bfloat16)
            parts.append(jnp.concatenate([k_s[jnp.minimum(jn, i), :, pair_cols(h)], en], axis=1))
        s = _dot(jnp.concatenate(parts, axis=0), rhs_s[h])
        dst_s[h] = s
        dst_mx[h] = jnp.broadcast_to(jnp.max(s, axis=0, keepdims=True), (SUBLANES, blk))

    causal = (lax.broadcasted_iota(jnp.int32, (blk, blk), 0) <= lax.broadcasted_iota(jnp.int32, (blk, blk), 1))
    set_a, set_b = (sa_s, ma_s), (sb_s, mb_s)
    for h in range(MOBA_HEADS):
        pair_scores(set_a, h, 0)
        s = jnp.where(causal, s_own[h], MASKED)
        m0 = jnp.max(s, axis=0, keepdims=True)
        acc_s[h] = _dot(vts[h], jnp.exp2(_bf16(s - m0)))
        m_s[h] = jnp.broadcast_to(m0, (SUBLANES, blk))

    def attend(src, dst, tp):
        src_s, src_mx = src
        ja = jnp.minimum(2 * tp, i)
        jb = jnp.minimum(2 * tp + 1, i)
        for h in range(MOBA_HEADS):
            if dst is not None:
                pair_scores(dst, h, tp + 1)
            m_old = m_s[h]
            m_new = jnp.maximum(m_old, src_mx[h])
            p = jnp.exp2(_bf16(src_s[h] - m_new[0:1]))
            pv = _dot(jnp.concatenate([vt_s[ja, h], vt_s[jb, h]], axis=1), p)
            acc_s[h] = acc_s[h] * jnp.exp2(m_old[0:1] - m_new[0:1]) + pv
            m_s[h] = m_new

    n_pairs = (i + 1) // 2

    def two_pairs(u, carry):
        attend(set_a, set_b, 2 * u)
        attend(set_b, set_a, 2 * u + 1)
        return carry

    lax.fori_loop(0, n_pairs // 2, two_pairs, 0)

    @pl.when(n_pairs % 2 == 1)
    def _():
        attend(set_a, None, n_pairs - 1)

    outs = []
    for h in range(MOBA_HEADS):
        a = acc_s[h]
        outs.append(a[:MOBA_HEAD_DIM] / a[MOBA_HEAD_DIM:MOBA_HEAD_DIM + 1])
    y_moba = jnp.concatenate(outs, axis=0).T
    merged = merged + gate1 * _dot(_bf16(y_moba), wbm_ref[...])

    out_ref[0] = x + _dot(_bf16(merged), wo_ref[...])


def _resident(shape):
    return pl.BlockSpec(shape, lambda b, i: (0,) * len(shape), pipeline_mode=pl.Buffered(1))


def _mixer(x, memk, memv, g_mix, w_in, b_gate, conv_w, conv_b, moba_q_gain, moba_k_gain, memq_gain,
           w_br_conv, w_br_moba, w_br_mem, w_o):
    bsz, s, d = x.shape
    blk = MOBA_BLOCK
    assert s % blk == 0 and d == D_MODEL
    nb = s // blk
    nbr = pl.cdiv(nb, BF16_ROWS) * BF16_ROWS
    assert nbr < LANES
    w_main = _bf16(jnp.concatenate([w_in[:, W_CONV:W_Q], w_in[:, W_K:W_V], w_in[:, W_QMEM:]], axis=1))
    w_qvt = _bf16(jnp.concatenate([w_in[:, W_Q:W_K], w_in[:, W_V:W_QMEM]], axis=1).T)
    qgt = jnp.broadcast_to(
        (jnp.tile(moba_q_gain, MOBA_HEADS) * (MOBA_HEAD_DIM ** -0.5 * math.log2(math.e)))[:, None], (MOBA_WIDTH, blk))
    kg = jnp.tile(moba_k_gain, MOBA_HEADS).reshape(1, MOBA_WIDTH)
    mqg = memq_gain.reshape(1, MEM_HEAD_DIM) * (MEM_HEAD_DIM ** -0.5)
    head_of = jnp.arange(MOBA_WIDTH) // MOBA_HEAD_DIM
    gsum = (head_of[:, None] == head_of[None, :]).astype(jnp.bfloat16)
    return pl.pallas_call(
        _mixer_kernel,
        grid=(bsz, nb),
        in_specs=[
            pl.BlockSpec((1, blk, d), lambda b, i: (b, i, 0)),
            _resident((1, d)),
            _resident((d, MAIN_COLS)),
            _resident((2 * MOBA_WIDTH, d)),
            _resident((1, 3 * d)),
            _resident((3, CONV_CH)),
            _resident((1, CONV_CH)),
            _resident((MOBA_WIDTH, blk)),
            _resident((1, MOBA_WIDTH)),
            _resident((MOBA_WIDTH, MOBA_WIDTH)),
            pl.BlockSpec((1, MEM_LEN, MEM_WIDTH), lambda b, i: (b, 0, 0)),
            pl.BlockSpec((1, MEM_LEN, MEM_WIDTH), lambda b, i: (b, 0, 0)),
            _resident((1, MEM_HEAD_DIM)),
            _resident((CONV_CH, d)),
            _resident((MOBA_WIDTH, d)),
            _resident((MEM_WIDTH, d)),
            _resident((d, d)),
        ],
        out_specs=pl.BlockSpec((1, blk, d), lambda b, i: (b, i, 0)),
        out_shape=jax.ShapeDtypeStruct((bsz, s, d), jnp.float32),
        scratch_shapes=[
            pltpu.VMEM((nb, blk, MOBA_WIDTH), jnp.bfloat16),
            pltpu.VMEM((nb, MOBA_HEADS, VT_ROWS, blk), jnp.bfloat16),
            pltpu.VMEM((nbr, MOBA_WIDTH), jnp.float32),
            pltpu.VMEM((SUBLANES + blk, CONV_CH), jnp.float32),
            pltpu.VMEM((MOBA_HEADS, 2 * LANES, blk), jnp.bfloat16),
            pltpu.VMEM((MOBA_HEADS, 2 * blk, blk), jnp.float32),
            pltpu.VMEM((MOBA_HEADS, 2 * blk, blk), jnp.float32),
            pltpu.VMEM((MOBA_HEADS, SUBLANES, blk), jnp.float32),
            pltpu.VMEM((MOBA_HEADS, SUBLANES, blk), jnp.float32),
            pltpu.VMEM((MOBA_HEADS, VT_ROWS, blk), jnp.float32),
            pltpu.VMEM((MOBA_HEADS, SUBLANES, blk), jnp.float32),
        ],
        compiler_params=pltpu.CompilerParams(
            dimension_semantics=("arbitrary", "arbitrary"),
            vmem_limit_bytes=VMEM_LIMIT_MIXER),
        name="mixer",
    )(x, g_mix.reshape(1, d), w_main, w_qvt, b_gate.reshape(1, 3 * d), conv_w, conv_b.reshape(1, CONV_CH),
      qgt, kg, gsum, memk, memv, mqg, _bf16(w_br_conv), _bf16(w_br_moba), _bf16(w_br_mem), _bf16(w_o))


def _ffn_kernel(x_ref, g_ref, wup_ref, cw_ref, cb_ref, wdown_ref, out_ref, aext_s, act_s):
    t = pl.program_id(1)
    tm = FFN_TILE
    x = x_ref[0]
    hb = _bf16(_rmsnorm(x, g_ref[...]))

    @pl.when(t == 0)
    def _():
        aext_s[0:SUBLANES, :] = jnp.zeros((SUBLANES, D_FF), jnp.float32)

    for c0, cw in FFN_CHUNKS:
        cs = slice(c0, c0 + cw)
        a = _dot(hb, wup_ref[:, c0:c0 + cw])
        b = _dot(hb, wup_ref[:, D_FF + c0:D_FF + c0 + cw])
        aext_s[SUBLANES:SUBLANES + tm, cs] = a
        a1 = aext_s[SUBLANES - 1:SUBLANES - 1 + tm, cs]
        a2 = aext_s[SUBLANES - 2:SUBLANES - 2 + tm, cs]
        aext_s[0:SUBLANES, cs] = a[tm - SUBLANES:, :]
        ac = a2 * cw_ref[0:1, cs] + a1 * cw_ref[1:2, cs] + a * cw_ref[2:3, cs] + cb_ref[:, cs]
        act_s[:, cs] = _bf16(ac * jax.nn.sigmoid(ac) * b)
    out_ref[0] = x + _dot(act_s[...], wdown_ref[...])


def _ffn(x, g_ffn, w_up, ffn_conv_w, ffn_conv_b, w_down):
    bsz, s, d = x.shape
    tm = FFN_TILE
    assert s % tm == 0
    return pl.pallas_call(
        _ffn_kernel,
        grid=(bsz, s // tm),
        in_specs=[
            pl.BlockSpec((1, tm, d), lambda b, t: (b, t, 0)),
            _resident((1, d)),
            _resident((d, 2 * D_FF)),
            _resident((3, D_FF)),
            _resident((1, D_FF)),
            _resident((D_FF, d)),
        ],
        out_specs=pl.BlockSpec((1, tm, d), lambda b, t: (b, t, 0)),
        out_shape=jax.ShapeDtypeStruct((bsz, s, d), jnp.float32),
        scratch_shapes=[
            pltpu.VMEM((SUBLANES + tm, D_FF), jnp.float32),
            pltpu.VMEM((tm, D_FF), jnp.bfloat16),
        ],
        compiler_params=pltpu.CompilerParams(
            dimension_semantics=("arbitrary", "arbitrary"),
            vmem_limit_bytes=VMEM_LIMIT_FFN),
        name="ffn",
    )(x, g_ffn.reshape(1, d), _bf16(w_up), ffn_conv_w, ffn_conv_b.reshape(1, D_FF), _bf16(w_down))


def kernel(x, mem, g_mix, w_in, b_gate, conv_w, conv_b, moba_q_gain, moba_k_gain, g_mem, w_mem_kv, memq_gain,
           memk_gain, w_br_conv, w_br_moba, w_br_mem, w_o, g_ffn, w_up, ffn_conv_w, ffn_conv_b, w_down):
    memk, memv = _memkv(mem, g_mem, w_mem_kv, memk_gain)
    x = _mixer(x, memk, memv, g_mix, w_in, b_gate, conv_w, conv_b, moba_q_gain, moba_k_gain, memq_gain,
               w_br_conv, w_br_moba, w_br_mem, w_o)
    return _ffn(x, g_ffn, w_up, ffn_conv_w, ffn_conv_b, w_down)
```

```python
import math

import jax
import jax.numpy as jnp
from jax import lax
from jax.experimental import pallas as pl
from jax.experimental.pallas import tpu as pltpu

D_MODEL = 1024
MEM_LEN = 256
CONV_CH = 512
MOBA_HEADS = 8
MOBA_HEAD_DIM = 64
MOBA_WIDTH = MOBA_HEADS * MOBA_HEAD_DIM
MOBA_BLOCK = 256
MOBA_TOPK = 3
MEM_HEADS = 4
MEM_HEAD_DIM = 128
MEM_WIDTH = MEM_HEADS * MEM_HEAD_DIM
D_FF = 2816
EPS = 1e-6

W_CONV = 0
W_Q = 3 * CONV_CH
W_K = W_Q + MOBA_WIDTH
W_V = W_K + MOBA_WIDTH
W_QMEM = W_V + MOBA_WIDTH
W_GATE = W_QMEM + MEM_WIDTH
IN_COLS = W_GATE + 3 * D_MODEL

LANES = 128
SUBLANES = 8
BF16_ROWS = 16
PAIR = 2 * MOBA_HEAD_DIM
VT_ROWS = MOBA_HEAD_DIM + BF16_ROWS
MASKED = -1e30
VMEM_LIMIT_MIXER = 56 * 1024 * 1024
VMEM_LIMIT_FFN = 48 * 1024 * 1024
FFN_TILE = 512
FFN_CHUNKS = ((0, 1024), (1024, 1024), (2048, 768))

_NT = (((1,), (1,)), ((), ()))


def _dot(a, b):
    return jnp.dot(a, b, preferred_element_type=jnp.float32)


def _dot_nt(a, b):
    return lax.dot_general(a, b, _NT, preferred_element_type=jnp.float32)


def _rmsnorm(x, g):
    return x * lax.rsqrt(jnp.mean(x * x, axis=-1, keepdims=True) + EPS) * g


def _bf16(x):
    return x.astype(jnp.bfloat16)


def _memkv_kernel(mem_ref, g_ref, w_ref, kg_ref, k_ref, v_ref):
    mn = _rmsnorm(mem_ref[0], g_ref[...])
    kv = _dot(_bf16(mn), w_ref[...])
    for h in range(MEM_HEADS):
        lo = h * MEM_HEAD_DIM
        kh = _rmsnorm(kv[:, lo:lo + MEM_HEAD_DIM], kg_ref[...])
        k_ref[0, :, lo:lo + MEM_HEAD_DIM] = _bf16(kh)
    v_ref[0] = _bf16(kv[:, MEM_WIDTH:])


def _memkv(mem, g_mem, w_mem_kv, memk_gain):
    bsz, m, d = mem.shape
    const = lambda b: (0, 0)
    return pl.pallas_call(
        _memkv_kernel,
        grid=(bsz,),
        in_specs=[
            pl.BlockSpec((1, m, d), lambda b: (b, 0, 0)),
            pl.BlockSpec((1, d), const),
            pl.BlockSpec((d, 2 * MEM_WIDTH), const),
            pl.BlockSpec((1, MEM_HEAD_DIM), const),
        ],
        out_specs=[
            pl.BlockSpec((1, m, MEM_WIDTH), lambda b: (b, 0, 0)),
            pl.BlockSpec((1, m, MEM_WIDTH), lambda b: (b, 0, 0)),
        ],
        out_shape=[
            jax.ShapeDtypeStruct((bsz, m, MEM_WIDTH), jnp.bfloat16),
            jax.ShapeDtypeStruct((bsz, m, MEM_WIDTH), jnp.bfloat16),
        ],
        name="memkv",
    )(mem, g_mem.reshape(1, d), _bf16(w_mem_kv), memk_gain.reshape(1, MEM_HEAD_DIM))


def _mixer_kernel(x_ref, gmix_ref, wmain_ref, wqvt_ref, bgate_ref, convw_ref, convb_ref, qgt_ref, kg_ref, gsum_ref,
                  memk_ref, memv_ref, mqg_ref, wbc_ref, wbm_ref, wbx_ref, wo_ref,
                  out_ref,
                  k_s, vt_s, kbar_s, uext_s, rhs_s, sa_s, sb_s, ma_s, mb_s, acc_s, m_s):
    i = pl.program_id(1)
    blk = MOBA_BLOCK
    nbr = kbar_s.shape[0]

    @pl.when(i == 0)
    def _():
        uext_s[0:SUBLANES, :] = jnp.zeros((SUBLANES, CONV_CH), jnp.float32)
        kbar_s[...] = jnp.zeros(kbar_s.shape, jnp.float32)

    x = x_ref[0]
    hb = _bf16(_rmsnorm(x, gmix_ref[...]))

    def proj(lo, width):
        return _dot(hb, wmain_ref[:, lo:lo + width])

    def gate(g, n):
        return jax.nn.sigmoid(g + bgate_ref[:, n * D_MODEL:(n + 1) * D_MODEL])

    pc = proj(W_CONV, 3 * CONV_CH)
    k_raw = proj(W_K, MOBA_WIDTH)
    qvt = _dot_nt(wqvt_ref[...], hb)
    qm = proj(W_QMEM, MEM_WIDTH)

    u = pc[:, CONV_CH:2 * CONV_CH] * pc[:, 2 * CONV_CH:]
    uext_s[SUBLANES:SUBLANES + blk, :] = u
    u1 = uext_s[SUBLANES - 1:SUBLANES - 1 + blk, :]
    u2 = uext_s[SUBLANES - 2:SUBLANES - 2 + blk, :]
    uext_s[0:SUBLANES, :] = u[blk - SUBLANES:, :]
    conv = u2 * convw_ref[0:1, :] + u1 * convw_ref[1:2, :] + u * convw_ref[2:3, :] + convb_ref[...]
    y_conv = _bf16(pc[:, :CONV_CH] * conv)

    k2_b = _bf16(k_raw * k_raw)

    q3 = qvt[:MOBA_WIDTH].reshape(MOBA_HEADS, MOBA_HEAD_DIM, blk)
    ssq = jnp.sum(q3 * q3, axis=1, keepdims=True)
    qbt = _bf16((q3 * lax.rsqrt(ssq * (1.0 / MOBA_HEAD_DIM) + EPS)).reshape(MOBA_WIDTH, blk) * qgt_ref[...])
    vbt = _bf16(qvt[MOBA_WIDTH:])

    qmb = [_bf16(_rmsnorm(qm[:, h * MEM_HEAD_DIM:(h + 1) * MEM_HEAD_DIM], mqg_ref[...])) for h in range(MEM_HEADS)]

    g0 = proj(W_GATE, D_MODEL)
    kss = _dot(k2_b, gsum_ref[...])
    p_conv = _dot(y_conv, wbc_ref[...])
    s_mem = [_dot_nt(qmb[h], memk_ref[0, :, h * MEM_HEAD_DIM:(h + 1) * MEM_HEAD_DIM]) for h in range(MEM_HEADS)]
    g2 = proj(W_GATE + 2 * D_MODEL, D_MODEL)

    merged = gate(g0, 0) * p_conv
    kn = k_raw * lax.rsqrt(kss * (1.0 / MOBA_HEAD_DIM) + EPS) * kg_ref[...]
    kb = _bf16(kn)
    k_s[i] = kb
    kbar_row = jnp.mean(kn, axis=0, keepdims=True)
    kb_rows = lax.broadcasted_iota(jnp.int32, kbar_s.shape, 0)
    kbar_prev = kbar_s[...]
    kbar_s[...] = jnp.where(kb_rows == i, kbar_row, kbar_prev)
    kbar_b = _bf16(kbar_prev)

    zeros_h = jnp.zeros((MOBA_HEAD_DIM, blk), jnp.bfloat16)
    qpairs = []
    for h in range(MOBA_HEADS):
        qh = qbt[h * MOBA_HEAD_DIM:(h + 1) * MOBA_HEAD_DIM]
        qpairs.append(jnp.concatenate([qh, zeros_h] if h % 2 == 0 else [zeros_h, qh], axis=0))

    p_mem = [jnp.exp(s - jnp.max(s, axis=-1, keepdims=True)) for s in s_mem]

    pair_cols = lambda h: slice((h // 2) * PAIR, (h // 2 + 1) * PAIR)
    gates = [_dot(kbar_b[:, pair_cols(h)], qpairs[h]) for h in range(MOBA_HEADS)]
    s_own = [_dot(kb[:, pair_cols(h)], qpairs[h]) for h in range(MOBA_HEADS)]
    o_mem = [_dot(_bf16(p_mem[h]), memv_ref[0, :, h * MEM_HEAD_DIM:(h + 1) * MEM_HEAD_DIM])
             / jnp.sum(p_mem[h], axis=-1, keepdims=True) for h in range(MEM_HEADS)]
    g1 = proj(W_GATE + D_MODEL, D_MODEL)
    merged = merged + gate(g2, 2) * _dot(_bf16(jnp.concatenate(o_mem, axis=-1)), wbx_ref[...])
    gate1 = gate(g1, 1)

    lane = lax.broadcasted_iota(jnp.int32, (blk, LANES), 1)
    brow = lax.broadcasted_iota(jnp.int32, (nbr, blk), 0)
    browf = brow.astype(jnp.float32)
    neg_inf = jnp.float32(-jnp.inf)
    sel_pad = jnp.full((LANES - nbr, blk), MASKED, jnp.bfloat16)
    ones_r = jnp.ones((BF16_ROWS, blk), jnp.bfloat16)
    vts = []
    for h in range(MOBA_HEADS):
        g = jnp.where(brow < i, gates[h], neg_inf)
        sel = jnp.full((nbr, blk), MASKED, jnp.float32)
        for _ in range(MOBA_TOPK):
            mx = jnp.max(g, axis=0, keepdims=True)
            idx = jnp.min(jnp.where(g == mx, browf, float(nbr)), axis=0, keepdims=True)
            pick = browf == idx
            sel = jnp.where(pick & (mx > neg_inf), 0.0, sel)
            g = jnp.where(pick, neg_inf, g)
        rhs_s[h] = jnp.concatenate([qpairs[h], _bf16(sel), sel_pad], axis=0)
        vt = jnp.concatenate([vbt[h * MOBA_HEAD_DIM:(h + 1) * MOBA_HEAD_DIM], ones_r], axis=0)
        vt_s[i, h] = vt
        vts.append(vt)

    def pair_scores(dst, h, tp):
        dst_s, dst_mx = dst
        parts = []
        for jn in (2 * tp, 2 * tp + 1):
            en = jnp.where(lane == jnp.where(jn < i, jn, LANES - 1), 1.0, 0.0).astype(jnp.bfloat16)
            parts.append(jnp.concatenate([k_s[jnp.minimum(jn, i), :, pair_cols(h)], en], axis=1))
        s = _dot(jnp.concatenate(parts, axis=0), rhs_s[h])
        dst_s[h] = s
        dst_mx[h] = jnp.broadcast_to(jnp.max(s, axis=0, keepdims=True), (SUBLANES, blk))

    causal = (lax.broadcasted_iota(jnp.int32, (blk, blk), 0) <= lax.broadcasted_iota(jnp.int32, (blk, blk), 1))
    set_a, set_b = (sa_s, ma_s), (sb_s, mb_s)
    for h in range(MOBA_HEADS):
        pair_scores(set_a, h, 0)
        s = jnp.where(causal, s_own[h], MASKED)
        m0 = jnp.max(s, axis=0, keepdims=True)
        acc_s[h] = _dot(vts[h], jnp.exp2(_bf16(s - m0)))
        m_s[h] = jnp.broadcast_to(m0, (SUBLANES, blk))

    def attend(src, dst, tp):
        src_s, src_mx = src
        ja = jnp.minimum(2 * tp, i)
        jb = jnp.minimum(2 * tp + 1, i)
        for h in range(MOBA_HEADS):
            if dst is not None:
                pair_scores(dst, h, tp + 1)
            m_old = m_s[h]
            m_new = jnp.maximum(m_old, src_mx[h])
            p = jnp.exp2(_bf16(src_s[h] - m_new[0:1]))
            pv = _dot(jnp.concatenate([vt_s[ja, h], vt_s[jb, h]], axis=1), p)
            acc_s[h] = acc_s[h] * jnp.exp2(m_old[0:1] - m_new[0:1]) + pv
            m_s[h] = m_new

    n_pairs = (i + 1) // 2

    def two_pairs(u, carry):
        attend(set_a, set_b, 2 * u)
        attend(set_b, set_a, 2 * u + 1)
        return carry

    lax.fori_loop(0, n_pairs // 2, two_pairs, 0)

    @pl.when(n_pairs % 2 == 1)
    def _():
        attend(set_a, None, n_pairs - 1)

    outs = []
    for h in range(MOBA_HEADS):
        a = acc_s[h]
        outs.append(a[:MOBA_HEAD_DIM] / a[MOBA_HEAD_DIM:MOBA_HEAD_DIM + 1])
    y_moba = jnp.concatenate(outs, axis=0).T
    merged = merged + gate1 * _dot(_bf16(y_moba), wbm_ref[...])

    out_ref[0] = x + _dot(_bf16(merged), wo_ref[...])


def _resident(shape):
    return pl.BlockSpec(shape, lambda b, i: (0,) * len(shape), pipeline_mode=pl.Buffered(1))


def _mixer(x, memk, memv, g_mix, w_in, b_gate, conv_w, conv_b, moba_q_gain, moba_k_gain, memq_gain,
           w_br_conv, w_br_moba, w_br_mem, w_o):
    bsz, s, d = x.shape
    blk = MOBA_BLOCK
    assert s % blk == 0 and d == D_MODEL
    nb = s // blk
    nbr = pl.cdiv(nb, BF16_ROWS) * BF16_ROWS
    assert nbr < LANES
    w_main = _bf16(w_in)
    w_qvt = jnp.concatenate([w_main[:, W_Q:W_K], w_main[:, W_V:W_QMEM]], axis=1).T
    qgt = jnp.broadcast_to(
        (jnp.tile(moba_q_gain, MOBA_HEADS) * (MOBA_HEAD_DIM ** -0.5 * math.log2(math.e)))[:, None], (MOBA_WIDTH, blk))
    kg = jnp.tile(moba_k_gain, MOBA_HEADS).reshape(1, MOBA_WIDTH)
    mqg = memq_gain.reshape(1, MEM_HEAD_DIM) * (MEM_HEAD_DIM ** -0.5)
    head_of = jnp.arange(MOBA_WIDTH) // MOBA_HEAD_DIM
    gsum = (head_of[:, None] == head_of[None, :]).astype(jnp.bfloat16)
    return pl.pallas_call(
        _mixer_kernel,
        grid=(bsz, nb),
        in_specs=[
            pl.BlockSpec((1, blk, d), lambda b, i: (b, i, 0)),
            _resident((1, d)),
            _resident((d, IN_COLS)),
            _resident((2 * MOBA_WIDTH, d)),
            _resident((1, 3 * d)),
            _resident((3, CONV_CH)),
            _resident((1, CONV_CH)),
            _resident((MOBA_WIDTH, blk)),
            _resident((1, MOBA_WIDTH)),
            _resident((MOBA_WIDTH, MOBA_WIDTH)),
            pl.BlockSpec((1, MEM_LEN, MEM_WIDTH), lambda b, i: (b, 0, 0)),
            pl.BlockSpec((1, MEM_LEN, MEM_WIDTH), lambda b, i: (b, 0, 0)),
            _resident((1, MEM_HEAD_DIM)),
            _resident((CONV_CH, d)),
            _resident((MOBA_WIDTH, d)),
            _resident((MEM_WIDTH, d)),
            _resident((d, d)),
        ],
        out_specs=pl.BlockSpec((1, blk, d), lambda b, i: (b, i, 0)),
        out_shape=jax.ShapeDtypeStruct((bsz, s, d), jnp.float32),
        scratch_shapes=[
            pltpu.VMEM((nb, blk, MOBA_WIDTH), jnp.bfloat16),
            pltpu.VMEM((nb, MOBA_HEADS, VT_ROWS, blk), jnp.bfloat16),
            pltpu.VMEM((nbr, MOBA_WIDTH), jnp.float32),
            pltpu.VMEM((SUBLANES + blk, CONV_CH), jnp.float32),
            pltpu.VMEM((MOBA_HEADS, 2 * LANES, blk), jnp.bfloat16),
            pltpu.VMEM((MOBA_HEADS, 2 * blk, blk), jnp.float32),
            pltpu.VMEM((MOBA_HEADS, 2 * blk, blk), jnp.float32),
            pltpu.VMEM((MOBA_HEADS, SUBLANES, blk), jnp.float32),
            pltpu.VMEM((MOBA_HEADS, SUBLANES, blk), jnp.float32),
            pltpu.VMEM((MOBA_HEADS, VT_ROWS, blk), jnp.float32),
            pltpu.VMEM((MOBA_HEADS, SUBLANES, blk), jnp.float32),
        ],
        compiler_params=pltpu.CompilerParams(
            dimension_semantics=("arbitrary", "arbitrary"),
            vmem_limit_bytes=VMEM_LIMIT_MIXER),
        name="mixer",
    )(x, g_mix.reshape(1, d), w_main, w_qvt, b_gate.reshape(1, 3 * d), conv_w, conv_b.reshape(1, CONV_CH),
      qgt, kg, gsum, memk, memv, mqg, _bf16(w_br_conv), _bf16(w_br_moba), _bf16(w_br_mem), _bf16(w_o))


def _ffn_kernel(x_ref, g_ref, wup_ref, cw_ref, cb_ref, wdown_ref, out_ref, aext_s, act_s):
    t = pl.program_id(1)
    tm = FFN_TILE
    x = x_ref[0]
    hb = _bf16(_rmsnorm(x, g_ref[...]))

    @pl.when(t == 0)
    def _():
        aext_s[0:SUBLANES, :] = jnp.zeros((SUBLANES, D_FF), jnp.float32)

    for c0, cw in FFN_CHUNKS:
        cs = slice(c0, c0 + cw)
        a = _dot(hb, wup_ref[:, c0:c0 + cw])
        b = _dot(hb, wup_ref[:, D_FF + c0:D_FF + c0 + cw])
        aext_s[SUBLANES:SUBLANES + tm, cs] = a
        a1 = aext_s[SUBLANES - 1:SUBLANES - 1 + tm, cs]
        a2 = aext_s[SUBLANES - 2:SUBLANES - 2 + tm, cs]
        aext_s[0:SUBLANES, cs] = a[tm - SUBLANES:, :]
        ac = a2 * cw_ref[0:1, cs] + a1 * cw_ref[1:2, cs] + a * cw_ref[2:3, cs] + cb_ref[:, cs]
        act_s[:, cs] = _bf16(ac * jax.nn.sigmoid(ac) * b)
    out_ref[0] = x + _dot(act_s[...], wdown_ref[...])


def _ffn(x, g_ffn, w_up, ffn_conv_w, ffn_conv_b, w_down):
    bsz, s, d = x.shape
    tm = FFN_TILE
    assert s % tm == 0
    return pl.pallas_call(
        _ffn_kernel,
        grid=(bsz, s // tm),
        in_specs=[
            pl.BlockSpec((1, tm, d), lambda b, t: (b, t, 0)),
            _resident((1, d)),
            _resident((d, 2 * D_FF)),
            _resident((3, D_FF)),
            _resident((1, D_FF)),
            _resident((D_FF, d)),
        ],
        out_specs=pl.BlockSpec((1, tm, d), lambda b, t: (b, t, 0)),
        out_shape=jax.ShapeDtypeStruct((bsz, s, d), jnp.float32),
        scratch_shapes=[
            pltpu.VMEM((SUBLANES + tm, D_FF), jnp.float32),
            pltpu.VMEM((tm, D_FF), jnp.bfloat16),
        ],
        compiler_params=pltpu.CompilerParams(
            dimension_semantics=("arbitrary", "arbitrary"),
            vmem_limit_bytes=VMEM_LIMIT_FFN),
        name="ffn",
    )(x, g_ffn.reshape(1, d), _bf16(w_up), ffn_conv_w, ffn_conv_b.reshape(1, D_FF), _bf16(w_down))


def kernel(x, mem, g_mix, w_in, b_gate, conv_w, conv_b, moba_q_gain, moba_k_gain, g_mem, w_mem_kv, memq_gain,
           memk_gain, w_br_conv, w_br_moba, w_br_mem, w_o, g_ffn, w_up, ffn_conv_w, ffn_conv_b, w_down):
    memk, memv = _memkv(mem, g_mem, w_mem_kv, memk_gain)
    x = _mixer(x, memk, memv, g_mix, w_in, b_gate, conv_w, conv_b, moba_q_gain, moba_k_gain, memq_gain,
               w_br_conv, w_br_moba, w_br_mem, w_o)
    return _ffn(x, g_ffn, w_up, ffn_conv_w, ffn_conv_b, w_down)
```

```python
import math

import jax
import jax.numpy as jnp
from jax import lax
from jax.experimental import pallas as pl
from jax.experimental.pallas import tpu as pltpu

D_MODEL = 1024
MEM_LEN = 256
CONV_CH = 512
MOBA_HEADS = 8
MOBA_HEAD_DIM = 64
MOBA_WIDTH = MOBA_HEADS * MOBA_HEAD_DIM
MOBA_BLOCK = 256
MOBA_TOPK = 3
MEM_HEADS = 4
MEM_HEAD_DIM = 128
MEM_WIDTH = MEM_HEADS * MEM_HEAD_DIM
D_FF = 2816
EPS = 1e-6

W_CONV = 0
W_Q = 3 * CONV_CH
W_K = W_Q + MOBA_WIDTH
W_V = W_K + MOBA_WIDTH
W_QMEM = W_V + MOBA_WIDTH
W_GATE = W_QMEM + MEM_WIDTH
IN_COLS = W_GATE + 3 * D_MODEL

LANES = 128
SUBLANES = 8
BF16_ROWS = 16
PAIR = 2 * MOBA_HEAD_DIM
VT_ROWS = MOBA_HEAD_DIM + BF16_ROWS
MASKED = -1e30
MIXER_SUBS = 2
VMEM_LIMIT_MIXER = 60 * 1024 * 1024
VMEM_LIMIT_FFN = 48 * 1024 * 1024
FFN_TILE = 512
FFN_CHUNKS = ((0, 1024), (1024, 1024), (2048, 768))

_NT = (((1,), (1,)), ((), ()))


def _dot(a, b):
    return jnp.dot(a, b, preferred_element_type=jnp.float32)


def _dot_nt(a, b):
    return lax.dot_general(a, b, _NT, preferred_element_type=jnp.float32)


def _rmsnorm(x, g):
    return x * lax.rsqrt(jnp.mean(x * x, axis=-1, keepdims=True) + EPS) * g


def _bf16(x):
    return x.astype(jnp.bfloat16)


def _memkv_kernel(mem_ref, g_ref, w_ref, kg_ref, k_ref, v_ref):
    mn = _rmsnorm(mem_ref[0], g_ref[...])
    kv = _dot(_bf16(mn), w_ref[...])
    for h in range(MEM_HEADS):
        lo = h * MEM_HEAD_DIM
        kh = _rmsnorm(kv[:, lo:lo + MEM_HEAD_DIM], kg_ref[...])
        k_ref[0, :, lo:lo + MEM_HEAD_DIM] = _bf16(kh)
    v_ref[0] = _bf16(kv[:, MEM_WIDTH:])


def _memkv(mem, g_mem, w_mem_kv, memk_gain):
    bsz, m, d = mem.shape
    const = lambda b: (0, 0)
    return pl.pallas_call(
        _memkv_kernel,
        grid=(bsz,),
        in_specs=[
            pl.BlockSpec((1, m, d), lambda b: (b, 0, 0)),
            pl.BlockSpec((1, d), const),
            pl.BlockSpec((d, 2 * MEM_WIDTH), const),
            pl.BlockSpec((1, MEM_HEAD_DIM), const),
        ],
        out_specs=[
            pl.BlockSpec((1, m, MEM_WIDTH), lambda b: (b, 0, 0)),
            pl.BlockSpec((1, m, MEM_WIDTH), lambda b: (b, 0, 0)),
        ],
        out_shape=[
            jax.ShapeDtypeStruct((bsz, m, MEM_WIDTH), jnp.bfloat16),
            jax.ShapeDtypeStruct((bsz, m, MEM_WIDTH), jnp.bfloat16),
        ],
        name="memkv",
    )(mem, g_mem.reshape(1, d), _bf16(w_mem_kv), memk_gain.reshape(1, MEM_HEAD_DIM))


def _mixer_kernel(*refs):
    def sub_block(sub, carry):
        _mixer_block(pl.program_id(1) * MIXER_SUBS + sub, pl.multiple_of(sub * MOBA_BLOCK, MOBA_BLOCK), *refs)
        return carry

    lax.fori_loop(0, MIXER_SUBS, sub_block, 0)


def _mixer_block(i, row0, x_ref, gmix_ref, wmain_ref, wqvt_ref, bgate_ref, convw_ref, convb_ref, qgt_ref, kg_ref,
                 gsum_ref, memk_ref, memv_ref, mqg_ref, wbc_ref, wbm_ref, wbx_ref, wo_ref,
                 out_ref,
                 k_s, vt_s, kbar_s, uext_s, rhs_s, sa_s, sb_s, ma_s, mb_s, acc_s, m_s):
    blk = MOBA_BLOCK
    nbr = kbar_s.shape[0]

    @pl.when(i == 0)
    def _():
        uext_s[0:SUBLANES, :] = jnp.zeros((SUBLANES, CONV_CH), jnp.float32)
        kbar_s[...] = jnp.zeros(kbar_s.shape, jnp.float32)

    x = x_ref[0, pl.ds(row0, blk), :]
    hb = _bf16(_rmsnorm(x, gmix_ref[...]))

    def proj(lo, width):
        return _dot(hb, wmain_ref[:, lo:lo + width])

    def gate(g, n):
        return jax.nn.sigmoid(g + bgate_ref[:, n * D_MODEL:(n + 1) * D_MODEL])

    pc = proj(W_CONV, 3 * CONV_CH)
    k_raw = proj(W_K, MOBA_WIDTH)
    qvt = _dot_nt(wqvt_ref[...], hb)
    qm = proj(W_QMEM, MEM_WIDTH)

    u = pc[:, CONV_CH:2 * CONV_CH] * pc[:, 2 * CONV_CH:]
    uext_s[SUBLANES:SUBLANES + blk, :] = u
    u1 = uext_s[SUBLANES - 1:SUBLANES - 1 + blk, :]
    u2 = uext_s[SUBLANES - 2:SUBLANES - 2 + blk, :]
    uext_s[0:SUBLANES, :] = u[blk - SUBLANES:, :]
    conv = u2 * convw_ref[0:1, :] + u1 * convw_ref[1:2, :] + u * convw_ref[2:3, :] + convb_ref[...]
    y_conv = _bf16(pc[:, :CONV_CH] * conv)

    k2_b = _bf16(k_raw * k_raw)

    q3 = qvt[:MOBA_WIDTH].reshape(MOBA_HEADS, MOBA_HEAD_DIM, blk)
    ssq = jnp.sum(q3 * q3, axis=1, keepdims=True)
    qbt = _bf16((q3 * lax.rsqrt(ssq * (1.0 / MOBA_HEAD_DIM) + EPS)).reshape(MOBA_WIDTH, blk) * qgt_ref[...])
    vbt = _bf16(qvt[MOBA_WIDTH:])

    qmb = [_bf16(_rmsnorm(qm[:, h * MEM_HEAD_DIM:(h + 1) * MEM_HEAD_DIM], mqg_ref[...])) for h in range(MEM_HEADS)]

    g0 = proj(W_GATE, D_MODEL)
    kss = _dot(k2_b, gsum_ref[...])
    p_conv = _dot(y_conv, wbc_ref[...])
    s_mem = [_dot_nt(qmb[h], memk_ref[0, :, h * MEM_HEAD_DIM:(h + 1) * MEM_HEAD_DIM]) for h in range(MEM_HEADS)]
    g2 = proj(W_GATE + 2 * D_MODEL, D_MODEL)

    merged = gate(g0, 0) * p_conv
    kn = k_raw * lax.rsqrt(kss * (1.0 / MOBA_HEAD_DIM) + EPS) * kg_ref[...]
    kb = _bf16(kn)
    k_s[i] = kb
    kbar_row = jnp.mean(kn, axis=0, keepdims=True)
    kb_rows = lax.broadcasted_iota(jnp.int32, kbar_s.shape, 0)
    kbar_prev = kbar_s[...]
    kbar_s[...] = jnp.where(kb_rows == i, kbar_row, kbar_prev)
    kbar_b = _bf16(kbar_prev)

    zeros_h = jnp.zeros((MOBA_HEAD_DIM, blk), jnp.bfloat16)
    qpairs = []
    for h in range(MOBA_HEADS):
        qh = qbt[h * MOBA_HEAD_DIM:(h + 1) * MOBA_HEAD_DIM]
        qpairs.append(jnp.concatenate([qh, zeros_h] if h % 2 == 0 else [zeros_h, qh], axis=0))

    p_mem = [jnp.exp(s - jnp.max(s, axis=-1, keepdims=True)) for s in s_mem]

    pair_cols = lambda h: slice((h // 2) * PAIR, (h // 2 + 1) * PAIR)
    gates = [_dot(kbar_b[:, pair_cols(h)], qpairs[h]) for h in range(MOBA_HEADS)]
    s_own = [_dot(kb[:, pair_cols(h)], qpairs[h]) for h in range(MOBA_HEADS)]
    o_mem = [_dot(_bf16(p_mem[h]), memv_ref[0, :, h * MEM_HEAD_DIM:(h + 1) * MEM_HEAD_DIM])
             / jnp.sum(p_mem[h], axis=-1, keepdims=True) for h in range(MEM_HEADS)]
    g1 = proj(W_GATE + D_MODEL, D_MODEL)
    merged = merged + gate(g2, 2) * _dot(_bf16(jnp.concatenate(o_mem, axis=-1)), wbx_ref[...])
    gate1 = gate(g1, 1)

    lane = lax.broadcasted_iota(jnp.int32, (blk, LANES), 1)
    brow = lax.broadcasted_iota(jnp.int32, (nbr, blk), 0)
    browf = brow.astype(jnp.float32)
    neg_inf = jnp.float32(-jnp.inf)
    sel_pad = jnp.full((LANES - nbr, blk), MASKED, jnp.bfloat16)
    ones_r = jnp.ones((BF16_ROWS, blk), jnp.bfloat16)
    vts = []
    for h in range(MOBA_HEADS):
        g = jnp.where(brow < i, gates[h], neg_inf)
        sel = jnp.full((nbr, blk), MASKED, jnp.float32)
        for _ in range(MOBA_TOPK):
            mx = jnp.max(g, axis=0, keepdims=True)
            idx = jnp.min(jnp.where(g == mx, browf, float(nbr)), axis=0, keepdims=True)
            pick = browf == idx
            sel = jnp.where(pick & (mx > neg_inf), 0.0, sel)
            g = jnp.where(pick, neg_inf, g)
        rhs_s[h] = jnp.concatenate([qpairs[h], _bf16(sel), sel_pad], axis=0)
        vt = jnp.concatenate([vbt[h * MOBA_HEAD_DIM:(h + 1) * MOBA_HEAD_DIM], ones_r], axis=0)
        vt_s[i, h] = vt
        vts.append(vt)

    def pair_scores(dst, h, tp):
        dst_s, dst_mx = dst
        parts = []
        for jn in (2 * tp, 2 * tp + 1):
            en = jnp.where(lane == jnp.where(jn < i, jn, LANES - 1), 1.0, 0.0).astype(jnp.bfloat16)
            parts.append(jnp.concatenate([k_s[jnp.minimum(jn, i), :, pair_cols(h)], en], axis=1))
        s = _dot(jnp.concatenate(parts, axis=0), rhs_s[h])
        dst_s[h] = s
        dst_mx[h] = jnp.broadcast_to(jnp.max(s, axis=0, keepdims=True), (SUBLANES, blk))

    causal = (lax.broadcasted_iota(jnp.int32, (blk, blk), 0) <= lax.broadcasted_iota(jnp.int32, (blk, blk), 1))
    set_a, set_b = (sa_s, ma_s), (sb_s, mb_s)
    for h in range(MOBA_HEADS):
        pair_scores(set_a, h, 0)
        s = jnp.where(causal, s_own[h], MASKED)
        m0 = jnp.max(s, axis=0, keepdims=True)
        acc_s[h] = _dot(vts[h], jnp.exp2(_bf16(s - m0)))
        m_s[h] = jnp.broadcast_to(m0, (SUBLANES, blk))

    def attend(src, dst, tp):
        src_s, src_mx = src
        ja = jnp.minimum(2 * tp, i)
        jb = jnp.minimum(2 * tp + 1, i)
        for h in range(MOBA_HEADS):
            if dst is not None:
                pair_scores(dst, h, tp + 1)
            m_old = m_s[h]
            m_new = jnp.maximum(m_old, src_mx[h])
            p = jnp.exp2(_bf16(src_s[h] - m_new[0:1]))
            pv = _dot(jnp.concatenate([vt_s[ja, h], vt_s[jb, h]], axis=1), p)
            acc_s[h] = acc_s[h] * jnp.exp2(m_old[0:1] - m_new[0:1]) + pv
            m_s[h] = m_new

    n_pairs = (i + 1) // 2

    def two_pairs(u, carry):
        attend(set_a, set_b, 2 * u)
        attend(set_b, set_a, 2 * u + 1)
        return carry

    lax.fori_loop(0, n_pairs // 2, two_pairs, 0)

    @pl.when(n_pairs % 2 == 1)
    def _():
        attend(set_a, None, n_pairs - 1)

    outs = []
    for h in range(MOBA_HEADS):
        a = acc_s[h]
        outs.append(a[:MOBA_HEAD_DIM] / a[MOBA_HEAD_DIM:MOBA_HEAD_DIM + 1])
    y_moba = jnp.concatenate(outs, axis=0).T
    merged = merged + gate1 * _dot(_bf16(y_moba), wbm_ref[...])

    out_ref[0, pl.ds(row0, blk), :] = x + _dot(_bf16(merged), wo_ref[...])


def _resident(shape):
    return pl.BlockSpec(shape, lambda b, i: (0,) * len(shape), pipeline_mode=pl.Buffered(1))


def _mixer(x, memk, memv, g_mix, w_in, b_gate, conv_w, conv_b, moba_q_gain, moba_k_gain, memq_gain,
           w_br_conv, w_br_moba, w_br_mem, w_o):
    bsz, s, d = x.shape
    blk = MOBA_BLOCK
    win = MIXER_SUBS * blk
    assert s % win == 0 and d == D_MODEL
    nb = s // blk
    nbr = pl.cdiv(nb, BF16_ROWS) * BF16_ROWS
    assert nbr < LANES
    w_main = _bf16(w_in)
    w_qvt = _bf16(jnp.concatenate([w_in[:, W_Q:W_K].T, w_in[:, W_V:W_QMEM].T], axis=0))
    qgt = jnp.broadcast_to(
        (jnp.tile(moba_q_gain, MOBA_HEADS) * (MOBA_HEAD_DIM ** -0.5 * math.log2(math.e)))[:, None], (MOBA_WIDTH, blk))
    kg = jnp.tile(moba_k_gain, MOBA_HEADS).reshape(1, MOBA_WIDTH)
    mqg = memq_gain.reshape(1, MEM_HEAD_DIM) * (MEM_HEAD_DIM ** -0.5)
    head_of = jnp.arange(MOBA_WIDTH) // MOBA_HEAD_DIM
    gsum = (head_of[:, None] == head_of[None, :]).astype(jnp.bfloat16)
    return pl.pallas_call(
        _mixer_kernel,
        grid=(bsz, s // win),
        in_specs=[
            pl.BlockSpec((1, win, d), lambda b, i: (b, i, 0)),
            _resident((1, d)),
            _resident((d, IN_COLS)),
            _resident((2 * MOBA_WIDTH, d)),
            _resident((1, 3 * d)),
            _resident((3, CONV_CH)),
            _resident((1, CONV_CH)),
            _resident((MOBA_WIDTH, blk)),
            _resident((1, MOBA_WIDTH)),
            _resident((MOBA_WIDTH, MOBA_WIDTH)),
            pl.BlockSpec((1, MEM_LEN, MEM_WIDTH), lambda b, i: (b, 0, 0)),
            pl.BlockSpec((1, MEM_LEN, MEM_WIDTH), lambda b, i: (b, 0, 0)),
            _resident((1, MEM_HEAD_DIM)),
            _resident((CONV_CH, d)),
            _resident((MOBA_WIDTH, d)),
            _resident((MEM_WIDTH, d)),
            _resident((d, d)),
        ],
        out_specs=pl.BlockSpec((1, win, d), lambda b, i: (b, i, 0)),
        out_shape=jax.ShapeDtypeStruct((bsz, s, d), jnp.float32),
        scratch_shapes=[
            pltpu.VMEM((nb, blk, MOBA_WIDTH), jnp.bfloat16),
            pltpu.VMEM((nb, MOBA_HEADS, VT_ROWS, blk), jnp.bfloat16),
            pltpu.VMEM((nbr, MOBA_WIDTH), jnp.float32),
            pltpu.VMEM((SUBLANES + blk, CONV_CH), jnp.float32),
            pltpu.VMEM((MOBA_HEADS, 2 * LANES, blk), jnp.bfloat16),
            pltpu.VMEM((MOBA_HEADS, 2 * blk, blk), jnp.float32),
            pltpu.VMEM((MOBA_HEADS, 2 * blk, blk), jnp.float32),
            pltpu.VMEM((MOBA_HEADS, SUBLANES, blk), jnp.float32),
            pltpu.VMEM((MOBA_HEADS, SUBLANES, blk), jnp.float32),
            pltpu.VMEM((MOBA_HEADS, VT_ROWS, blk), jnp.float32),
            pltpu.VMEM((MOBA_HEADS, SUBLANES, blk), jnp.float32),
        ],
        compiler_params=pltpu.CompilerParams(
            dimension_semantics=("arbitrary", "arbitrary"),
            vmem_limit_bytes=VMEM_LIMIT_MIXER),
        name="mixer",
    )(x, g_mix.reshape(1, d), w_main, w_qvt, b_gate.reshape(1, 3 * d), conv_w, conv_b.reshape(1, CONV_CH),
      qgt, kg, gsum, memk, memv, mqg, _bf16(w_br_conv), _bf16(w_br_moba), _bf16(w_br_mem), _bf16(w_o))


def _ffn_kernel(x_ref, g_ref, wup_ref, cw_ref, cb_ref, wdown_ref, out_ref, aext_s, act_s):
    t = pl.program_id(1)
    tm = FFN_TILE
    x = x_ref[0]
    hb = _bf16(_rmsnorm(x, g_ref[...]))

    @pl.when(t == 0)
    def _():
        aext_s[0:SUBLANES, :] = jnp.zeros((SUBLANES, D_FF), jnp.float32)

    for c0, cw in FFN_CHUNKS:
        cs = slice(c0, c0 + cw)
        a = _dot(hb, wup_ref[:, c0:c0 + cw])
        b = _dot(hb, wup_ref[:, D_FF + c0:D_FF + c0 + cw])
        aext_s[SUBLANES:SUBLANES + tm, cs] = a
        a1 = aext_s[SUBLANES - 1:SUBLANES - 1 + tm, cs]
        a2 = aext_s[SUBLANES - 2:SUBLANES - 2 + tm, cs]
        aext_s[0:SUBLANES, cs] = a[tm - SUBLANES:, :]
        ac = a2 * cw_ref[0:1, cs] + a1 * cw_ref[1:2, cs] + a * cw_ref[2:3, cs] + cb_ref[:, cs]
        act_s[:, cs] = _bf16(ac * jax.nn.sigmoid(ac) * b)
    out_ref[0] = x + _dot(act_s[...], wdown_ref[...])


def _ffn(x, g_ffn, w_up, ffn_conv_w, ffn_conv_b, w_down):
    bsz, s, d = x.shape
    tm = FFN_TILE
    assert s % tm == 0
    return pl.pallas_call(
        _ffn_kernel,
        grid=(bsz, s // tm),
        in_specs=[
            pl.BlockSpec((1, tm, d), lambda b, t: (b, t, 0)),
            _resident((1, d)),
            _resident((d, 2 * D_FF)),
            _resident((3, D_FF)),
            _resident((1, D_FF)),
            _resident((D_FF, d)),
        ],
        out_specs=pl.BlockSpec((1, tm, d), lambda b, t: (b, t, 0)),
        out_shape=jax.ShapeDtypeStruct((bsz, s, d), jnp.float32),
        scratch_shapes=[
            pltpu.VMEM((SUBLANES + tm, D_FF), jnp.float32),
            pltpu.VMEM((tm, D_FF), jnp.bfloat16),
        ],
        compiler_params=pltpu.CompilerParams(
            dimension_semantics=("arbitrary", "arbitrary"),
            vmem_limit_bytes=VMEM_LIMIT_FFN),
        name="ffn",
    )(x, g_ffn.reshape(1, d), _bf16(w_up), ffn_conv_w, ffn_conv_b.reshape(1, D_FF), _bf16(w_down))


def kernel(x, mem, g_mix, w_in, b_gate, conv_w, conv_b, moba_q_gain, moba_k_gain, g_mem, w_mem_kv, memq_gain,
           memk_gain, w_br_conv, w_br_moba, w_br_mem, w_o, g_ffn, w_up, ffn_conv_w, ffn_conv_b, w_down):
    memk, memv = _memkv(mem, g_mem, w_mem_kv, memk_gain)
    x = _mixer(x, memk, memv, g_mix, w_in, b_gate, conv_w, conv_b, moba_q_gain, moba_k_gain, memq_gain,
               w_br_conv, w_br_moba, w_br_mem, w_o)
    return _ffn(x, g_ffn, w_up, ffn_conv_w, ffn_conv_b, w_down)
```

```python
import math

import jax
import jax.numpy as jnp
from jax import lax
from jax.experimental import pallas as pl
from jax.experimental.pallas import tpu as pltpu

D_MODEL = 1024
MEM_LEN = 256
CONV_CH = 512
MOBA_HEADS = 8
MOBA_HEAD_DIM = 64
MOBA_WIDTH = MOBA_HEADS * MOBA_HEAD_DIM
MOBA_BLOCK = 256
MOBA_TOPK = 3
MEM_HEADS = 4
MEM_HEAD_DIM = 128
MEM_WIDTH = MEM_HEADS * MEM_HEAD_DIM
D_FF = 2816
EPS = 1e-6

W_CONV = 0
W_Q = 3 * CONV_CH
W_K = W_Q + MOBA_WIDTH
W_V = W_K + MOBA_WIDTH
W_QMEM = W_V + MOBA_WIDTH
W_GATE = W_QMEM + MEM_WIDTH
IN_COLS = W_GATE + 3 * D_MODEL

LANES = 128
SUBLANES = 8
BF16_ROWS = 16
PAIR = 2 * MOBA_HEAD_DIM
VT_ROWS = MOBA_HEAD_DIM + BF16_ROWS
MASKED = -1e30
VMEM_LIMIT_MIXER = 56 * 1024 * 1024
VMEM_LIMIT_FFN = 48 * 1024 * 1024
FFN_TILE = 512
FFN_CHUNKS = ((0, 1024), (1024, 1024), (2048, 768))

_NT = (((1,), (1,)), ((), ()))


def _dot(a, b):
    return jnp.dot(a, b, preferred_element_type=jnp.float32)


def _dot_nt(a, b):
    return lax.dot_general(a, b, _NT, preferred_element_type=jnp.float32)


def _rmsnorm(x, g):
    return x * lax.rsqrt(jnp.mean(x * x, axis=-1, keepdims=True) + EPS) * g


def _bf16(x):
    return x.astype(jnp.bfloat16)


def _memkv_kernel(mem_ref, g_ref, w_ref, kg_ref, k_ref, v_ref):
    mn = _rmsnorm(mem_ref[0], g_ref[...])
    kv = _dot(_bf16(mn), w_ref[...])
    for h in range(MEM_HEADS):
        lo = h * MEM_HEAD_DIM
        kh = _rmsnorm(kv[:, lo:lo + MEM_HEAD_DIM], kg_ref[...])
        k_ref[0, :, lo:lo + MEM_HEAD_DIM] = _bf16(kh)
    v_ref[0] = _bf16(kv[:, MEM_WIDTH:])


def _memkv(mem, g_mem, w_mem_kv, memk_gain):
    bsz, m, d = mem.shape
    const = lambda b: (0, 0)
    return pl.pallas_call(
        _memkv_kernel,
        grid=(bsz,),
        in_specs=[
            pl.BlockSpec((1, m, d), lambda b: (b, 0, 0)),
            pl.BlockSpec((1, d), const),
            pl.BlockSpec((d, 2 * MEM_WIDTH), const),
            pl.BlockSpec((1, MEM_HEAD_DIM), const),
        ],
        out_specs=[
            pl.BlockSpec((1, m, MEM_WIDTH), lambda b: (b, 0, 0)),
            pl.BlockSpec((1, m, MEM_WIDTH), lambda b: (b, 0, 0)),
        ],
        out_shape=[
            jax.ShapeDtypeStruct((bsz, m, MEM_WIDTH), jnp.bfloat16),
            jax.ShapeDtypeStruct((bsz, m, MEM_WIDTH), jnp.bfloat16),
        ],
        name="memkv",
    )(mem, g_mem.reshape(1, d), _bf16(w_mem_kv), memk_gain.reshape(1, MEM_HEAD_DIM))


def _mixer_kernel(x_ref, gmix_ref, wmain_ref, wqvt_ref, bgate_ref, convw_ref, convb_ref, qgt_ref, kg_ref, gsum_ref,
                  memk_ref, memv_ref, mqg_ref, wbc_ref, wbm_ref, wbx_ref, wo_ref,
                  out_ref,
                  k_s, vt_s, kbar_s, uext_s, rhs_s, sa_s, sb_s, ma_s, mb_s, acc_s, m_s):
    i = pl.program_id(1)
    blk = MOBA_BLOCK
    nbr = kbar_s.shape[0]

    @pl.when(i == 0)
    def _():
        uext_s[0:SUBLANES, :] = jnp.zeros((SUBLANES, CONV_CH), jnp.float32)
        kbar_s[...] = jnp.zeros(kbar_s.shape, jnp.float32)

    x = x_ref[0]
    hb = _bf16(_rmsnorm(x, gmix_ref[...]))

    def proj(lo, width):
        return _dot(hb, wmain_ref[:, lo:lo + width])

    def gate(g, n):
        return jax.nn.sigmoid(g + bgate_ref[:, n * D_MODEL:(n + 1) * D_MODEL])

    pc = proj(W_CONV, 3 * CONV_CH)
    k_raw = proj(W_K, MOBA_WIDTH)
    qvt = _dot_nt(wqvt_ref[...], hb)
    qm = proj(W_QMEM, MEM_WIDTH)

    u = pc[:, CONV_CH:2 * CONV_CH] * pc[:, 2 * CONV_CH:]
    uext_s[SUBLANES:SUBLANES + blk, :] = u
    u1 = uext_s[SUBLANES - 1:SUBLANES - 1 + blk, :]
    u2 = uext_s[SUBLANES - 2:SUBLANES - 2 + blk, :]
    uext_s[0:SUBLANES, :] = u[blk - SUBLANES:, :]
    conv = u2 * convw_ref[0:1, :] + u1 * convw_ref[1:2, :] + u * convw_ref[2:3, :] + convb_ref[...]
    y_conv = _bf16(pc[:, :CONV_CH] * conv)

    k2_b = _bf16(k_raw * k_raw)

    q3 = qvt[:MOBA_WIDTH].reshape(MOBA_HEADS, MOBA_HEAD_DIM, blk)
    ssq = jnp.sum(q3 * q3, axis=1, keepdims=True)
    qbt = _bf16((q3 * lax.rsqrt(ssq * (1.0 / MOBA_HEAD_DIM) + EPS)).reshape(MOBA_WIDTH, blk) * qgt_ref[...])
    vbt = _bf16(qvt[MOBA_WIDTH:])

    qmb = [_bf16(_rmsnorm(qm[:, h * MEM_HEAD_DIM:(h + 1) * MEM_HEAD_DIM], mqg_ref[...])) for h in range(MEM_HEADS)]

    g0 = proj(W_GATE, D_MODEL)
    kss = _dot(k2_b, gsum_ref[...])
    p_conv = _dot(y_conv, wbc_ref[...])
    s_mem = [_dot_nt(qmb[h], memk_ref[0, :, h * MEM_HEAD_DIM:(h + 1) * MEM_HEAD_DIM]) for h in range(MEM_HEADS)]
    g2 = proj(W_GATE + 2 * D_MODEL, D_MODEL)

    merged = gate(g0, 0) * p_conv
    kn = k_raw * lax.rsqrt(kss * (1.0 / MOBA_HEAD_DIM) + EPS) * kg_ref[...]
    kb = _bf16(kn)
    k_s[i] = kb
    kbar_row = jnp.mean(kn, axis=0, keepdims=True)
    kb_rows = lax.broadcasted_iota(jnp.int32, kbar_s.shape, 0)
    kbar_prev = kbar_s[...]
    kbar_s[...] = jnp.where(kb_rows == i, kbar_row, kbar_prev)
    kbar_b = _bf16(kbar_prev)

    zeros_h = jnp.zeros((MOBA_HEAD_DIM, blk), jnp.bfloat16)
    qpairs = []
    for h in range(MOBA_HEADS):
        qh = qbt[h * MOBA_HEAD_DIM:(h + 1) * MOBA_HEAD_DIM]
        qpairs.append(jnp.concatenate([qh, zeros_h] if h % 2 == 0 else [zeros_h, qh], axis=0))

    p_mem = [jnp.exp(s - jnp.max(s, axis=-1, keepdims=True)) for s in s_mem]

    pair_cols = lambda h: slice((h // 2) * PAIR, (h // 2 + 1) * PAIR)
    gates = [_dot(kbar_b[:, pair_cols(h)], qpairs[h]) for h in range(MOBA_HEADS)]
    s_own = [_dot(kb[:, pair_cols(h)], qpairs[h]) for h in range(MOBA_HEADS)]
    o_mem = [_dot(_bf16(p_mem[h]), memv_ref[0, :, h * MEM_HEAD_DIM:(h + 1) * MEM_HEAD_DIM])
             / jnp.sum(p_mem[h], axis=-1, keepdims=True) for h in range(MEM_HEADS)]
    g1 = proj(W_GATE + D_MODEL, D_MODEL)
    merged = merged + gate(g2, 2) * _dot(_bf16(jnp.concatenate(o_mem, axis=-1)), wbx_ref[...])
    gate1 = gate(g1, 1)

    lane = lax.broadcasted_iota(jnp.int32, (blk, LANES), 1)
    brow = lax.broadcasted_iota(jnp.int32, (nbr, blk), 0)
    browf = brow.astype(jnp.float32)
    neg_inf = jnp.float32(-jnp.inf)
    sel_pad = jnp.full((LANES - nbr, blk), MASKED, jnp.bfloat16)
    ones_r = jnp.ones((BF16_ROWS, blk), jnp.bfloat16)
    vts = []
    for h in range(MOBA_HEADS):
        g = jnp.where(brow < i, gates[h], neg_inf)
        sel = jnp.full((nbr, blk), MASKED, jnp.float32)
        for _ in range(MOBA_TOPK):
            mx = jnp.max(g, axis=0, keepdims=True)
            idx = jnp.min(jnp.where(g == mx, browf, float(nbr)), axis=0, keepdims=True)
            pick = browf == idx
            sel = jnp.where(pick & (mx > neg_inf), 0.0, sel)
            g = jnp.where(pick, neg_inf, g)
        rhs_s[h] = jnp.concatenate([qpairs[h], _bf16(sel), sel_pad], axis=0)
        vt = jnp.concatenate([vbt[h * MOBA_HEAD_DIM:(h + 1) * MOBA_HEAD_DIM], ones_r], axis=0)
        vt_s[i, h] = vt
        vts.append(vt)

    def pair_scores(dst, h, tp):
        dst_s, dst_mx = dst
        parts = []
        for jn in (2 * tp, 2 * tp + 1):
            en = jnp.where(lane == jnp.where(jn < i, jn, LANES - 1), 1.0, 0.0).astype(jnp.bfloat16)
            parts.append(jnp.concatenate([k_s[jnp.minimum(jn, i), :, pair_cols(h)], en], axis=1))
        s = _dot(jnp.concatenate(parts, axis=0), rhs_s[h])
        dst_s[h] = s
        dst_mx[h] = jnp.broadcast_to(jnp.max(s, axis=0, keepdims=True), (SUBLANES, blk))

    causal = (lax.broadcasted_iota(jnp.int32, (blk, blk), 0) <= lax.broadcasted_iota(jnp.int32, (blk, blk), 1))
    set_a, set_b = (sa_s, ma_s), (sb_s, mb_s)
    for h in range(MOBA_HEADS):
        pair_scores(set_a, h, 0)
        s = jnp.where(causal, s_own[h], MASKED)
        m0 = jnp.max(s, axis=0, keepdims=True)
        acc_s[h] = _dot(vts[h], jnp.exp2(_bf16(s - m0)))
        m_s[h] = jnp.broadcast_to(m0, (SUBLANES, blk))

    def attend(src, dst, tp):
        src_s, src_mx = src
        ja = jnp.minimum(2 * tp, i)
        jb = jnp.minimum(2 * tp + 1, i)
        for h in range(MOBA_HEADS):
            if dst is not None:
                pair_scores(dst, h, tp + 1)
            m_old = m_s[h]
            m_new = jnp.maximum(m_old, src_mx[h])
            p = jnp.exp2(_bf16(src_s[h] - m_new[0:1]))
            pv = _dot(jnp.concatenate([vt_s[ja, h], vt_s[jb, h]], axis=1), p)
            acc_s[h] = acc_s[h] * jnp.exp2(m_old[0:1] - m_new[0:1]) + pv
            m_s[h] = m_new

    n_pairs = (i + 1) // 2

    def two_pairs(u, carry):
        attend(set_a, set_b, 2 * u)
        attend(set_b, set_a, 2 * u + 1)
        return carry

    lax.fori_loop(0, n_pairs // 2, two_pairs, 0)

    @pl.when(n_pairs % 2 == 1)
    def _():
        attend(set_a, None, n_pairs - 1)

    outs = []
    for h in range(MOBA_HEADS):
        a = acc_s[h]
        outs.append(a[:MOBA_HEAD_DIM] / a[MOBA_HEAD_DIM:MOBA_HEAD_DIM + 1])
    y_moba = jnp.concatenate(outs, axis=0).T
    merged = merged + gate1 * _dot(_bf16(y_moba), wbm_ref[...])

    out_ref[0] = x + _dot(_bf16(merged), wo_ref[...])


def _resident(shape):
    return pl.BlockSpec(shape, lambda b, i: (0,) * len(shape), pipeline_mode=pl.Buffered(1))


def _mixer(x, memk, memv, g_mix, w_in, b_gate, conv_w, conv_b, moba_q_gain, moba_k_gain, memq_gain,
           w_br_conv, w_br_moba, w_br_mem, w_o):
    bsz, s, d = x.shape
    blk = MOBA_BLOCK
    assert s % blk == 0 and d == D_MODEL
    nb = s // blk
    nbr = pl.cdiv(nb, BF16_ROWS) * BF16_ROWS
    assert nbr < LANES
    w_main = _bf16(w_in)
    w_qv = lax.optimization_barrier(jnp.concatenate([w_in[:, W_Q:W_K], w_in[:, W_V:W_QMEM]], axis=1))
    w_qvt = _bf16(w_qv.T)
    qgt = jnp.broadcast_to(
        (jnp.tile(moba_q_gain, MOBA_HEADS) * (MOBA_HEAD_DIM ** -0.5 * math.log2(math.e)))[:, None], (MOBA_WIDTH, blk))
    kg = jnp.tile(moba_k_gain, MOBA_HEADS).reshape(1, MOBA_WIDTH)
    mqg = memq_gain.reshape(1, MEM_HEAD_DIM) * (MEM_HEAD_DIM ** -0.5)
    head_of = jnp.arange(MOBA_WIDTH) // MOBA_HEAD_DIM
    gsum = (head_of[:, None] == head_of[None, :]).astype(jnp.bfloat16)
    return pl.pallas_call(
        _mixer_kernel,
        grid=(bsz, nb),
        in_specs=[
            pl.BlockSpec((1, blk, d), lambda b, i: (b, i, 0)),
            _resident((1, d)),
            _resident((d, IN_COLS)),
            _resident((2 * MOBA_WIDTH, d)),
            _resident((1, 3 * d)),
            _resident((3, CONV_CH)),
            _resident((1, CONV_CH)),
            _resident((MOBA_WIDTH, blk)),
            _resident((1, MOBA_WIDTH)),
            _resident((MOBA_WIDTH, MOBA_WIDTH)),
            pl.BlockSpec((1, MEM_LEN, MEM_WIDTH), lambda b, i: (b, 0, 0)),
            pl.BlockSpec((1, MEM_LEN, MEM_WIDTH), lambda b, i: (b, 0, 0)),
            _resident((1, MEM_HEAD_DIM)),
            _resident((CONV_CH, d)),
            _resident((MOBA_WIDTH, d)),
            _resident((MEM_WIDTH, d)),
            _resident((d, d)),
        ],
        out_specs=pl.BlockSpec((1, blk, d), lambda b, i: (b, i, 0)),
        out_shape=jax.ShapeDtypeStruct((bsz, s, d), jnp.float32),
        scratch_shapes=[
            pltpu.VMEM((nb, blk, MOBA_WIDTH), jnp.bfloat16),
            pltpu.VMEM((nb, MOBA_HEADS, VT_ROWS, blk), jnp.bfloat16),
            pltpu.VMEM((nbr, MOBA_WIDTH), jnp.float32),
            pltpu.VMEM((SUBLANES + blk, CONV_CH), jnp.float32),
            pltpu.VMEM((MOBA_HEADS, 2 * LANES, blk), jnp.bfloat16),
            pltpu.VMEM((MOBA_HEADS, 2 * blk, blk), jnp.float32),
            pltpu.VMEM((MOBA_HEADS, 2 * blk, blk), jnp.float32),
            pltpu.VMEM((MOBA_HEADS, SUBLANES, blk), jnp.float32),
            pltpu.VMEM((MOBA_HEADS, SUBLANES, blk), jnp.float32),
            pltpu.VMEM((MOBA_HEADS, VT_ROWS, blk), jnp.float32),
            pltpu.VMEM((MOBA_HEADS, SUBLANES, blk), jnp.float32),
        ],
        compiler_params=pltpu.CompilerParams(
            dimension_semantics=("arbitrary", "arbitrary"),
            vmem_limit_bytes=VMEM_LIMIT_MIXER),
        name="mixer",
    )(x, g_mix.reshape(1, d), w_main, w_qvt, b_gate.reshape(1, 3 * d), conv_w, conv_b.reshape(1, CONV_CH),
      qgt, kg, gsum, memk, memv, mqg, _bf16(w_br_conv), _bf16(w_br_moba), _bf16(w_br_mem), _bf16(w_o))


def _ffn_kernel(x_ref, g_ref, wup_ref, cw_ref, cb_ref, wdown_ref, out_ref, aext_s, act_s):
    t = pl.program_id(1)
    tm = FFN_TILE
    x = x_ref[0]
    hb = _bf16(_rmsnorm(x, g_ref[...]))

    @pl.when(t == 0)
    def _():
        aext_s[0:SUBLANES, :] = jnp.zeros((SUBLANES, D_FF), jnp.float32)

    for c0, cw in FFN_CHUNKS:
        cs = slice(c0, c0 + cw)
        a = _dot(hb, wup_ref[:, c0:c0 + cw])
        b = _dot(hb, wup_ref[:, D_FF + c0:D_FF + c0 + cw])
        aext_s[SUBLANES:SUBLANES + tm, cs] = a
        a1 = aext_s[SUBLANES - 1:SUBLANES - 1 + tm, cs]
        a2 = aext_s[SUBLANES - 2:SUBLANES - 2 + tm, cs]
        aext_s[0:SUBLANES, cs] = a[tm - SUBLANES:, :]
        ac = a2 * cw_ref[0:1, cs] + a1 * cw_ref[1:2, cs] + a * cw_ref[2:3, cs] + cb_ref[:, cs]
        act_s[:, cs] = _bf16(ac * jax.nn.sigmoid(ac) * b)
    out_ref[0] = x + _dot(act_s[...], wdown_ref[...])


def _ffn(x, g_ffn, w_up, ffn_conv_w, ffn_conv_b, w_down):
    bsz, s, d = x.shape
    tm = FFN_TILE
    assert s % tm == 0
    return pl.pallas_call(
        _ffn_kernel,
        grid=(bsz, s // tm),
        in_specs=[
            pl.BlockSpec((1, tm, d), lambda b, t: (b, t, 0)),
            _resident((1, d)),
            _resident((d, 2 * D_FF)),
            _resident((3, D_FF)),
            _resident((1, D_FF)),
            _resident((D_FF, d)),
        ],
        out_specs=pl.BlockSpec((1, tm, d), lambda b, t: (b, t, 0)),
        out_shape=jax.ShapeDtypeStruct((bsz, s, d), jnp.float32),
        scratch_shapes=[
            pltpu.VMEM((SUBLANES + tm, D_FF), jnp.float32),
            pltpu.VMEM((tm, D_FF), jnp.bfloat16),
        ],
        compiler_params=pltpu.CompilerParams(
            dimension_semantics=("arbitrary", "arbitrary"),
            vmem_limit_bytes=VMEM_LIMIT_FFN),
        name="ffn",
    )(x, g_ffn.reshape(1, d), _bf16(w_up), ffn_conv_w, ffn_conv_b.reshape(1, D_FF), _bf16(w_down))


def kernel(x, mem, g_mix, w_in, b_gate, conv_w, conv_b, moba_q_gain, moba_k_gain, g_mem, w_mem_kv, memq_gain,
           memk_gain, w_br_conv, w_br_moba, w_br_mem, w_o, g_ffn, w_up, ffn_conv_w, ffn_conv_b, w_down):
    memk, memv = _memkv(mem, g_mem, w_mem_kv, memk_gain)
    x = _mixer(x, memk, memv, g_mix, w_in, b_gate, conv_w, conv_b, moba_q_gain, moba_k_gain, memq_gain,
               w_br_conv, w_br_moba, w_br_mem, w_o)
    return _ffn(x, g_ffn, w_up, ffn_conv_w, ffn_conv_b, w_down)
```

```python
import math

import jax
import jax.numpy as jnp
from jax import lax
from jax.experimental import pallas as pl
from jax.experimental.pallas import tpu as pltpu

D_MODEL = 1024
MEM_LEN = 256
CONV_CH = 512
MOBA_HEADS = 8
MOBA_HEAD_DIM = 64
MOBA_WIDTH = MOBA_HEADS * MOBA_HEAD_DIM
MOBA_BLOCK = 256
MOBA_TOPK = 3
MEM_HEADS = 4
MEM_HEAD_DIM = 128
MEM_WIDTH = MEM_HEADS * MEM_HEAD_DIM
D_FF = 2816
EPS = 1e-6

W_CONV = 0
W_Q = 3 * CONV_CH
W_K = W_Q + MOBA_WIDTH
W_V = W_K + MOBA_WIDTH
W_QMEM = W_V + MOBA_WIDTH
W_GATE = W_QMEM + MEM_WIDTH
IN_COLS = W_GATE + 3 * D_MODEL

LANES = 128
SUBLANES = 8
BF16_ROWS = 16
PAIR = 2 * MOBA_HEAD_DIM
VT_ROWS = MOBA_HEAD_DIM + BF16_ROWS
MASKED = -1e30
VMEM_LIMIT_MIXER = 56 * 1024 * 1024
VMEM_LIMIT_FFN = 48 * 1024 * 1024
FFN_TILE = 512
FFN_CHUNKS = ((0, 1024), (1024, 1024), (2048, 768))

_NT = (((1,), (1,)), ((), ()))


def _dot(a, b):
    return jnp.dot(a, b, preferred_element_type=jnp.float32)


def _dot_nt(a, b):
    return lax.dot_general(a, b, _NT, preferred_element_type=jnp.float32)


def _rmsnorm(x, g):
    return x * lax.rsqrt(jnp.mean(x * x, axis=-1, keepdims=True) + EPS) * g


def _bf16(x):
    return x.astype(jnp.bfloat16)


def _memkv_kernel(mem_ref, g_ref, w_ref, kg_ref, kt_ref, v_ref):
    mn = _rmsnorm(mem_ref[0], g_ref[...])
    kv = _dot(_bf16(mn), w_ref[...])
    for h in range(MEM_HEADS):
        lo = h * MEM_HEAD_DIM
        kh = _rmsnorm(kv[:, lo:lo + MEM_HEAD_DIM], kg_ref[...])
        kt_ref[0, lo:lo + MEM_HEAD_DIM, :] = _bf16(kh.T)
    v_ref[0] = _bf16(kv[:, MEM_WIDTH:])


def _memkv(mem, g_mem, w_mem_kv, memk_gain):
    bsz, m, d = mem.shape
    const = lambda b: (0, 0)
    return pl.pallas_call(
        _memkv_kernel,
        grid=(bsz,),
        in_specs=[
            pl.BlockSpec((1, m, d), lambda b: (b, 0, 0)),
            pl.BlockSpec((1, d), const),
            pl.BlockSpec((d, 2 * MEM_WIDTH), const),
            pl.BlockSpec((1, MEM_HEAD_DIM), const),
        ],
        out_specs=[
            pl.BlockSpec((1, MEM_WIDTH, m), lambda b: (b, 0, 0)),
            pl.BlockSpec((1, m, MEM_WIDTH), lambda b: (b, 0, 0)),
        ],
        out_shape=[
            jax.ShapeDtypeStruct((bsz, MEM_WIDTH, m), jnp.bfloat16),
            jax.ShapeDtypeStruct((bsz, m, MEM_WIDTH), jnp.bfloat16),
        ],
        name="memkv",
    )(mem, g_mem.reshape(1, d), _bf16(w_mem_kv), memk_gain.reshape(1, MEM_HEAD_DIM))


def _mixer_kernel(x_ref, gmix_ref, wmain_ref, wqvt_ref, bgate_ref, convw_ref, convb_ref, qgt_ref, kg_ref, gsum_ref,
                  memkt_ref, memv_ref, mqg_ref, wbc_ref, wbm_ref, wbx_ref, wo_ref,
                  out_ref,
                  k_s, vt_s, kbar_s, uext_s, rhs_s, sa_s, sb_s, ma_s, mb_s, acc_s, m_s):
    i = pl.program_id(1)
    blk = MOBA_BLOCK
    nbr = kbar_s.shape[0]

    @pl.when(i == 0)
    def _():
        uext_s[0:SUBLANES, :] = jnp.zeros((SUBLANES, CONV_CH), jnp.float32)
        kbar_s[...] = jnp.zeros(kbar_s.shape, jnp.float32)

    x = x_ref[0]
    hb = _bf16(_rmsnorm(x, gmix_ref[...]))

    def proj(lo, width):
        return _dot(hb, wmain_ref[:, lo:lo + width])

    def gate(g, n):
        return jax.nn.sigmoid(g + bgate_ref[:, n * D_MODEL:(n + 1) * D_MODEL])

    pc = proj(W_CONV, 3 * CONV_CH)
    k_raw = proj(W_K, MOBA_WIDTH)
    qvt = _dot_nt(wqvt_ref[...], hb)
    qm = proj(W_QMEM, MEM_WIDTH)

    u = pc[:, CONV_CH:2 * CONV_CH] * pc[:, 2 * CONV_CH:]
    uext_s[SUBLANES:SUBLANES + blk, :] = u
    u1 = uext_s[SUBLANES - 1:SUBLANES - 1 + blk, :]
    u2 = uext_s[SUBLANES - 2:SUBLANES - 2 + blk, :]
    uext_s[0:SUBLANES, :] = u[blk - SUBLANES:, :]
    conv = u2 * convw_ref[0:1, :] + u1 * convw_ref[1:2, :] + u * convw_ref[2:3, :] + convb_ref[...]
    y_conv = _bf16(pc[:, :CONV_CH] * conv)

    k2_b = _bf16(k_raw * k_raw)

    q3 = qvt[:MOBA_WIDTH].reshape(MOBA_HEADS, MOBA_HEAD_DIM, blk)
    ssq = jnp.sum(q3 * q3, axis=1, keepdims=True)
    qbt = _bf16((q3 * lax.rsqrt(ssq * (1.0 / MOBA_HEAD_DIM) + EPS)).reshape(MOBA_WIDTH, blk) * qgt_ref[...])
    vbt = _bf16(qvt[MOBA_WIDTH:])

    qmb = [_bf16(_rmsnorm(qm[:, h * MEM_HEAD_DIM:(h + 1) * MEM_HEAD_DIM], mqg_ref[...])) for h in range(MEM_HEADS)]

    g0 = proj(W_GATE, D_MODEL)
    kss = _dot(k2_b, gsum_ref[...])
    p_conv = _dot(y_conv, wbc_ref[...])
    s_mem = [_dot(qmb[h], memkt_ref[0, h * MEM_HEAD_DIM:(h + 1) * MEM_HEAD_DIM, :]) for h in range(MEM_HEADS)]
    g2 = proj(W_GATE + 2 * D_MODEL, D_MODEL)

    merged = gate(g0, 0) * p_conv
    kn = k_raw * lax.rsqrt(kss * (1.0 / MOBA_HEAD_DIM) + EPS) * kg_ref[...]
    kb = _bf16(kn)
    k_s[i] = kb
    kbar_row = jnp.mean(kn, axis=0, keepdims=True)
    kb_rows = lax.broadcasted_iota(jnp.int32, kbar_s.shape, 0)
    kbar_prev = kbar_s[...]
    kbar_s[...] = jnp.where(kb_rows == i, kbar_row, kbar_prev)
    kbar_b = _bf16(kbar_prev)

    zeros_h = jnp.zeros((MOBA_HEAD_DIM, blk), jnp.bfloat16)
    qpairs = []
    for h in range(MOBA_HEADS):
        qh = qbt[h * MOBA_HEAD_DIM:(h + 1) * MOBA_HEAD_DIM]
        qpairs.append(jnp.concatenate([qh, zeros_h] if h % 2 == 0 else [zeros_h, qh], axis=0))

    p_mem = [jnp.exp(s - jnp.max(s, axis=-1, keepdims=True)) for s in s_mem]

    pair_cols = lambda h: slice((h // 2) * PAIR, (h // 2 + 1) * PAIR)
    gates = [_dot(kbar_b[:, pair_cols(h)], qpairs[h]) for h in range(MOBA_HEADS)]
    s_own = [_dot(kb[:, pair_cols(h)], qpairs[h]) for h in range(MOBA_HEADS)]
    o_mem = [_dot(_bf16(p_mem[h]), memv_ref[0, :, h * MEM_HEAD_DIM:(h + 1) * MEM_HEAD_DIM])
             / jnp.sum(p_mem[h], axis=-1, keepdims=True) for h in range(MEM_HEADS)]
    g1 = proj(W_GATE + D_MODEL, D_MODEL)
    merged = merged + gate(g2, 2) * _dot(_bf16(jnp.concatenate(o_mem, axis=-1)), wbx_ref[...])
    gate1 = gate(g1, 1)

    lane = lax.broadcasted_iota(jnp.int32, (blk, LANES), 1)
    brow = lax.broadcasted_iota(jnp.int32, (nbr, blk), 0)
    browf = brow.astype(jnp.float32)
    neg_inf = jnp.float32(-jnp.inf)
    sel_pad = jnp.full((LANES - nbr, blk), MASKED, jnp.bfloat16)
    ones_r = jnp.ones((BF16_ROWS, blk), jnp.bfloat16)
    vts = []
    for h in range(MOBA_HEADS):
        g = jnp.where(brow < i, gates[h], neg_inf)
        sel = jnp.full((nbr, blk), MASKED, jnp.float32)
        for _ in range(MOBA_TOPK):
            mx = jnp.max(g, axis=0, keepdims=True)
            idx = jnp.min(jnp.where(g == mx, browf, float(nbr)), axis=0, keepdims=True)
            pick = browf == idx
            sel = jnp.where(pick & (mx > neg_inf), 0.0, sel)
            g = jnp.where(pick, neg_inf, g)
        rhs_s[h] = jnp.concatenate([qpairs[h], _bf16(sel), sel_pad], axis=0)
        vt = jnp.concatenate([vbt[h * MOBA_HEAD_DIM:(h + 1) * MOBA_HEAD_DIM], ones_r], axis=0)
        vt_s[i, h] = vt
        vts.append(vt)

    def pair_scores(dst, h, tp):
        dst_s, dst_mx = dst
        parts = []
        for jn in (2 * tp, 2 * tp + 1):
            en = jnp.where(lane == jnp.where(jn < i, jn, LANES - 1), 1.0, 0.0).astype(jnp.bfloat16)
            parts.append(jnp.concatenate([k_s[jnp.minimum(jn, i), :, pair_cols(h)], en], axis=1))
        s = _dot(jnp.concatenate(parts, axis=0), rhs_s[h])
        dst_s[h] = s
        dst_mx[h] = jnp.broadcast_to(jnp.max(s, axis=0, keepdims=True), (SUBLANES, blk))

    causal = (lax.broadcasted_iota(jnp.int32, (blk, blk), 0) <= lax.broadcasted_iota(jnp.int32, (blk, blk), 1))
    set_a, set_b = (sa_s, ma_s), (sb_s, mb_s)
    for h in range(MOBA_HEADS):
        pair_scores(set_a, h, 0)
        s = jnp.where(causal, s_own[h], MASKED)
        m0 = jnp.max(s, axis=0, keepdims=True)
        acc_s[h] = _dot(vts[h], jnp.exp2(_bf16(s - m0)))
        m_s[h] = jnp.broadcast_to(m0, (SUBLANES, blk))

    def attend(src, dst, tp):
        src_s, src_mx = src
        ja = jnp.minimum(2 * tp, i)
        jb = jnp.minimum(2 * tp + 1, i)
        for h in range(MOBA_HEADS):
            if dst is not None:
                pair_scores(dst, h, tp + 1)
            m_old = m_s[h]
            m_new = jnp.maximum(m_old, src_mx[h])
            p = jnp.exp2(_bf16(src_s[h] - m_new[0:1]))
            pv = _dot(jnp.concatenate([vt_s[ja, h], vt_s[jb, h]], axis=1), p)
            acc_s[h] = acc_s[h] * jnp.exp2(m_old[0:1] - m_new[0:1]) + pv
            m_s[h] = m_new

    n_pairs = (i + 1) // 2

    def two_pairs(u, carry):
        attend(set_a, set_b, 2 * u)
        attend(set_b, set_a, 2 * u + 1)
        return carry

    lax.fori_loop(0, n_pairs // 2, two_pairs, 0)

    @pl.when(n_pairs % 2 == 1)
    def _():
        attend(set_a, None, n_pairs - 1)

    outs = []
    for h in range(MOBA_HEADS):
        a = acc_s[h]
        outs.append(a[:MOBA_HEAD_DIM] / a[MOBA_HEAD_DIM:MOBA_HEAD_DIM + 1])
    y_moba = jnp.concatenate(outs, axis=0).T
    merged = merged + gate1 * _dot(_bf16(y_moba), wbm_ref[...])

    out_ref[0] = x + _dot(_bf16(merged), wo_ref[...])


def _resident(shape):
    return pl.BlockSpec(shape, lambda b, i: (0,) * len(shape), pipeline_mode=pl.Buffered(1))


def _mixer(x, memk, memv, g_mix, w_in, b_gate, conv_w, conv_b, moba_q_gain, moba_k_gain, memq_gain,
           w_br_conv, w_br_moba, w_br_mem, w_o):
    bsz, s, d = x.shape
    blk = MOBA_BLOCK
    assert s % blk == 0 and d == D_MODEL
    nb = s // blk
    nbr = pl.cdiv(nb, BF16_ROWS) * BF16_ROWS
    assert nbr < LANES
    w_main = _bf16(w_in)
    w_qv = lax.optimization_barrier(jnp.concatenate([w_in[:, W_Q:W_K], w_in[:, W_V:W_QMEM]], axis=1))
    w_qvt = _bf16(w_qv.T)
    qgt = jnp.broadcast_to(
        (jnp.tile(moba_q_gain, MOBA_HEADS) * (MOBA_HEAD_DIM ** -0.5 * math.log2(math.e)))[:, None], (MOBA_WIDTH, blk))
    kg = jnp.tile(moba_k_gain, MOBA_HEADS).reshape(1, MOBA_WIDTH)
    mqg = memq_gain.reshape(1, MEM_HEAD_DIM) * (MEM_HEAD_DIM ** -0.5)
    head_of = jnp.arange(MOBA_WIDTH) // MOBA_HEAD_DIM
    gsum = (head_of[:, None] == head_of[None, :]).astype(jnp.bfloat16)
    return pl.pallas_call(
        _mixer_kernel,
        grid=(bsz, nb),
        in_specs=[
            pl.BlockSpec((1, blk, d), lambda b, i: (b, i, 0)),
            _resident((1, d)),
            _resident((d, IN_COLS)),
            _resident((2 * MOBA_WIDTH, d)),
            _resident((1, 3 * d)),
            _resident((3, CONV_CH)),
            _resident((1, CONV_CH)),
            _resident((MOBA_WIDTH, blk)),
            _resident((1, MOBA_WIDTH)),
            _resident((MOBA_WIDTH, MOBA_WIDTH)),
            pl.BlockSpec((1, MEM_WIDTH, MEM_LEN), lambda b, i: (b, 0, 0)),
            pl.BlockSpec((1, MEM_LEN, MEM_WIDTH), lambda b, i: (b, 0, 0)),
            _resident((1, MEM_HEAD_DIM)),
            _resident((CONV_CH, d)),
            _resident((MOBA_WIDTH, d)),
            _resident((MEM_WIDTH, d)),
            _resident((d, d)),
        ],
        out_specs=pl.BlockSpec((1, blk, d), lambda b, i: (b, i, 0)),
        out_shape=jax.ShapeDtypeStruct((bsz, s, d), jnp.float32),
        scratch_shapes=[
            pltpu.VMEM((nb, blk, MOBA_WIDTH), jnp.bfloat16),
            pltpu.VMEM((nb, MOBA_HEADS, VT_ROWS, blk), jnp.bfloat16),
            pltpu.VMEM((nbr, MOBA_WIDTH), jnp.float32),
            pltpu.VMEM((SUBLANES + blk, CONV_CH), jnp.float32),
            pltpu.VMEM((MOBA_HEADS, 2 * LANES, blk), jnp.bfloat16),
            pltpu.VMEM((MOBA_HEADS, 2 * blk, blk), jnp.float32),
            pltpu.VMEM((MOBA_HEADS, 2 * blk, blk), jnp.float32),
            pltpu.VMEM((MOBA_HEADS, SUBLANES, blk), jnp.float32),
            pltpu.VMEM((MOBA_HEADS, SUBLANES, blk), jnp.float32),
            pltpu.VMEM((MOBA_HEADS, VT_ROWS, blk), jnp.float32),
            pltpu.VMEM((MOBA_HEADS, SUBLANES, blk), jnp.float32),
        ],
        compiler_params=pltpu.CompilerParams(
            dimension_semantics=("arbitrary", "arbitrary"),
            vmem_limit_bytes=VMEM_LIMIT_MIXER),
        name="mixer",
    )(x, g_mix.reshape(1, d), w_main, w_qvt, b_gate.reshape(1, 3 * d), conv_w, conv_b.reshape(1, CONV_CH),
      qgt, kg, gsum, memk, memv, mqg, _bf16(w_br_conv), _bf16(w_br_moba), _bf16(w_br_mem), _bf16(w_o))


def _ffn_kernel(x_ref, g_ref, wup_ref, cw_ref, cb_ref, wdown_ref, out_ref, aext_s, act_s):
    t = pl.program_id(1)
    tm = FFN_TILE
    x = x_ref[0]
    hb = _bf16(_rmsnorm(x, g_ref[...]))

    @pl.when(t == 0)
    def _():
        aext_s[0:SUBLANES, :] = jnp.zeros((SUBLANES, D_FF), jnp.float32)

    for c0, cw in FFN_CHUNKS:
        cs = slice(c0, c0 + cw)
        a = _dot(hb, wup_ref[:, c0:c0 + cw])
        b = _dot(hb, wup_ref[:, D_FF + c0:D_FF + c0 + cw])
        aext_s[SUBLANES:SUBLANES + tm, cs] = a
        a1 = aext_s[SUBLANES - 1:SUBLANES - 1 + tm, cs]
        a2 = aext_s[SUBLANES - 2:SUBLANES - 2 + tm, cs]
        aext_s[0:SUBLANES, cs] = a[tm - SUBLANES:, :]
        ac = a2 * cw_ref[0:1, cs] + a1 * cw_ref[1:2, cs] + a * cw_ref[2:3, cs] + cb_ref[:, cs]
        act_s[:, cs] = _bf16(ac * jax.nn.sigmoid(ac) * b)
    out_ref[0] = x + _dot(act_s[...], wdown_ref[...])


def _ffn(x, g_ffn, w_up, ffn_conv_w, ffn_conv_b, w_down):
    bsz, s, d = x.shape
    tm = FFN_TILE
    assert s % tm == 0
    return pl.pallas_call(
        _ffn_kernel,
        grid=(bsz, s // tm),
        in_specs=[
            pl.BlockSpec((1, tm, d), lambda b, t: (b, t, 0)),
            _resident((1, d)),
            _resident((d, 2 * D_FF)),
            _resident((3, D_FF)),
            _resident((1, D_FF)),
            _resident((D_FF, d)),
        ],
        out_specs=pl.BlockSpec((1, tm, d), lambda b, t: (b, t, 0)),
        out_shape=jax.ShapeDtypeStruct((bsz, s, d), jnp.float32),
        scratch_shapes=[
            pltpu.VMEM((SUBLANES + tm, D_FF), jnp.float32),
            pltpu.VMEM((tm, D_FF), jnp.bfloat16),
        ],
        compiler_params=pltpu.CompilerParams(
            dimension_semantics=("arbitrary", "arbitrary"),
            vmem_limit_bytes=VMEM_LIMIT_FFN),
        name="ffn",
    )(x, g_ffn.reshape(1, d), _bf16(w_up), ffn_conv_w, ffn_conv_b.reshape(1, D_FF), _bf16(w_down))


def kernel(x, mem, g_mix, w_in, b_gate, conv_w, conv_b, moba_q_gain, moba_k_gain, g_mem, w_mem_kv, memq_gain,
           memk_gain, w_br_conv, w_br_moba, w_br_mem, w_o, g_ffn, w_up, ffn_conv_w, ffn_conv_b, w_down):
    memk, memv = _memkv(mem, g_mem, w_mem_kv, memk_gain)
    x = _mixer(x, memk, memv, g_mix, w_in, b_gate, conv_w, conv_b, moba_q_gain, moba_k_gain, memq_gain,
               w_br_conv, w_br_moba, w_br_mem, w_o)
    return _ffn(x, g_ffn, w_up, ffn_conv_w, ffn_conv_b, w_down)
```

```python
import math

import jax
import jax.numpy as jnp
from jax import lax
from jax.experimental import pallas as pl
from jax.experimental.pallas import tpu as pltpu

D_MODEL = 1024
MEM_LEN = 256
CONV_CH = 512
MOBA_HEADS = 8
MOBA_HEAD_DIM = 64
MOBA_WIDTH = MOBA_HEADS * MOBA_HEAD_DIM
MOBA_BLOCK = 256
MOBA_TOPK = 3
MEM_HEADS = 4
MEM_HEAD_DIM = 128
MEM_WIDTH = MEM_HEADS * MEM_HEAD_DIM
D_FF = 2816
EPS = 1e-6

W_CONV = 0
W_Q = 3 * CONV_CH
W_K = W_Q + MOBA_WIDTH
W_V = W_K + MOBA_WIDTH
W_QMEM = W_V + MOBA_WIDTH
W_GATE = W_QMEM + MEM_WIDTH
IN_COLS = W_GATE + 3 * D_MODEL

LANES = 128
SUBLANES = 8
BF16_ROWS = 16
PAIR = 2 * MOBA_HEAD_DIM
VT_ROWS = MOBA_HEAD_DIM + BF16_ROWS
MASKED = -1e30
TOKEN_TILE = 512
TILE_BLOCKS = TOKEN_TILE // MOBA_BLOCK
VMEM_LIMIT_PROJ = 56 * 1024 * 1024
VMEM_LIMIT_ATTN = 40 * 1024 * 1024
VMEM_LIMIT_FFN = 60 * 1024 * 1024
FFN_CHUNKS = ((0, 1024), (1024, 1024), (2048, 768))

_NT = (((1,), (1,)), ((), ()))


def _dot(a, b):
    return jnp.dot(a, b, preferred_element_type=jnp.float32)


def _dot_nt(a, b):
    return lax.dot_general(a, b, _NT, preferred_element_type=jnp.float32)


def _rmsnorm(x, g):
    return x * lax.rsqrt(jnp.mean(x * x, axis=-1, keepdims=True) + EPS) * g


def _bf16(x):
    return x.astype(jnp.bfloat16)


def _resident(shape):
    return pl.BlockSpec(shape, lambda b, i: (0,) * len(shape), pipeline_mode=pl.Buffered(1))


def _params(vmem_limit):
    return pltpu.CompilerParams(dimension_semantics=("arbitrary", "arbitrary"), vmem_limit_bytes=vmem_limit)


def _memkv_kernel(mem_ref, g_ref, w_ref, kg_ref, kt_ref, v_ref):
    mn = _rmsnorm(mem_ref[0], g_ref[...])
    kv = _dot(_bf16(mn), w_ref[...])
    for h in range(MEM_HEADS):
        lo = h * MEM_HEAD_DIM
        kh = _rmsnorm(kv[:, lo:lo + MEM_HEAD_DIM], kg_ref[...])
        kt_ref[0, lo:lo + MEM_HEAD_DIM, :] = _bf16(kh.T)
    v_ref[0] = _bf16(kv[:, MEM_WIDTH:])


def _memkv(mem, g_mem, w_mem_kv, memk_gain):
    bsz, m, d = mem.shape
    const = lambda b: (0, 0)
    return pl.pallas_call(
        _memkv_kernel,
        grid=(bsz,),
        in_specs=[
            pl.BlockSpec((1, m, d), lambda b: (b, 0, 0)),
            pl.BlockSpec((1, d), const),
            pl.BlockSpec((d, 2 * MEM_WIDTH), const),
            pl.BlockSpec((1, MEM_HEAD_DIM), const),
        ],
        out_specs=[
            pl.BlockSpec((1, MEM_WIDTH, m), lambda b: (b, 0, 0)),
            pl.BlockSpec((1, m, MEM_WIDTH), lambda b: (b, 0, 0)),
        ],
        out_shape=[
            jax.ShapeDtypeStruct((bsz, MEM_WIDTH, m), jnp.bfloat16),
            jax.ShapeDtypeStruct((bsz, m, MEM_WIDTH), jnp.bfloat16),
        ],
        name="memkv",
    )(mem, g_mem.reshape(1, d), _bf16(w_mem_kv), memk_gain.reshape(1, MEM_HEAD_DIM))


def _proj_kernel(x_ref, gmix_ref, win_ref, wqvt_ref, bgate_ref, convw_ref, convb_ref, qgt_ref, kg_ref, gsum_ref,
                 memkt_ref, memv_ref, mqg_ref, wbc_ref, wbx_ref,
                 partial_ref, gate1_ref, k_ref, qt_ref, vt_ref, kbar_ref,
                 uext_s):
    tm = TOKEN_TILE
    blk = MOBA_BLOCK

    @pl.when(pl.program_id(1) == 0)
    def _():
        uext_s[0:SUBLANES, :] = jnp.zeros((SUBLANES, CONV_CH), jnp.float32)

    hb = _bf16(_rmsnorm(x_ref[0], gmix_ref[...]))

    def proj(lo, width):
        return _dot(hb, win_ref[:, lo:lo + width])

    def gate(g, n):
        return jax.nn.sigmoid(g + bgate_ref[:, n * D_MODEL:(n + 1) * D_MODEL])

    pc = proj(W_CONV, 3 * CONV_CH)
    k_raw = proj(W_K, MOBA_WIDTH)
    qvt = _dot_nt(wqvt_ref[...], hb)
    qm = proj(W_QMEM, MEM_WIDTH)

    u = pc[:, CONV_CH:2 * CONV_CH] * pc[:, 2 * CONV_CH:]
    uext_s[SUBLANES:SUBLANES + tm, :] = u
    u1 = uext_s[SUBLANES - 1:SUBLANES - 1 + tm, :]
    u2 = uext_s[SUBLANES - 2:SUBLANES - 2 + tm, :]
    uext_s[0:SUBLANES, :] = u[tm - SUBLANES:, :]
    conv = u2 * convw_ref[0:1, :] + u1 * convw_ref[1:2, :] + u * convw_ref[2:3, :] + convb_ref[...]
    y_conv = _bf16(pc[:, :CONV_CH] * conv)

    k2_b = _bf16(k_raw * k_raw)

    q3 = qvt[:MOBA_WIDTH].reshape(MOBA_HEADS, MOBA_HEAD_DIM, tm)
    ssq = jnp.sum(q3 * q3, axis=1, keepdims=True)
    qbt = _bf16((q3 * lax.rsqrt(ssq * (1.0 / MOBA_HEAD_DIM) + EPS)).reshape(MOBA_WIDTH, tm) * qgt_ref[...])
    vbt = _bf16(qvt[MOBA_WIDTH:])
    for sub in range(TILE_BLOCKS):
        qt_ref[0, sub] = qbt[:, sub * blk:(sub + 1) * blk]
        vt_ref[0, sub] = vbt[:, sub * blk:(sub + 1) * blk]

    qmb = [_bf16(_rmsnorm(qm[:, h * MEM_HEAD_DIM:(h + 1) * MEM_HEAD_DIM], mqg_ref[...])) for h in range(MEM_HEADS)]

    g0 = proj(W_GATE, D_MODEL)
    kss = _dot(k2_b, gsum_ref[...])
    p_conv = _dot(y_conv, wbc_ref[...])
    s_mem = [_dot(qmb[h], memkt_ref[0, h * MEM_HEAD_DIM:(h + 1) * MEM_HEAD_DIM, :]) for h in range(MEM_HEADS)]
    g2 = proj(W_GATE + 2 * D_MODEL, D_MODEL)

    merged = gate(g0, 0) * p_conv
    kn = k_raw * lax.rsqrt(kss * (1.0 / MOBA_HEAD_DIM) + EPS) * kg_ref[...]
    k_ref[0] = _bf16(kn)
    for sub in range(TILE_BLOCKS):
        kbar = jnp.mean(kn[sub * blk:(sub + 1) * blk], axis=0, keepdims=True)
        kbar_ref[0, sub] = jnp.broadcast_to(kbar, (SUBLANES, MOBA_WIDTH))
    p_mem = [jnp.exp(s - jnp.max(s, axis=-1, keepdims=True)) for s in s_mem]

    o_mem = [_dot(_bf16(p_mem[h]), memv_ref[0, :, h * MEM_HEAD_DIM:(h + 1) * MEM_HEAD_DIM])
             / jnp.sum(p_mem[h], axis=-1, keepdims=True) for h in range(MEM_HEADS)]
    g1 = proj(W_GATE + D_MODEL, D_MODEL)
    partial_ref[0] = merged + gate(g2, 2) * _dot(_bf16(jnp.concatenate(o_mem, axis=-1)), wbx_ref[...])
    gate1_ref[0] = _bf16(gate(g1, 1))


def _proj(x, memkt, memv, g_mix, w_in, b_gate, conv_w, conv_b, moba_q_gain, moba_k_gain, memq_gain,
          w_br_conv, w_br_mem):
    bsz, s, d = x.shape
    tm = TOKEN_TILE
    assert s % tm == 0 and d == D_MODEL
    nb = s // MOBA_BLOCK
    w_qv = lax.optimization_barrier(jnp.concatenate([w_in[:, W_Q:W_K], w_in[:, W_V:W_QMEM]], axis=1))
    w_qvt = _bf16(w_qv.T)
    qgt = jnp.broadcast_to(
        (jnp.tile(moba_q_gain, MOBA_HEADS) * (MOBA_HEAD_DIM ** -0.5 * math.log2(math.e)))[:, None], (MOBA_WIDTH, tm))
    kg = jnp.tile(moba_k_gain, MOBA_HEADS).reshape(1, MOBA_WIDTH)
    mqg = memq_gain.reshape(1, MEM_HEAD_DIM) * (MEM_HEAD_DIM ** -0.5)
    head_of = jnp.arange(MOBA_WIDTH) // MOBA_HEAD_DIM
    gsum = (head_of[:, None] == head_of[None, :]).astype(jnp.bfloat16)
    tile = lambda b, t: (b, t, 0)
    tile4 = lambda b, t: (b, t, 0, 0)
    return pl.pallas_call(
        _proj_kernel,
        grid=(bsz, s // tm),
        in_specs=[
            pl.BlockSpec((1, tm, d), tile),
            _resident((1, d)),
            _resident((d, IN_COLS)),
            _resident((2 * MOBA_WIDTH, d)),
            _resident((1, 3 * d)),
            _resident((3, CONV_CH)),
            _resident((1, CONV_CH)),
            _resident((MOBA_WIDTH, tm)),
            _resident((1, MOBA_WIDTH)),
            _resident((MOBA_WIDTH, MOBA_WIDTH)),
            pl.BlockSpec((1, MEM_WIDTH, MEM_LEN), lambda b, t: (b, 0, 0)),
            pl.BlockSpec((1, MEM_LEN, MEM_WIDTH), lambda b, t: (b, 0, 0)),
            _resident((1, MEM_HEAD_DIM)),
            _resident((CONV_CH, d)),
            _resident((MEM_WIDTH, d)),
        ],
        out_specs=[
            pl.BlockSpec((1, tm, d), tile),
            pl.BlockSpec((1, tm, d), tile),
            pl.BlockSpec((1, tm, MOBA_WIDTH), tile),
            pl.BlockSpec((1, TILE_BLOCKS, MOBA_WIDTH, MOBA_BLOCK), tile4),
            pl.BlockSpec((1, TILE_BLOCKS, MOBA_WIDTH, MOBA_BLOCK), tile4),
            pl.BlockSpec((1, TILE_BLOCKS, SUBLANES, MOBA_WIDTH), tile4),
        ],
        out_shape=[
            jax.ShapeDtypeStruct((bsz, s, d), jnp.float32),
            jax.ShapeDtypeStruct((bsz, s, d), jnp.bfloat16),
            jax.ShapeDtypeStruct((bsz, s, MOBA_WIDTH), jnp.bfloat16),
            jax.ShapeDtypeStruct((bsz, nb, MOBA_WIDTH, MOBA_BLOCK), jnp.bfloat16),
            jax.ShapeDtypeStruct((bsz, nb, MOBA_WIDTH, MOBA_BLOCK), jnp.bfloat16),
            jax.ShapeDtypeStruct((bsz, nb, SUBLANES, MOBA_WIDTH), jnp.float32),
        ],
        scratch_shapes=[
            pltpu.VMEM((SUBLANES + tm, CONV_CH), jnp.float32),
        ],
        compiler_params=_params(VMEM_LIMIT_PROJ),
        name="proj",
    )(x, g_mix.reshape(1, d), _bf16(w_in), w_qvt, b_gate.reshape(1, 3 * d), conv_w, conv_b.reshape(1, CONV_CH),
      qgt, kg, gsum, memkt, memv, mqg, _bf16(w_br_conv), _bf16(w_br_mem))


def _attn_kernel(qt_ref, k_ref, vt_ref, kbar_ref, y_ref,
                 k_s, vt_s, kbar_s, rhs_s, sa_s, sb_s, ma_s, mb_s, acc_s, m_s):
    i = pl.program_id(1)
    blk = MOBA_BLOCK
    nbr = kbar_s.shape[0]

    @pl.when(i == 0)
    def _():
        kbar_s[...] = jnp.zeros(kbar_s.shape, jnp.float32)

    kb = k_ref[0]
    k_s[i] = kb
    qbt = qt_ref[0, 0]
    vbt = vt_ref[0, 0]
    kb_rows = lax.broadcasted_iota(jnp.int32, kbar_s.shape, 0)
    kbar_prev = kbar_s[...]
    kbar_s[...] = jnp.where(kb_rows == i, kbar_ref[0, 0, 0:1, :], kbar_prev)
    kbar_b = _bf16(kbar_prev)

    zeros_h = jnp.zeros((MOBA_HEAD_DIM, blk), jnp.bfloat16)
    qpairs = []
    for h in range(MOBA_HEADS):
        qh = qbt[h * MOBA_HEAD_DIM:(h + 1) * MOBA_HEAD_DIM]
        qpairs.append(jnp.concatenate([qh, zeros_h] if h % 2 == 0 else [zeros_h, qh], axis=0))

    pair_cols = lambda h: slice((h // 2) * PAIR, (h // 2 + 1) * PAIR)
    gates = [_dot(kbar_b[:, pair_cols(h)], qpairs[h]) for h in range(MOBA_HEADS)]
    s_own = [_dot(kb[:, pair_cols(h)], qpairs[h]) for h in range(MOBA_HEADS)]

    lane = lax.broadcasted_iota(jnp.int32, (blk, LANES), 1)
    brow = lax.broadcasted_iota(jnp.int32, (nbr, blk), 0)
    browf = brow.astype(jnp.float32)
    neg_inf = jnp.float32(-jnp.inf)
    sel_pad = jnp.full((LANES - nbr, blk), MASKED, jnp.bfloat16)
    ones_r = jnp.ones((BF16_ROWS, blk), jnp.bfloat16)
    vts = []
    for h in range(MOBA_HEADS):
        g = jnp.where(brow < i, gates[h], neg_inf)
        sel = jnp.full((nbr, blk), MASKED, jnp.float32)
        for _ in range(MOBA_TOPK):
            mx = jnp.max(g, axis=0, keepdims=True)
            idx = jnp.min(jnp.where(g == mx, browf, float(nbr)), axis=0, keepdims=True)
            pick = browf == idx
            sel = jnp.where(pick & (mx > neg_inf), 0.0, sel)
            g = jnp.where(pick, neg_inf, g)
        rhs_s[h] = jnp.concatenate([qpairs[h], _bf16(sel), sel_pad], axis=0)
        vt = jnp.concatenate([vbt[h * MOBA_HEAD_DIM:(h + 1) * MOBA_HEAD_DIM], ones_r], axis=0)
        vt_s[i, h] = vt
        vts.append(vt)

    def pair_scores(dst, h, tp):
        dst_s, dst_mx = dst
        parts = []
        for jn in (2 * tp, 2 * tp + 1):
            en = jnp.where(lane == jnp.where(jn < i, jn, LANES - 1), 1.0, 0.0).astype(jnp.bfloat16)
            parts.append(jnp.concatenate([k_s[jnp.minimum(jn, i), :, pair_cols(h)], en], axis=1))
        s = _dot(jnp.concatenate(parts, axis=0), rhs_s[h])
        dst_s[h] = s
        dst_mx[h] = jnp.broadcast_to(jnp.max(s, axis=0, keepdims=True), (SUBLANES, blk))

    causal = (lax.broadcasted_iota(jnp.int32, (blk, blk), 0) <= lax.broadcasted_iota(jnp.int32, (blk, blk), 1))
    set_a, set_b = (sa_s, ma_s), (sb_s, mb_s)
    for h in range(MOBA_HEADS):
        pair_scores(set_a, h, 0)
        s = jnp.where(causal, s_own[h], MASKED)
        m0 = jnp.max(s, axis=0, keepdims=True)
        acc_s[h] = _dot(vts[h], jnp.exp2(_bf16(s - m0)))
        m_s[h] = jnp.broadcast_to(m0, (SUBLANES, blk))

    def attend(src, dst, tp):
        src_s, src_mx = src
        ja = jnp.minimum(2 * tp, i)
        jb = jnp.minimum(2 * tp + 1, i)
        for h in range(MOBA_HEADS):
            if dst is not None:
                pair_scores(dst, h, tp + 1)
            m_old = m_s[h]
            m_new = jnp.maximum(m_old, src_mx[h])
            p = jnp.exp2(_bf16(src_s[h] - m_new[0:1]))
            pv = _dot(jnp.concatenate([vt_s[ja, h], vt_s[jb, h]], axis=1), p)
            acc_s[h] = acc_s[h] * jnp.exp2(m_old[0:1] - m_new[0:1]) + pv
            m_s[h] = m_new

    n_pairs = (i + 1) // 2

    def two_pairs(u, carry):
        attend(set_a, set_b, 2 * u)
        attend(set_b, set_a, 2 * u + 1)
        return carry

    lax.fori_loop(0, n_pairs // 2, two_pairs, 0)

    @pl.when(n_pairs % 2 == 1)
    def _():
        attend(set_a, None, n_pairs - 1)

    outs = []
    for h in range(MOBA_HEADS):
        a = acc_s[h]
        outs.append(a[:MOBA_HEAD_DIM] / a[MOBA_HEAD_DIM:MOBA_HEAD_DIM + 1])
    y_ref[0] = _bf16(jnp.concatenate(outs, axis=0).T)


def _attn(qt, k, vt, kbar):
    bsz, nb, _, blk = qt.shape
    nbr = pl.cdiv(nb, BF16_ROWS) * BF16_ROWS
    assert nbr < LANES
    blk4 = lambda b, i: (b, i, 0, 0)
    return pl.pallas_call(
        _attn_kernel,
        grid=(bsz, nb),
        in_specs=[
            pl.BlockSpec((1, 1, MOBA_WIDTH, blk), blk4),
            pl.BlockSpec((1, blk, MOBA_WIDTH), lambda b, i: (b, i, 0)),
            pl.BlockSpec((1, 1, MOBA_WIDTH, blk), blk4),
            pl.BlockSpec((1, 1, SUBLANES, MOBA_WIDTH), blk4),
        ],
        out_specs=pl.BlockSpec((1, blk, MOBA_WIDTH), lambda b, i: (b, i, 0)),
        out_shape=jax.ShapeDtypeStruct((bsz, nb * blk, MOBA_WIDTH), jnp.bfloat16),
        scratch_shapes=[
            pltpu.VMEM((nb, blk, MOBA_WIDTH), jnp.bfloat16),
            pltpu.VMEM((nb, MOBA_HEADS, VT_ROWS, blk), jnp.bfloat16),
            pltpu.VMEM((nbr, MOBA_WIDTH), jnp.float32),
            pltpu.VMEM((MOBA_HEADS, 2 * LANES, blk), jnp.bfloat16),
            pltpu.VMEM((MOBA_HEADS, 2 * blk, blk), jnp.float32),
            pltpu.VMEM((MOBA_HEADS, 2 * blk, blk), jnp.float32),
            pltpu.VMEM((MOBA_HEADS, SUBLANES, blk), jnp.float32),
            pltpu.VMEM((MOBA_HEADS, SUBLANES, blk), jnp.float32),
            pltpu.VMEM((MOBA_HEADS, VT_ROWS, blk), jnp.float32),
            pltpu.VMEM((MOBA_HEADS, SUBLANES, blk), jnp.float32),
        ],
        compiler_params=_params(VMEM_LIMIT_ATTN),
        name="attn",
    )(qt, k, vt, kbar)


def _ffn_kernel(x_ref, partial_ref, gate1_ref, y_ref, wbm_ref, wo_ref, g_ref, wup_ref, cw_ref, cb_ref, wdown_ref,
                out_ref, aext_s, act_s):
    tm = TOKEN_TILE

    @pl.when(pl.program_id(1) == 0)
    def _():
        aext_s[0:SUBLANES, :] = jnp.zeros((SUBLANES, D_FF), jnp.float32)

    merged = partial_ref[0] + gate1_ref[0].astype(jnp.float32) * _dot(y_ref[0], wbm_ref[...])
    x = x_ref[0] + _dot(_bf16(merged), wo_ref[...])
    rs = lax.rsqrt(jnp.mean(x * x, axis=-1, keepdims=True) + EPS)
    xg = _bf16(x * g_ref[...])
    for c0, cw in FFN_CHUNKS:
        cs = slice(c0, c0 + cw)
        a = _dot(xg, wup_ref[:, c0:c0 + cw]) * rs
        b = _dot(xg, wup_ref[:, D_FF + c0:D_FF + c0 + cw]) * rs
        aext_s[SUBLANES:SUBLANES + tm, cs] = a
        a1 = aext_s[SUBLANES - 1:SUBLANES - 1 + tm, cs]
        a2 = aext_s[SUBLANES - 2:SUBLANES - 2 + tm, cs]
        aext_s[0:SUBLANES, cs] = a[tm - SUBLANES:, :]
        ac = a2 * cw_ref[0:1, cs] + a1 * cw_ref[1:2, cs] + a * cw_ref[2:3, cs] + cb_ref[:, cs]
        act_s[:, cs] = _bf16(ac * jax.nn.sigmoid(ac) * b)
    out_ref[0] = x + _dot(act_s[...], wdown_ref[...])


def _ffn(x, partial, gate1, y_moba, w_br_moba, w_o, g_ffn, w_up, ffn_conv_w, ffn_conv_b, w_down):
    bsz, s, d = x.shape
    tm = TOKEN_TILE
    assert s % tm == 0
    tile = lambda b, t: (b, t, 0)
    return pl.pallas_call(
        _ffn_kernel,
        grid=(bsz, s // tm),
        in_specs=[
            pl.BlockSpec((1, tm, d), tile),
            pl.BlockSpec((1, tm, d), tile),
            pl.BlockSpec((1, tm, d), tile),
            pl.BlockSpec((1, tm, MOBA_WIDTH), tile),
            _resident((MOBA_WIDTH, d)),
            _resident((d, d)),
            _resident((1, d)),
            _resident((d, 2 * D_FF)),
            _resident((3, D_FF)),
            _resident((1, D_FF)),
            _resident((D_FF, d)),
        ],
        out_specs=pl.BlockSpec((1, tm, d), tile),
        out_shape=jax.ShapeDtypeStruct((bsz, s, d), jnp.float32),
        scratch_shapes=[
            pltpu.VMEM((SUBLANES + tm, D_FF), jnp.float32),
            pltpu.VMEM((tm, D_FF), jnp.bfloat16),
        ],
        compiler_params=_params(VMEM_LIMIT_FFN),
        name="ffn",
    )(x, partial, gate1, y_moba, _bf16(w_br_moba), _bf16(w_o), g_ffn.reshape(1, d), _bf16(w_up), ffn_conv_w,
      ffn_conv_b.reshape(1, D_FF), _bf16(w_down))


def kernel(x, mem, g_mix, w_in, b_gate, conv_w, conv_b, moba_q_gain, moba_k_gain, g_mem, w_mem_kv, memq_gain,
           memk_gain, w_br_conv, w_br_moba, w_br_mem, w_o, g_ffn, w_up, ffn_conv_w, ffn_conv_b, w_down):
    memkt, memv = _memkv(mem, g_mem, w_mem_kv, memk_gain)
    partial, gate1, k, qt, vt, kbar = _proj(x, memkt, memv, g_mix, w_in, b_gate, conv_w, conv_b, moba_q_gain,
                                            moba_k_gain, memq_gain, w_br_conv, w_br_mem)
    y_moba = _attn(qt, k, vt, kbar)
    return _ffn(x, partial, gate1, y_moba, w_br_moba, w_o, g_ffn, w_up, ffn_conv_w, ffn_conv_b, w_down)
```

```python
import math

import jax
import jax.numpy as jnp
from jax import lax
from jax.experimental import pallas as pl
from jax.experimental.pallas import tpu as pltpu

D_MODEL = 1024
MEM_LEN = 256
CONV_CH = 512
MOBA_HEADS = 8
MOBA_HEAD_DIM = 64
MOBA_WIDTH = MOBA_HEADS * MOBA_HEAD_DIM
MOBA_BLOCK = 256
MOBA_TOPK = 3
MEM_HEADS = 4
MEM_HEAD_DIM = 128
MEM_WIDTH = MEM_HEADS * MEM_HEAD_DIM
D_FF = 2816
EPS = 1e-6

W_CONV = 0
W_Q = 3 * CONV_CH
W_K = W_Q + MOBA_WIDTH
W_V = W_K + MOBA_WIDTH
W_QMEM = W_V + MOBA_WIDTH
W_GATE = W_QMEM + MEM_WIDTH
IN_COLS = W_GATE + 3 * D_MODEL

LANES = 128
SUBLANES = 8
BF16_ROWS = 16
PAIR = 2 * MOBA_HEAD_DIM
VT_ROWS = MOBA_HEAD_DIM + BF16_ROWS
MASKED = -1e30
VMEM_LIMIT_MIXER = 56 * 1024 * 1024
VMEM_LIMIT_FFN = 48 * 1024 * 1024
FFN_TILE = 512
FFN_CHUNKS = ((0, 1024), (1024, 1024), (2048, 768))

_NT = (((1,), (1,)), ((), ()))


def _dot(a, b):
    return jnp.dot(a, b, preferred_element_type=jnp.float32)


def _dot_nt(a, b):
    return lax.dot_general(a, b, _NT, preferred_element_type=jnp.float32)


def _rmsnorm(x, g):
    return x * lax.rsqrt(jnp.mean(x * x, axis=-1, keepdims=True) + EPS) * g


def _bf16(x):
    return x.astype(jnp.bfloat16)


def _memkv_kernel(mem_ref, g_ref, w_ref, kg_ref, k_ref, v_ref):
    mn = _rmsnorm(mem_ref[0], g_ref[...])
    kv = _dot(_bf16(mn), w_ref[...])
    for h in range(MEM_HEADS):
        lo = h * MEM_HEAD_DIM
        kh = _rmsnorm(kv[:, lo:lo + MEM_HEAD_DIM], kg_ref[...])
        k_ref[0, :, lo:lo + MEM_HEAD_DIM] = _bf16(kh)
    v_ref[0] = _bf16(kv[:, MEM_WIDTH:])


def _memkv(mem, g_mem, w_mem_kv, memk_gain):
    bsz, m, d = mem.shape
    const = lambda b: (0, 0)
    return pl.pallas_call(
        _memkv_kernel,
        grid=(bsz,),
        in_specs=[
            pl.BlockSpec((1, m, d), lambda b: (b, 0, 0)),
            pl.BlockSpec((1, d), const),
            pl.BlockSpec((d, 2 * MEM_WIDTH), const),
            pl.BlockSpec((1, MEM_HEAD_DIM), const),
        ],
        out_specs=[
            pl.BlockSpec((1, m, MEM_WIDTH), lambda b: (b, 0, 0)),
            pl.BlockSpec((1, m, MEM_WIDTH), lambda b: (b, 0, 0)),
        ],
        out_shape=[
            jax.ShapeDtypeStruct((bsz, m, MEM_WIDTH), jnp.bfloat16),
            jax.ShapeDtypeStruct((bsz, m, MEM_WIDTH), jnp.bfloat16),
        ],
        name="memkv",
    )(mem, g_mem.reshape(1, d), _bf16(w_mem_kv), memk_gain.reshape(1, MEM_HEAD_DIM))


def _mixer_kernel(x_ref, gmix_ref, wmain_ref, wqvt_ref, bgate_ref, convw_ref, convb_ref, qgt_ref, kg_ref, gsum_ref,
                  memk_ref, memv_ref, mqg_ref, wbc_ref, wbm_ref, wbx_ref, wo_ref,
                  out_ref,
                  k_s, vt_s, kbar_s, uext_s, rhs_s, sa_s, sb_s, ma_s, mb_s, acc_s, m_s):
    i = pl.program_id(1)
    blk = MOBA_BLOCK
    nbr = kbar_s.shape[0]

    @pl.when(i == 0)
    def _():
        uext_s[0:SUBLANES, :] = jnp.zeros((SUBLANES, CONV_CH), jnp.float32)
        kbar_s[...] = jnp.zeros(kbar_s.shape, jnp.float32)

    x = x_ref[0]
    hb = _bf16(_rmsnorm(x, gmix_ref[...]))

    def proj(lo, width):
        return _dot(hb, wmain_ref[:, lo:lo + width])

    def gate(g, n):
        return jax.nn.sigmoid(g + bgate_ref[:, n * D_MODEL:(n + 1) * D_MODEL])

    pc = proj(W_CONV, 3 * CONV_CH)
    k_raw = proj(W_K, MOBA_WIDTH)
    qvt = _dot_nt(wqvt_ref[...], hb)
    qm = proj(W_QMEM, MEM_WIDTH)

    u = pc[:, CONV_CH:2 * CONV_CH] * pc[:, 2 * CONV_CH:]
    uext_s[SUBLANES:SUBLANES + blk, :] = u
    u1 = uext_s[SUBLANES - 1:SUBLANES - 1 + blk, :]
    u2 = uext_s[SUBLANES - 2:SUBLANES - 2 + blk, :]
    uext_s[0:SUBLANES, :] = u[blk - SUBLANES:, :]
    conv = u2 * convw_ref[0:1, :] + u1 * convw_ref[1:2, :] + u * convw_ref[2:3, :] + convb_ref[...]
    y_conv = _bf16(pc[:, :CONV_CH] * conv)

    k2_b = _bf16(k_raw * k_raw)

    q3 = qvt[:MOBA_WIDTH].reshape(MOBA_HEADS, MOBA_HEAD_DIM, blk)
    ssq = jnp.sum(q3 * q3, axis=1, keepdims=True)
    qbt = _bf16((q3 * lax.rsqrt(ssq * (1.0 / MOBA_HEAD_DIM) + EPS)).reshape(MOBA_WIDTH, blk) * qgt_ref[...])
    vbt = _bf16(qvt[MOBA_WIDTH:])

    qmb = [_bf16(_rmsnorm(qm[:, h * MEM_HEAD_DIM:(h + 1) * MEM_HEAD_DIM], mqg_ref[...])) for h in range(MEM_HEADS)]

    g0 = proj(W_GATE, D_MODEL)
    kss = _dot(k2_b, gsum_ref[...])
    p_conv = _dot(y_conv, wbc_ref[...])
    s_mem = [_dot_nt(qmb[h], memk_ref[0, :, h * MEM_HEAD_DIM:(h + 1) * MEM_HEAD_DIM]) for h in range(MEM_HEADS)]
    g2 = proj(W_GATE + 2 * D_MODEL, D_MODEL)

    merged = gate(g0, 0) * p_conv
    kn = k_raw * lax.rsqrt(kss * (1.0 / MOBA_HEAD_DIM) + EPS) * kg_ref[...]
    kb = _bf16(kn)
    k_s[i] = kb
    kbar_row = jnp.mean(kn, axis=0, keepdims=True)
    kb_rows = lax.broadcasted_iota(jnp.int32, kbar_s.shape, 0)
    kbar_prev = kbar_s[...]
    kbar_s[...] = jnp.where(kb_rows == i, kbar_row, kbar_prev)
    kbar_b = _bf16(kbar_prev)

    zeros_h = jnp.zeros((MOBA_HEAD_DIM, blk), jnp.bfloat16)
    qpairs = []
    for h in range(MOBA_HEADS):
        qh = qbt[h * MOBA_HEAD_DIM:(h + 1) * MOBA_HEAD_DIM]
        qpairs.append(jnp.concatenate([qh, zeros_h] if h % 2 == 0 else [zeros_h, qh], axis=0))

    p_mem = [jnp.exp(s - jnp.max(s, axis=-1, keepdims=True)) for s in s_mem]

    pair_cols = lambda h: slice((h // 2) * PAIR, (h // 2 + 1) * PAIR)
    gates = [_dot(kbar_b[:, pair_cols(h)], qpairs[h]) for h in range(MOBA_HEADS)]
    s_own = [_dot(kb[:, pair_cols(h)], qpairs[h]) for h in range(MOBA_HEADS)]
    o_mem = [_dot(_bf16(p_mem[h]), memv_ref[0, :, h * MEM_HEAD_DIM:(h + 1) * MEM_HEAD_DIM])
             / jnp.sum(p_mem[h], axis=-1, keepdims=True) for h in range(MEM_HEADS)]
    merged = merged + gate(g2, 2) * _dot(_bf16(jnp.concatenate(o_mem, axis=-1)), wbx_ref[...])

    lane = lax.broadcasted_iota(jnp.int32, (blk, LANES), 1)
    brow = lax.broadcasted_iota(jnp.int32, (nbr, blk), 0)
    browf = brow.astype(jnp.float32)
    neg_inf = jnp.float32(-jnp.inf)
    sel_pad = jnp.full((LANES - nbr, blk), MASKED, jnp.bfloat16)
    ones_r = jnp.ones((BF16_ROWS, blk), jnp.bfloat16)
    vts = []
    for h in range(MOBA_HEADS):
        g = jnp.where(brow < i, gates[h], neg_inf)
        sel = jnp.full((nbr, blk), MASKED, jnp.float32)
        for _ in range(MOBA_TOPK):
            mx = jnp.max(g, axis=0, keepdims=True)
            idx = jnp.min(jnp.where(g == mx, browf, float(nbr)), axis=0, keepdims=True)
            pick = browf == idx
            sel = jnp.where(pick & (mx > neg_inf), 0.0, sel)
            g = jnp.where(pick, neg_inf, g)
        rhs_s[h] = jnp.concatenate([qpairs[h], _bf16(sel), sel_pad], axis=0)
        vt = jnp.concatenate([vbt[h * MOBA_HEAD_DIM:(h + 1) * MOBA_HEAD_DIM], ones_r], axis=0)
        vt_s[i, h] = vt
        vts.append(vt)

    def pair_scores(dst, h, tp):
        dst_s, dst_mx = dst
        parts = []
        for jn in (2 * tp, 2 * tp + 1):
            en = jnp.where(lane == jnp.where(jn < i, jn, LANES - 1), 1.0, 0.0).astype(jnp.bfloat16)
            parts.append(jnp.concatenate([k_s[jnp.minimum(jn, i), :, pair_cols(h)], en], axis=1))
        s = _dot(jnp.concatenate(parts, axis=0), rhs_s[h])
        dst_s[h] = s
        dst_mx[h] = jnp.broadcast_to(jnp.max(s, axis=0, keepdims=True), (SUBLANES, blk))

    causal = (lax.broadcasted_iota(jnp.int32, (blk, blk), 0) <= lax.broadcasted_iota(jnp.int32, (blk, blk), 1))
    set_a, set_b = (sa_s, ma_s), (sb_s, mb_s)
    for h in range(MOBA_HEADS):
        pair_scores(set_a, h, 0)
        s = jnp.where(causal, s_own[h], MASKED)
        m0 = jnp.max(s, axis=0, keepdims=True)
        acc_s[h] = _dot(vts[h], jnp.exp2(_bf16(s - m0)))
        m_s[h] = jnp.broadcast_to(m0, (SUBLANES, blk))

    def attend(src, dst, tp):
        src_s, src_mx = src
        ja = jnp.minimum(2 * tp, i)
        jb = jnp.minimum(2 * tp + 1, i)
        for h in range(MOBA_HEADS):
            if dst is not None:
                pair_scores(dst, h, tp + 1)
            m_old = m_s[h]
            m_new = jnp.maximum(m_old, src_mx[h])
            p = jnp.exp2(_bf16(src_s[h] - m_new[0:1]))
            pv = _dot(jnp.concatenate([vt_s[ja, h], vt_s[jb, h]], axis=1), p)
            acc_s[h] = acc_s[h] * jnp.exp2(m_old[0:1] - m_new[0:1]) + pv
            m_s[h] = m_new

    n_pairs = (i + 1) // 2

    def two_pairs(u, carry):
        attend(set_a, set_b, 2 * u)
        attend(set_b, set_a, 2 * u + 1)
        return carry

    lax.fori_loop(0, n_pairs // 2, two_pairs, 0)

    @pl.when(n_pairs % 2 == 1)
    def _():
        attend(set_a, None, n_pairs - 1)

    g1 = proj(W_GATE + D_MODEL, D_MODEL)
    outs = []
    for h in range(MOBA_HEADS):
        a = acc_s[h]
        outs.append(a[:MOBA_HEAD_DIM] / a[MOBA_HEAD_DIM:MOBA_HEAD_DIM + 1])
    y_moba = jnp.concatenate(outs, axis=0).T
    merged = merged + gate(g1, 1) * _dot(_bf16(y_moba), wbm_ref[...])

    out_ref[0] = x + _dot(_bf16(merged), wo_ref[...])


def _resident(shape):
    return pl.BlockSpec(shape, lambda b, i: (0,) * len(shape), pipeline_mode=pl.Buffered(1))


def _mixer(x, memk, memv, g_mix, w_in, b_gate, conv_w, conv_b, moba_q_gain, moba_k_gain, memq_gain,
           w_br_conv, w_br_moba, w_br_mem, w_o):
    bsz, s, d = x.shape
    blk = MOBA_BLOCK
    assert s % blk == 0 and d == D_MODEL
    nb = s // blk
    nbr = pl.cdiv(nb, BF16_ROWS) * BF16_ROWS
    assert nbr < LANES
    w_main = _bf16(w_in)
    w_qv = lax.optimization_barrier(jnp.concatenate([w_in[:, W_Q:W_K], w_in[:, W_V:W_QMEM]], axis=1))
    w_qvt = _bf16(w_qv.T)
    qgt = jnp.broadcast_to(
        (jnp.tile(moba_q_gain, MOBA_HEADS) * (MOBA_HEAD_DIM ** -0.5 * math.log2(math.e)))[:, None], (MOBA_WIDTH, blk))
    kg = jnp.tile(moba_k_gain, MOBA_HEADS).reshape(1, MOBA_WIDTH)
    mqg = memq_gain.reshape(1, MEM_HEAD_DIM) * (MEM_HEAD_DIM ** -0.5)
    head_of = jnp.arange(MOBA_WIDTH) // MOBA_HEAD_DIM
    gsum = (head_of[:, None] == head_of[None, :]).astype(jnp.bfloat16)
    return pl.pallas_call(
        _mixer_kernel,
        grid=(bsz, nb),
        in_specs=[
            pl.BlockSpec((1, blk, d), lambda b, i: (b, i, 0)),
            _resident((1, d)),
            _resident((d, IN_COLS)),
            _resident((2 * MOBA_WIDTH, d)),
            _resident((1, 3 * d)),
            _resident((3, CONV_CH)),
            _resident((1, CONV_CH)),
            _resident((MOBA_WIDTH, blk)),
            _resident((1, MOBA_WIDTH)),
            _resident((MOBA_WIDTH, MOBA_WIDTH)),
            pl.BlockSpec((1, MEM_LEN, MEM_WIDTH), lambda b, i: (b, 0, 0)),
            pl.BlockSpec((1, MEM_LEN, MEM_WIDTH), lambda b, i: (b, 0, 0)),
            _resident((1, MEM_HEAD_DIM)),
            _resident((CONV_CH, d)),
            _resident((MOBA_WIDTH, d)),
            _resident((MEM_WIDTH, d)),
            _resident((d, d)),
        ],
        out_specs=pl.BlockSpec((1, blk, d), lambda b, i: (b, i, 0)),
        out_shape=jax.ShapeDtypeStruct((bsz, s, d), jnp.float32),
        scratch_shapes=[
            pltpu.VMEM((nb, blk, MOBA_WIDTH), jnp.bfloat16),
            pltpu.VMEM((nb, MOBA_HEADS, VT_ROWS, blk), jnp.bfloat16),
            pltpu.VMEM((nbr, MOBA_WIDTH), jnp.float32),
            pltpu.VMEM((SUBLANES + blk, CONV_CH), jnp.float32),
            pltpu.VMEM((MOBA_HEADS, 2 * LANES, blk), jnp.bfloat16),
            pltpu.VMEM((MOBA_HEADS, 2 * blk, blk), jnp.float32),
            pltpu.VMEM((MOBA_HEADS, 2 * blk, blk), jnp.float32),
            pltpu.VMEM((MOBA_HEADS, SUBLANES, blk), jnp.float32),
            pltpu.VMEM((MOBA_HEADS, SUBLANES, blk), jnp.float32),
            pltpu.VMEM((MOBA_HEADS, VT_ROWS, blk), jnp.float32),
            pltpu.VMEM((MOBA_HEADS, SUBLANES, blk), jnp.float32),
        ],
        compiler_params=pltpu.CompilerParams(
            dimension_semantics=("arbitrary", "arbitrary"),
            vmem_limit_bytes=VMEM_LIMIT_MIXER),
        name="mixer",
    )(x, g_mix.reshape(1, d), w_main, w_qvt, b_gate.reshape(1, 3 * d), conv_w, conv_b.reshape(1, CONV_CH),
      qgt, kg, gsum, memk, memv, mqg, _bf16(w_br_conv), _bf16(w_br_moba), _bf16(w_br_mem), _bf16(w_o))


def _ffn_kernel(x_ref, g_ref, wup_ref, cw_ref, cb_ref, wdown_ref, out_ref, aext_s, act_s):
    t = pl.program_id(1)
    tm = FFN_TILE
    x = x_ref[0]
    hb = _bf16(_rmsnorm(x, g_ref[...]))

    @pl.when(t == 0)
    def _():
        aext_s[0:SUBLANES, :] = jnp.zeros((SUBLANES, D_FF), jnp.float32)

    for c0, cw in FFN_CHUNKS:
        cs = slice(c0, c0 + cw)
        a = _dot(hb, wup_ref[:, c0:c0 + cw])
        b = _dot(hb, wup_ref[:, D_FF + c0:D_FF + c0 + cw])
        aext_s[SUBLANES:SUBLANES + tm, cs] = a
        a1 = aext_s[SUBLANES - 1:SUBLANES - 1 + tm, cs]
        a2 = aext_s[SUBLANES - 2:SUBLANES - 2 + tm, cs]
        aext_s[0:SUBLANES, cs] = a[tm - SUBLANES:, :]
        ac = a2 * cw_ref[0:1, cs] + a1 * cw_ref[1:2, cs] + a * cw_ref[2:3, cs] + cb_ref[:, cs]
        act_s[:, cs] = _bf16(ac * jax.nn.sigmoid(ac) * b)
    out_ref[0] = x + _dot(act_s[...], wdown_ref[...])


def _ffn(x, g_ffn, w_up, ffn_conv_w, ffn_conv_b, w_down):
    bsz, s, d = x.shape
    tm = FFN_TILE
    assert s % tm == 0
    return pl.pallas_call(
        _ffn_kernel,
        grid=(bsz, s // tm),
        in_specs=[
            pl.BlockSpec((1, tm, d), lambda b, t: (b, t, 0)),
            _resident((1, d)),
            _resident((d, 2 * D_FF)),
            _resident((3, D_FF)),
            _resident((1, D_FF)),
            _resident((D_FF, d)),
        ],
        out_specs=pl.BlockSpec((1, tm, d), lambda b, t: (b, t, 0)),
        out_shape=jax.ShapeDtypeStruct((bsz, s, d), jnp.float32),
        scratch_shapes=[
            pltpu.VMEM((SUBLANES + tm, D_FF), jnp.float32),
            pltpu.VMEM((tm, D_FF), jnp.bfloat16),
        ],
        compiler_params=pltpu.CompilerParams(
            dimension_semantics=("arbitrary", "arbitrary"),
            vmem_limit_bytes=VMEM_LIMIT_FFN),
        name="ffn",
    )(x, g_ffn.reshape(1, d), _bf16(w_up), ffn_conv_w, ffn_conv_b.reshape(1, D_FF), _bf16(w_down))


def kernel(x, mem, g_mix, w_in, b_gate, conv_w, conv_b, moba_q_gain, moba_k_gain, g_mem, w_mem_kv, memq_gain,
           memk_gain, w_br_conv, w_br_moba, w_br_mem, w_o, g_ffn, w_up, ffn_conv_w, ffn_conv_b, w_down):
    memk, memv = _memkv(mem, g_mem, w_mem_kv, memk_gain)
    x = _mixer(x, memk, memv, g_mix, w_in, b_gate, conv_w, conv_b, moba_q_gain, moba_k_gain, memq_gain,
               w_br_conv, w_br_moba, w_br_mem, w_o)
    return _ffn(x, g_ffn, w_up, ffn_conv_w, ffn_conv_b, w_down)
```

```python
import math

import jax
import jax.numpy as jnp
from jax import lax
from jax.experimental import pallas as pl
from jax.experimental.pallas import tpu as pltpu

D_MODEL = 1024
MEM_LEN = 256
CONV_CH = 512
MOBA_HEADS = 8
MOBA_HEAD_DIM = 64
MOBA_WIDTH = MOBA_HEADS * MOBA_HEAD_DIM
MOBA_BLOCK = 256
MOBA_TOPK = 3
MEM_HEADS = 4
MEM_HEAD_DIM = 128
MEM_WIDTH = MEM_HEADS * MEM_HEAD_DIM
D_FF = 2816
EPS = 1e-6

W_CONV = 0
W_Q = 3 * CONV_CH
W_K = W_Q + MOBA_WIDTH
W_V = W_K + MOBA_WIDTH
W_QMEM = W_V + MOBA_WIDTH
W_GATE = W_QMEM + MEM_WIDTH
IN_COLS = W_GATE + 3 * D_MODEL

LANES = 128
SUBLANES = 8
BF16_ROWS = 16
PAIR = 2 * MOBA_HEAD_DIM
VT_ROWS = MOBA_HEAD_DIM + BF16_ROWS
MASKED = -1e30
MIXER_SUBS = 2
VMEM_LIMIT_MIXER = 60 * 1024 * 1024
VMEM_LIMIT_FFN = 48 * 1024 * 1024
FFN_TILE = 512
FFN_CHUNKS = ((0, 1024), (1024, 1024), (2048, 768))

_NT = (((1,), (1,)), ((), ()))


def _dot(a, b):
    return jnp.dot(a, b, preferred_element_type=jnp.float32)


def _dot_nt(a, b):
    return lax.dot_general(a, b, _NT, preferred_element_type=jnp.float32)


def _rmsnorm(x, g):
    return x * lax.rsqrt(jnp.mean(x * x, axis=-1, keepdims=True) + EPS) * g


def _bf16(x):
    return x.astype(jnp.bfloat16)


def _memkv_kernel(mem_ref, g_ref, w_ref, kg_ref, k_ref, v_ref):
    mn = _rmsnorm(mem_ref[0], g_ref[...])
    kv = _dot(_bf16(mn), w_ref[...])
    for h in range(MEM_HEADS):
        lo = h * MEM_HEAD_DIM
        kh = _rmsnorm(kv[:, lo:lo + MEM_HEAD_DIM], kg_ref[...])
        k_ref[0, :, lo:lo + MEM_HEAD_DIM] = _bf16(kh)
    v_ref[0] = _bf16(kv[:, MEM_WIDTH:])


def _memkv(mem, g_mem, w_mem_kv, memk_gain):
    bsz, m, d = mem.shape
    const = lambda b: (0, 0)
    return pl.pallas_call(
        _memkv_kernel,
        grid=(bsz,),
        in_specs=[
            pl.BlockSpec((1, m, d), lambda b: (b, 0, 0)),
            pl.BlockSpec((1, d), const),
            pl.BlockSpec((d, 2 * MEM_WIDTH), const),
            pl.BlockSpec((1, MEM_HEAD_DIM), const),
        ],
        out_specs=[
            pl.BlockSpec((1, m, MEM_WIDTH), lambda b: (b, 0, 0)),
            pl.BlockSpec((1, m, MEM_WIDTH), lambda b: (b, 0, 0)),
        ],
        out_shape=[
            jax.ShapeDtypeStruct((bsz, m, MEM_WIDTH), jnp.bfloat16),
            jax.ShapeDtypeStruct((bsz, m, MEM_WIDTH), jnp.bfloat16),
        ],
        name="memkv",
    )(mem, g_mem.reshape(1, d), _bf16(w_mem_kv), memk_gain.reshape(1, MEM_HEAD_DIM))


def _mixer_kernel(x_ref, *refs):
    uext_s, kbar_s = refs[-8], refs[-9]

    @pl.when(pl.program_id(1) == 0)
    def _():
        uext_s[0:SUBLANES, :] = jnp.zeros((SUBLANES, CONV_CH), jnp.float32)
        kbar_s[...] = jnp.zeros(kbar_s.shape, jnp.float32)

    for sub in range(MIXER_SUBS):
        _mixer_block(pl.program_id(1) * MIXER_SUBS + sub, slice(sub * MOBA_BLOCK, (sub + 1) * MOBA_BLOCK), x_ref, *refs)


def _mixer_block(i, rows, x_ref, gmix_ref, wmain_ref, wqvt_ref, bgate_ref, convw_ref, convb_ref, qgt_ref, kg_ref,
                 gsum_ref, memk_ref, memv_ref, mqg_ref, wbc_ref, wbm_ref, wbx_ref, wo_ref,
                 out_ref,
                 k_s, vt_s, kbar_s, uext_s, rhs_s, sa_s, sb_s, ma_s, mb_s, acc_s, m_s):
    blk = MOBA_BLOCK
    nbr = kbar_s.shape[0]
    x = x_ref[0, rows, :]
    hb = _bf16(_rmsnorm(x, gmix_ref[...]))

    def proj(lo, width):
        return _dot(hb, wmain_ref[:, lo:lo + width])

    def gate(g, n):
        return jax.nn.sigmoid(g + bgate_ref[:, n * D_MODEL:(n + 1) * D_MODEL])

    pc = proj(W_CONV, 3 * CONV_CH)
    k_raw = proj(W_K, MOBA_WIDTH)
    qvt = _dot_nt(wqvt_ref[...], hb)
    qm = proj(W_QMEM, MEM_WIDTH)

    u = pc[:, CONV_CH:2 * CONV_CH] * pc[:, 2 * CONV_CH:]
    uext_s[SUBLANES:SUBLANES + blk, :] = u
    u1 = uext_s[SUBLANES - 1:SUBLANES - 1 + blk, :]
    u2 = uext_s[SUBLANES - 2:SUBLANES - 2 + blk, :]
    uext_s[0:SUBLANES, :] = u[blk - SUBLANES:, :]
    conv = u2 * convw_ref[0:1, :] + u1 * convw_ref[1:2, :] + u * convw_ref[2:3, :] + convb_ref[...]
    y_conv = _bf16(pc[:, :CONV_CH] * conv)

    k2_b = _bf16(k_raw * k_raw)

    q3 = qvt[:MOBA_WIDTH].reshape(MOBA_HEADS, MOBA_HEAD_DIM, blk)
    ssq = jnp.sum(q3 * q3, axis=1, keepdims=True)
    qbt = _bf16((q3 * lax.rsqrt(ssq * (1.0 / MOBA_HEAD_DIM) + EPS)).reshape(MOBA_WIDTH, blk) * qgt_ref[...])
    vbt = _bf16(qvt[MOBA_WIDTH:])

    qmb = [_bf16(_rmsnorm(qm[:, h * MEM_HEAD_DIM:(h + 1) * MEM_HEAD_DIM], mqg_ref[...])) for h in range(MEM_HEADS)]

    g0 = proj(W_GATE, D_MODEL)
    kss = _dot(k2_b, gsum_ref[...])
    p_conv = _dot(y_conv, wbc_ref[...])
    s_mem = [_dot_nt(qmb[h], memk_ref[0, :, h * MEM_HEAD_DIM:(h + 1) * MEM_HEAD_DIM]) for h in range(MEM_HEADS)]
    g2 = proj(W_GATE + 2 * D_MODEL, D_MODEL)

    merged = gate(g0, 0) * p_conv
    kn = k_raw * lax.rsqrt(kss * (1.0 / MOBA_HEAD_DIM) + EPS) * kg_ref[...]
    kb = _bf16(kn)
    k_s[i] = kb
    kbar_row = jnp.mean(kn, axis=0, keepdims=True)
    kb_rows = lax.broadcasted_iota(jnp.int32, kbar_s.shape, 0)
    kbar_prev = kbar_s[...]
    kbar_s[...] = jnp.where(kb_rows == i, kbar_row, kbar_prev)
    kbar_b = _bf16(kbar_prev)

    zeros_h = jnp.zeros((MOBA_HEAD_DIM, blk), jnp.bfloat16)
    qpairs = []
    for h in range(MOBA_HEADS):
        qh = qbt[h * MOBA_HEAD_DIM:(h + 1) * MOBA_HEAD_DIM]
        qpairs.append(jnp.concatenate([qh, zeros_h] if h % 2 == 0 else [zeros_h, qh], axis=0))

    p_mem = [jnp.exp(s - jnp.max(s, axis=-1, keepdims=True)) for s in s_mem]

    pair_cols = lambda h: slice((h // 2) * PAIR, (h // 2 + 1) * PAIR)
    gates = [_dot(kbar_b[:, pair_cols(h)], qpairs[h]) for h in range(MOBA_HEADS)]
    s_own = [_dot(kb[:, pair_cols(h)], qpairs[h]) for h in range(MOBA_HEADS)]
    o_mem = [_dot(_bf16(p_mem[h]), memv_ref[0, :, h * MEM_HEAD_DIM:(h + 1) * MEM_HEAD_DIM])
             / jnp.sum(p_mem[h], axis=-1, keepdims=True) for h in range(MEM_HEADS)]
    merged = merged + gate(g2, 2) * _dot(_bf16(jnp.concatenate(o_mem, axis=-1)), wbx_ref[...])

    lane = lax.broadcasted_iota(jnp.int32, (blk, LANES), 1)
    brow = lax.broadcasted_iota(jnp.int32, (nbr, blk), 0)
    browf = brow.astype(jnp.float32)
    neg_inf = jnp.float32(-jnp.inf)
    sel_pad = jnp.full((LANES - nbr, blk), MASKED, jnp.bfloat16)
    ones_r = jnp.ones((BF16_ROWS, blk), jnp.bfloat16)
    vts = []
    for h in range(MOBA_HEADS):
        g = jnp.where(brow < i, gates[h], neg_inf)
        sel = jnp.full((nbr, blk), MASKED, jnp.float32)
        for _ in range(MOBA_TOPK):
            mx = jnp.max(g, axis=0, keepdims=True)
            idx = jnp.min(jnp.where(g == mx, browf, float(nbr)), axis=0, keepdims=True)
            pick = browf == idx
            sel = jnp.where(pick & (mx > neg_inf), 0.0, sel)
            g = jnp.where(pick, neg_inf, g)
        rhs_s[h] = jnp.concatenate([qpairs[h], _bf16(sel), sel_pad], axis=0)
        vt = jnp.concatenate([vbt[h * MOBA_HEAD_DIM:(h + 1) * MOBA_HEAD_DIM], ones_r], axis=0)
        vt_s[i, h] = vt
        vts.append(vt)

    def pair_scores(dst, h, tp):
        dst_s, dst_mx = dst
        parts = []
        for jn in (2 * tp, 2 * tp + 1):
            en = jnp.where(lane == jnp.where(jn < i, jn, LANES - 1), 1.0, 0.0).astype(jnp.bfloat16)
            parts.append(jnp.concatenate([k_s[jnp.minimum(jn, i), :, pair_cols(h)], en], axis=1))
        s = _dot(jnp.concatenate(parts, axis=0), rhs_s[h])
        dst_s[h] = s
        dst_mx[h] = jnp.broadcast_to(jnp.max(s, axis=0, keepdims=True), (SUBLANES, blk))

    causal = (lax.broadcasted_iota(jnp.int32, (blk, blk), 0) <= lax.broadcasted_iota(jnp.int32, (blk, blk), 1))
    set_a, set_b = (sa_s, ma_s), (sb_s, mb_s)
    for h in range(MOBA_HEADS):
        pair_scores(set_a, h, 0)
        s = jnp.where(causal, s_own[h], MASKED)
        m0 = jnp.max(s, axis=0, keepdims=True)
        acc_s[h] = _dot(vts[h], jnp.exp2(_bf16(s - m0)))
        m_s[h] = jnp.broadcast_to(m0, (SUBLANES, blk))

    def attend(src, dst, tp):
        src_s, src_mx = src
        ja = jnp.minimum(2 * tp, i)
        jb = jnp.minimum(2 * tp + 1, i)
        for h in range(MOBA_HEADS):
            if dst is not None:
                pair_scores(dst, h, tp + 1)
            m_old = m_s[h]
            m_new = jnp.maximum(m_old, src_mx[h])
            p = jnp.exp2(_bf16(src_s[h] - m_new[0:1]))
            pv = _dot(jnp.concatenate([vt_s[ja, h], vt_s[jb, h]], axis=1), p)
            acc_s[h] = acc_s[h] * jnp.exp2(m_old[0:1] - m_new[0:1]) + pv
            m_s[h] = m_new

    n_pairs = (i + 1) // 2

    def two_pairs(u, carry):
        attend(set_a, set_b, 2 * u)
        attend(set_b, set_a, 2 * u + 1)
        return carry

    lax.fori_loop(0, n_pairs // 2, two_pairs, 0)

    @pl.when(n_pairs % 2 == 1)
    def _():
        attend(set_a, None, n_pairs - 1)

    g1 = proj(W_GATE + D_MODEL, D_MODEL)
    outs = []
    for h in range(MOBA_HEADS):
        a = acc_s[h]
        outs.append(a[:MOBA_HEAD_DIM] / a[MOBA_HEAD_DIM:MOBA_HEAD_DIM + 1])
    y_moba = jnp.concatenate(outs, axis=0).T
    merged = merged + gate(g1, 1) * _dot(_bf16(y_moba), wbm_ref[...])

    out_ref[0, rows, :] = x + _dot(_bf16(merged), wo_ref[...])


def _resident(shape):
    return pl.BlockSpec(shape, lambda b, i: (0,) * len(shape), pipeline_mode=pl.Buffered(1))


def _mixer(x, memk, memv, g_mix, w_in, b_gate, conv_w, conv_b, moba_q_gain, moba_k_gain, memq_gain,
           w_br_conv, w_br_moba, w_br_mem, w_o):
    bsz, s, d = x.shape
    blk = MOBA_BLOCK
    assert s % (MIXER_SUBS * blk) == 0 and d == D_MODEL
    nb = s // blk
    nbr = pl.cdiv(nb, BF16_ROWS) * BF16_ROWS
    assert nbr < LANES
    w_main = _bf16(w_in)
    w_qv = lax.optimization_barrier(jnp.concatenate([w_in[:, W_Q:W_K], w_in[:, W_V:W_QMEM]], axis=1))
    w_qvt = _bf16(w_qv.T)
    qgt = jnp.broadcast_to(
        (jnp.tile(moba_q_gain, MOBA_HEADS) * (MOBA_HEAD_DIM ** -0.5 * math.log2(math.e)))[:, None], (MOBA_WIDTH, blk))
    kg = jnp.tile(moba_k_gain, MOBA_HEADS).reshape(1, MOBA_WIDTH)
    mqg = memq_gain.reshape(1, MEM_HEAD_DIM) * (MEM_HEAD_DIM ** -0.5)
    head_of = jnp.arange(MOBA_WIDTH) // MOBA_HEAD_DIM
    gsum = (head_of[:, None] == head_of[None, :]).astype(jnp.bfloat16)
    return pl.pallas_call(
        _mixer_kernel,
        grid=(bsz, nb // MIXER_SUBS),
        in_specs=[
            pl.BlockSpec((1, MIXER_SUBS * blk, d), lambda b, i: (b, i, 0)),
            _resident((1, d)),
            _resident((d, IN_COLS)),
            _resident((2 * MOBA_WIDTH, d)),
            _resident((1, 3 * d)),
            _resident((3, CONV_CH)),
            _resident((1, CONV_CH)),
            _resident((MOBA_WIDTH, blk)),
            _resident((1, MOBA_WIDTH)),
            _resident((MOBA_WIDTH, MOBA_WIDTH)),
            pl.BlockSpec((1, MEM_LEN, MEM_WIDTH), lambda b, i: (b, 0, 0)),
            pl.BlockSpec((1, MEM_LEN, MEM_WIDTH), lambda b, i: (b, 0, 0)),
            _resident((1, MEM_HEAD_DIM)),
            _resident((CONV_CH, d)),
            _resident((MOBA_WIDTH, d)),
            _resident((MEM_WIDTH, d)),
            _resident((d, d)),
        ],
        out_specs=pl.BlockSpec((1, MIXER_SUBS * blk, d), lambda b, i: (b, i, 0)),
        out_shape=jax.ShapeDtypeStruct((bsz, s, d), jnp.float32),
        scratch_shapes=[
            pltpu.VMEM((nb, blk, MOBA_WIDTH), jnp.bfloat16),
            pltpu.VMEM((nb, MOBA_HEADS, VT_ROWS, blk), jnp.bfloat16),
            pltpu.VMEM((nbr, MOBA_WIDTH), jnp.float32),
            pltpu.VMEM((SUBLANES + blk, CONV_CH), jnp.float32),
            pltpu.VMEM((MOBA_HEADS, 2 * LANES, blk), jnp.bfloat16),
            pltpu.VMEM((MOBA_HEADS, 2 * blk, blk), jnp.float32),
            pltpu.VMEM((MOBA_HEADS, 2 * blk, blk), jnp.float32),
            pltpu.VMEM((MOBA_HEADS, SUBLANES, blk), jnp.float32),
            pltpu.VMEM((MOBA_HEADS, SUBLANES, blk), jnp.float32),
            pltpu.VMEM((MOBA_HEADS, VT_ROWS, blk), jnp.float32),
            pltpu.VMEM((MOBA_HEADS, SUBLANES, blk), jnp.float32),
        ],
        compiler_params=pltpu.CompilerParams(
            dimension_semantics=("arbitrary", "arbitrary"),
            vmem_limit_bytes=VMEM_LIMIT_MIXER),
        name="mixer",
    )(x, g_mix.reshape(1, d), w_main, w_qvt, b_gate.reshape(1, 3 * d), conv_w, conv_b.reshape(1, CONV_CH),
      qgt, kg, gsum, memk, memv, mqg, _bf16(w_br_conv), _bf16(w_br_moba), _bf16(w_br_mem), _bf16(w_o))


def _ffn_kernel(x_ref, g_ref, wup_ref, cw_ref, cb_ref, wdown_ref, out_ref, aext_s, act_s):
    t = pl.program_id(1)
    tm = FFN_TILE
    x = x_ref[0]
    hb = _bf16(_rmsnorm(x, g_ref[...]))

    @pl.when(t == 0)
    def _():
        aext_s[0:SUBLANES, :] = jnp.zeros((SUBLANES, D_FF), jnp.float32)

    for c0, cw in FFN_CHUNKS:
        cs = slice(c0, c0 + cw)
        a = _dot(hb, wup_ref[:, c0:c0 + cw])
        b = _dot(hb, wup_ref[:, D_FF + c0:D_FF + c0 + cw])
        aext_s[SUBLANES:SUBLANES + tm, cs] = a
        a1 = aext_s[SUBLANES - 1:SUBLANES - 1 + tm, cs]
        a2 = aext_s[SUBLANES - 2:SUBLANES - 2 + tm, cs]
        aext_s[0:SUBLANES, cs] = a[tm - SUBLANES:, :]
        ac = a2 * cw_ref[0:1, cs] + a1 * cw_ref[1:2, cs] + a * cw_ref[2:3, cs] + cb_ref[:, cs]
        act_s[:, cs] = _bf16(ac * jax.nn.sigmoid(ac) * b)
    out_ref[0] = x + _dot(act_s[...], wdown_ref[...])


def _ffn(x, g_ffn, w_up, ffn_conv_w, ffn_conv_b, w_down):
    bsz, s, d = x.shape
    tm = FFN_TILE
    assert s % tm == 0
    return pl.pallas_call(
        _ffn_kernel,
        grid=(bsz, s // tm),
        in_specs=[
            pl.BlockSpec((1, tm, d), lambda b, t: (b, t, 0)),
            _resident((1, d)),
            _resident((d, 2 * D_FF)),
            _resident((3, D_FF)),
            _resident((1, D_FF)),
            _resident((D_FF, d)),
        ],
        out_specs=pl.BlockSpec((1, tm, d), lambda b, t: (b, t, 0)),
        out_shape=jax.ShapeDtypeStruct((bsz, s, d), jnp.float32),
        scratch_shapes=[
            pltpu.VMEM((SUBLANES + tm, D_FF), jnp.float32),
            pltpu.VMEM((tm, D_FF), jnp.bfloat16),
        ],
        compiler_params=pltpu.CompilerParams(
            dimension_semantics=("arbitrary", "arbitrary"),
            vmem_limit_bytes=VMEM_LIMIT_FFN),
        name="ffn",
    )(x, g_ffn.reshape(1, d), _bf16(w_up), ffn_conv_w, ffn_conv_b.reshape(1, D_FF), _bf16(w_down))


def kernel(x, mem, g_mix, w_in, b_gate, conv_w, conv_b, moba_q_gain, moba_k_gain, g_mem, w_mem_kv, memq_gain,
           memk_gain, w_br_conv, w_br_moba, w_br_mem, w_o, g_ffn, w_up, ffn_conv_w, ffn_conv_b, w_down):
    memk, memv = _memkv(mem, g_mem, w_mem_kv, memk_gain)
    x = _mixer(x, memk, memv, g_mix, w_in, b_gate, conv_w, conv_b, moba_q_gain, moba_k_gain, memq_gain,
               w_br_conv, w_br_moba, w_br_mem, w_o)
    return _ffn(x, g_ffn, w_up, ffn_conv_w, ffn_conv_b, w_down)
```

```python
import math

import jax
import jax.numpy as jnp
from jax import lax
from jax.experimental import pallas as pl
from jax.experimental.pallas import tpu as pltpu

D_MODEL = 1024
MEM_LEN = 256
CONV_CH = 512
MOBA_HEADS = 8
MOBA_HEAD_DIM = 64
MOBA_WIDTH = MOBA_HEADS * MOBA_HEAD_DIM
MOBA_BLOCK = 256
MOBA_TOPK = 3
MEM_HEADS = 4
MEM_HEAD_DIM = 128
MEM_WIDTH = MEM_HEADS * MEM_HEAD_DIM
D_FF = 2816
EPS = 1e-6

W_CONV = 0
W_Q = 3 * CONV_CH
W_K = W_Q + MOBA_WIDTH
W_V = W_K + MOBA_WIDTH
W_QMEM = W_V + MOBA_WIDTH
W_GATE = W_QMEM + MEM_WIDTH
IN_COLS = W_GATE + 3 * D_MODEL

LANES = 128
SUBLANES = 8
BF16_ROWS = 16
PAIR = 2 * MOBA_HEAD_DIM
VT_ROWS = MOBA_HEAD_DIM + BF16_ROWS
MASKED = -1e30
MIXER_SUBS = 2
VMEM_LIMIT_MIXER = 60 * 1024 * 1024
VMEM_LIMIT_FFN = 56 * 1024 * 1024
FFN_TILE = 512
FFN_SUBS = 2
FFN_CHUNKS = ((0, 1024), (1024, 1024), (2048, 768))

_NT = (((1,), (1,)), ((), ()))


def _dot(a, b):
    return jnp.dot(a, b, preferred_element_type=jnp.float32)


def _dot_nt(a, b):
    return lax.dot_general(a, b, _NT, preferred_element_type=jnp.float32)


def _rmsnorm(x, g):
    return x * lax.rsqrt(jnp.mean(x * x, axis=-1, keepdims=True) + EPS) * g


def _bf16(x):
    return x.astype(jnp.bfloat16)


def _memkv_kernel(mem_ref, g_ref, w_ref, kg_ref, k_ref, v_ref):
    mn = _rmsnorm(mem_ref[0], g_ref[...])
    kv = _dot(_bf16(mn), w_ref[...])
    for h in range(MEM_HEADS):
        lo = h * MEM_HEAD_DIM
        kh = _rmsnorm(kv[:, lo:lo + MEM_HEAD_DIM], kg_ref[...])
        k_ref[0, :, lo:lo + MEM_HEAD_DIM] = _bf16(kh)
    v_ref[0] = _bf16(kv[:, MEM_WIDTH:])


def _memkv(mem, g_mem, w_mem_kv, memk_gain):
    bsz, m, d = mem.shape
    const = lambda b: (0, 0)
    return pl.pallas_call(
        _memkv_kernel,
        grid=(bsz,),
        in_specs=[
            pl.BlockSpec((1, m, d), lambda b: (b, 0, 0)),
            pl.BlockSpec((1, d), const),
            pl.BlockSpec((d, 2 * MEM_WIDTH), const),
            pl.BlockSpec((1, MEM_HEAD_DIM), const),
        ],
        out_specs=[
            pl.BlockSpec((1, m, MEM_WIDTH), lambda b: (b, 0, 0)),
            pl.BlockSpec((1, m, MEM_WIDTH), lambda b: (b, 0, 0)),
        ],
        out_shape=[
            jax.ShapeDtypeStruct((bsz, m, MEM_WIDTH), jnp.bfloat16),
            jax.ShapeDtypeStruct((bsz, m, MEM_WIDTH), jnp.bfloat16),
        ],
        name="memkv",
    )(mem, g_mem.reshape(1, d), _bf16(w_mem_kv), memk_gain.reshape(1, MEM_HEAD_DIM))


def _mixer_kernel(x_ref, *refs):
    uext_s, kbar_s = refs[-8], refs[-9]

    @pl.when(pl.program_id(1) == 0)
    def _():
        uext_s[0:SUBLANES, :] = jnp.zeros((SUBLANES, CONV_CH), jnp.float32)
        kbar_s[...] = jnp.zeros(kbar_s.shape, jnp.float32)

    for sub in range(MIXER_SUBS):
        _mixer_block(pl.program_id(1) * MIXER_SUBS + sub, slice(sub * MOBA_BLOCK, (sub + 1) * MOBA_BLOCK), x_ref, *refs)


def _mixer_block(i, rows, x_ref, gmix_ref, wmain_ref, wqvt_ref, bgate_ref, convw_ref, convb_ref, qgt_ref, kg_ref,
                 gsum_ref, memk_ref, memv_ref, mqg_ref, wbc_ref, wbm_ref, wbx_ref, wo_ref,
                 out_ref,
                 k_s, vt_s, kbar_s, uext_s, rhs_s, sa_s, sb_s, ma_s, mb_s, acc_s, m_s):
    blk = MOBA_BLOCK
    nbr = kbar_s.shape[0]
    x = x_ref[0, rows, :]
    hb = _bf16(_rmsnorm(x, gmix_ref[...]))

    def proj(lo, width):
        return _dot(hb, wmain_ref[:, lo:lo + width])

    def gate(g, n):
        return jax.nn.sigmoid(g + bgate_ref[:, n * D_MODEL:(n + 1) * D_MODEL])

    pc = proj(W_CONV, 3 * CONV_CH)
    k_raw = proj(W_K, MOBA_WIDTH)
    qvt = _dot_nt(wqvt_ref[...], hb)
    qm = proj(W_QMEM, MEM_WIDTH)

    u = pc[:, CONV_CH:2 * CONV_CH] * pc[:, 2 * CONV_CH:]
    uext_s[SUBLANES:SUBLANES + blk, :] = u
    u1 = uext_s[SUBLANES - 1:SUBLANES - 1 + blk, :]
    u2 = uext_s[SUBLANES - 2:SUBLANES - 2 + blk, :]
    uext_s[0:SUBLANES, :] = u[blk - SUBLANES:, :]
    conv = u2 * convw_ref[0:1, :] + u1 * convw_ref[1:2, :] + u * convw_ref[2:3, :] + convb_ref[...]
    y_conv = _bf16(pc[:, :CONV_CH] * conv)

    k2_b = _bf16(k_raw * k_raw)

    q3 = qvt[:MOBA_WIDTH].reshape(MOBA_HEADS, MOBA_HEAD_DIM, blk)
    ssq = jnp.sum(q3 * q3, axis=1, keepdims=True)
    qbt = _bf16((q3 * lax.rsqrt(ssq * (1.0 / MOBA_HEAD_DIM) + EPS)).reshape(MOBA_WIDTH, blk) * qgt_ref[...])
    vbt = _bf16(qvt[MOBA_WIDTH:])

    qmb = [_bf16(_rmsnorm(qm[:, h * MEM_HEAD_DIM:(h + 1) * MEM_HEAD_DIM], mqg_ref[...])) for h in range(MEM_HEADS)]

    g0 = proj(W_GATE, D_MODEL)
    kss = _dot(k2_b, gsum_ref[...])
    p_conv = _dot(y_conv, wbc_ref[...])
    s_mem = [_dot_nt(qmb[h], memk_ref[0, :, h * MEM_HEAD_DIM:(h + 1) * MEM_HEAD_DIM]) for h in range(MEM_HEADS)]
    g2 = proj(W_GATE + 2 * D_MODEL, D_MODEL)

    merged = gate(g0, 0) * p_conv
    kn = k_raw * lax.rsqrt(kss * (1.0 / MOBA_HEAD_DIM) + EPS) * kg_ref[...]
    kb = _bf16(kn)
    k_s[i] = kb
    kbar_row = jnp.mean(kn, axis=0, keepdims=True)
    kb_rows = lax.broadcasted_iota(jnp.int32, kbar_s.shape, 0)
    kbar_prev = kbar_s[...]
    kbar_s[...] = jnp.where(kb_rows == i, kbar_row, kbar_prev)
    kbar_b = _bf16(kbar_prev)

    zeros_h = jnp.zeros((MOBA_HEAD_DIM, blk), jnp.bfloat16)
    qpairs = []
    for h in range(MOBA_HEADS):
        qh = qbt[h * MOBA_HEAD_DIM:(h + 1) * MOBA_HEAD_DIM]
        qpairs.append(jnp.concatenate([qh, zeros_h] if h % 2 == 0 else [zeros_h, qh], axis=0))

    p_mem = [jnp.exp(s - jnp.max(s, axis=-1, keepdims=True)) for s in s_mem]

    pair_cols = lambda h: slice((h // 2) * PAIR, (h // 2 + 1) * PAIR)
    gates = [_dot(kbar_b[:, pair_cols(h)], qpairs[h]) for h in range(MOBA_HEADS)]
    s_own = [_dot(kb[:, pair_cols(h)], qpairs[h]) for h in range(MOBA_HEADS)]
    o_mem = [_dot(_bf16(p_mem[h]), memv_ref[0, :, h * MEM_HEAD_DIM:(h + 1) * MEM_HEAD_DIM])
             / jnp.sum(p_mem[h], axis=-1, keepdims=True) for h in range(MEM_HEADS)]
    merged = merged + gate(g2, 2) * _dot(_bf16(jnp.concatenate(o_mem, axis=-1)), wbx_ref[...])

    lane = lax.broadcasted_iota(jnp.int32, (blk, LANES), 1)
    brow = lax.broadcasted_iota(jnp.int32, (nbr, blk), 0)
    browf = brow.astype(jnp.float32)
    neg_inf = jnp.float32(-jnp.inf)
    sel_pad = jnp.full((LANES - nbr, blk), MASKED, jnp.bfloat16)
    ones_r = jnp.ones((BF16_ROWS, blk), jnp.bfloat16)
    vts = []
    for h in range(MOBA_HEADS):
        g = jnp.where(brow < i, gates[h], neg_inf)
        sel = jnp.full((nbr, blk), MASKED, jnp.float32)
        for _ in range(MOBA_TOPK):
            mx = jnp.max(g, axis=0, keepdims=True)
            idx = jnp.min(jnp.where(g == mx, browf, float(nbr)), axis=0, keepdims=True)
            pick = browf == idx
            sel = jnp.where(pick & (mx > neg_inf), 0.0, sel)
            g = jnp.where(pick, neg_inf, g)
        rhs_s[h] = jnp.concatenate([qpairs[h], _bf16(sel), sel_pad], axis=0)
        vt = jnp.concatenate([vbt[h * MOBA_HEAD_DIM:(h + 1) * MOBA_HEAD_DIM], ones_r], axis=0)
        vt_s[i, h] = vt
        vts.append(vt)

    def pair_scores(dst, h, tp):
        dst_s, dst_mx = dst
        parts = []
        for jn in (2 * tp, 2 * tp + 1):
            en = jnp.where(lane == jnp.where(jn < i, jn, LANES - 1), 1.0, 0.0).astype(jnp.bfloat16)
            parts.append(jnp.concatenate([k_s[jnp.minimum(jn, i), :, pair_cols(h)], en], axis=1))
        s = _dot(jnp.concatenate(parts, axis=0), rhs_s[h])
        dst_s[h] = s
        dst_mx[h] = jnp.broadcast_to(jnp.max(s, axis=0, keepdims=True), (SUBLANES, blk))

    causal = (lax.broadcasted_iota(jnp.int32, (blk, blk), 0) <= lax.broadcasted_iota(jnp.int32, (blk, blk), 1))
    set_a, set_b = (sa_s, ma_s), (sb_s, mb_s)
    for h in range(MOBA_HEADS):
        pair_scores(set_a, h, 0)
        s = jnp.where(causal, s_own[h], MASKED)
        m0 = jnp.max(s, axis=0, keepdims=True)
        acc_s[h] = _dot(vts[h], jnp.exp2(_bf16(s - m0)))
        m_s[h] = jnp.broadcast_to(m0, (SUBLANES, blk))

    def attend(src, dst, tp):
        src_s, src_mx = src
        ja = jnp.minimum(2 * tp, i)
        jb = jnp.minimum(2 * tp + 1, i)
        for h in range(MOBA_HEADS):
            if dst is not None:
                pair_scores(dst, h, tp + 1)
            m_old = m_s[h]
            m_new = jnp.maximum(m_old, src_mx[h])
            p = jnp.exp2(_bf16(src_s[h] - m_new[0:1]))
            pv = _dot(jnp.concatenate([vt_s[ja, h], vt_s[jb, h]], axis=1), p)
            acc_s[h] = acc_s[h] * jnp.exp2(m_old[0:1] - m_new[0:1]) + pv
            m_s[h] = m_new

    n_pairs = (i + 1) // 2

    def two_pairs(u, carry):
        attend(set_a, set_b, 2 * u)
        attend(set_b, set_a, 2 * u + 1)
        return carry

    lax.fori_loop(0, n_pairs // 2, two_pairs, 0)

    @pl.when(n_pairs % 2 == 1)
    def _():
        attend(set_a, None, n_pairs - 1)

    g1 = proj(W_GATE + D_MODEL, D_MODEL)
    outs = []
    for h in range(MOBA_HEADS):
        a = acc_s[h]
        outs.append(a[:MOBA_HEAD_DIM] / a[MOBA_HEAD_DIM:MOBA_HEAD_DIM + 1])
    y_moba = jnp.concatenate(outs, axis=0).T
    merged = merged + gate(g1, 1) * _dot(_bf16(y_moba), wbm_ref[...])

    out_ref[0, rows, :] = x + _dot(_bf16(merged), wo_ref[...])


def _resident(shape):
    return pl.BlockSpec(shape, lambda b, i: (0,) * len(shape), pipeline_mode=pl.Buffered(1))


def _mixer(x, memk, memv, g_mix, w_in, b_gate, conv_w, conv_b, moba_q_gain, moba_k_gain, memq_gain,
           w_br_conv, w_br_moba, w_br_mem, w_o):
    bsz, s, d = x.shape
    blk = MOBA_BLOCK
    assert s % (MIXER_SUBS * blk) == 0 and d == D_MODEL
    nb = s // blk
    nbr = pl.cdiv(nb, BF16_ROWS) * BF16_ROWS
    assert nbr < LANES
    w_main = _bf16(w_in)
    w_qv = lax.optimization_barrier(jnp.concatenate([w_in[:, W_Q:W_K], w_in[:, W_V:W_QMEM]], axis=1))
    w_qvt = _bf16(w_qv.T)
    qgt = jnp.broadcast_to(
        (jnp.tile(moba_q_gain, MOBA_HEADS) * (MOBA_HEAD_DIM ** -0.5 * math.log2(math.e)))[:, None], (MOBA_WIDTH, blk))
    kg = jnp.tile(moba_k_gain, MOBA_HEADS).reshape(1, MOBA_WIDTH)
    mqg = memq_gain.reshape(1, MEM_HEAD_DIM) * (MEM_HEAD_DIM ** -0.5)
    head_of = jnp.arange(MOBA_WIDTH) // MOBA_HEAD_DIM
    gsum = (head_of[:, None] == head_of[None, :]).astype(jnp.bfloat16)
    return pl.pallas_call(
        _mixer_kernel,
        grid=(bsz, nb // MIXER_SUBS),
        in_specs=[
            pl.BlockSpec((1, MIXER_SUBS * blk, d), lambda b, i: (b, i, 0)),
            _resident((1, d)),
            _resident((d, IN_COLS)),
            _resident((2 * MOBA_WIDTH, d)),
            _resident((1, 3 * d)),
            _resident((3, CONV_CH)),
            _resident((1, CONV_CH)),
            _resident((MOBA_WIDTH, blk)),
            _resident((1, MOBA_WIDTH)),
            _resident((MOBA_WIDTH, MOBA_WIDTH)),
            pl.BlockSpec((1, MEM_LEN, MEM_WIDTH), lambda b, i: (b, 0, 0)),
            pl.BlockSpec((1, MEM_LEN, MEM_WIDTH), lambda b, i: (b, 0, 0)),
            _resident((1, MEM_HEAD_DIM)),
            _resident((CONV_CH, d)),
            _resident((MOBA_WIDTH, d)),
            _resident((MEM_WIDTH, d)),
            _resident((d, d)),
        ],
        out_specs=pl.BlockSpec((1, MIXER_SUBS * blk, d), lambda b, i: (b, i, 0)),
        out_shape=jax.ShapeDtypeStruct((bsz, s, d), jnp.float32),
        scratch_shapes=[
            pltpu.VMEM((nb, blk, MOBA_WIDTH), jnp.bfloat16),
            pltpu.VMEM((nb, MOBA_HEADS, VT_ROWS, blk), jnp.bfloat16),
            pltpu.VMEM((nbr, MOBA_WIDTH), jnp.float32),
            pltpu.VMEM((SUBLANES + blk, CONV_CH), jnp.float32),
            pltpu.VMEM((MOBA_HEADS, 2 * LANES, blk), jnp.bfloat16),
            pltpu.VMEM((MOBA_HEADS, 2 * blk, blk), jnp.float32),
            pltpu.VMEM((MOBA_HEADS, 2 * blk, blk), jnp.float32),
            pltpu.VMEM((MOBA_HEADS, SUBLANES, blk), jnp.float32),
            pltpu.VMEM((MOBA_HEADS, SUBLANES, blk), jnp.float32),
            pltpu.VMEM((MOBA_HEADS, VT_ROWS, blk), jnp.float32),
            pltpu.VMEM((MOBA_HEADS, SUBLANES, blk), jnp.float32),
        ],
        compiler_params=pltpu.CompilerParams(
            dimension_semantics=("arbitrary", "arbitrary"),
            vmem_limit_bytes=VMEM_LIMIT_MIXER),
        name="mixer",
    )(x, g_mix.reshape(1, d), w_main, w_qvt, b_gate.reshape(1, 3 * d), conv_w, conv_b.reshape(1, CONV_CH),
      qgt, kg, gsum, memk, memv, mqg, _bf16(w_br_conv), _bf16(w_br_moba), _bf16(w_br_mem), _bf16(w_o))


def _ffn_kernel(x_ref, g_ref, wup_ref, cw_ref, cb_ref, wdown_ref, out_ref, aext_s, act_s):
    tm = FFN_TILE

    @pl.when(pl.program_id(1) == 0)
    def _():
        aext_s[0:SUBLANES, :] = jnp.zeros((SUBLANES, D_FF), jnp.float32)

    for sub in range(FFN_SUBS):
        hb = _bf16(_rmsnorm(x_ref[0, sub * tm:(sub + 1) * tm, :], g_ref[...]))
        for c0, cw in FFN_CHUNKS:
            cs = slice(c0, c0 + cw)
            a = _dot(hb, wup_ref[:, c0:c0 + cw])
            b = _dot(hb, wup_ref[:, D_FF + c0:D_FF + c0 + cw])
            aext_s[SUBLANES:SUBLANES + tm, cs] = a
            a1 = aext_s[SUBLANES - 1:SUBLANES - 1 + tm, cs]
            a2 = aext_s[SUBLANES - 2:SUBLANES - 2 + tm, cs]
            aext_s[0:SUBLANES, cs] = a[tm - SUBLANES:, :]
            ac = a2 * cw_ref[0:1, cs] + a1 * cw_ref[1:2, cs] + a * cw_ref[2:3, cs] + cb_ref[:, cs]
            act_s[sub, :, cs] = _bf16(ac * jax.nn.sigmoid(ac) * b)
    for sub in range(FFN_SUBS):
        rows = slice(sub * tm, (sub + 1) * tm)
        out_ref[0, rows, :] = x_ref[0, rows, :] + _dot(act_s[sub], wdown_ref[...])


def _ffn(x, g_ffn, w_up, ffn_conv_w, ffn_conv_b, w_down):
    bsz, s, d = x.shape
    tm = FFN_TILE
    win = FFN_SUBS * tm
    assert s % win == 0
    return pl.pallas_call(
        _ffn_kernel,
        grid=(bsz, s // win),
        in_specs=[
            pl.BlockSpec((1, win, d), lambda b, t: (b, t, 0)),
            _resident((1, d)),
            _resident((d, 2 * D_FF)),
            _resident((3, D_FF)),
            _resident((1, D_FF)),
            _resident((D_FF, d)),
        ],
        out_specs=pl.BlockSpec((1, win, d), lambda b, t: (b, t, 0)),
        out_shape=jax.ShapeDtypeStruct((bsz, s, d), jnp.float32),
        scratch_shapes=[
            pltpu.VMEM((SUBLANES + tm, D_FF), jnp.float32),
            pltpu.VMEM((FFN_SUBS, tm, D_FF), jnp.bfloat16),
        ],
        compiler_params=pltpu.CompilerParams(
            dimension_semantics=("arbitrary", "arbitrary"),
            vmem_limit_bytes=VMEM_LIMIT_FFN),
        name="ffn",
    )(x, g_ffn.reshape(1, d), _bf16(w_up), ffn_conv_w, ffn_conv_b.reshape(1, D_FF), _bf16(w_down))


def kernel(x, mem, g_mix, w_in, b_gate, conv_w, conv_b, moba_q_gain, moba_k_gain, g_mem, w_mem_kv, memq_gain,
           memk_gain, w_br_conv, w_br_moba, w_br_mem, w_o, g_ffn, w_up, ffn_conv_w, ffn_conv_b, w_down):
    memk, memv = _memkv(mem, g_mem, w_mem_kv, memk_gain)
    x = _mixer(x, memk, memv, g_mix, w_in, b_gate, conv_w, conv_b, moba_q_gain, moba_k_gain, memq_gain,
               w_br_conv, w_br_moba, w_br_mem, w_o)
    return _ffn(x, g_ffn, w_up, ffn_conv_w, ffn_conv_b, w_down)
```

```python
import math

import jax
import jax.numpy as jnp
from jax import lax
from jax.experimental import pallas as pl
from jax.experimental.pallas import tpu as pltpu

D_MODEL = 1024
MEM_LEN = 256
CONV_CH = 512
MOBA_HEADS = 8
MOBA_HEAD_DIM = 64
MOBA_WIDTH = MOBA_HEADS * MOBA_HEAD_DIM
MOBA_BLOCK = 256
MOBA_TOPK = 3
MEM_HEADS = 4
MEM_HEAD_DIM = 128
MEM_WIDTH = MEM_HEADS * MEM_HEAD_DIM
D_FF = 2816
EPS = 1e-6

W_CONV = 0
W_Q = 3 * CONV_CH
W_K = W_Q + MOBA_WIDTH
W_V = W_K + MOBA_WIDTH
W_QMEM = W_V + MOBA_WIDTH
W_GATE = W_QMEM + MEM_WIDTH
IN_COLS = W_GATE + 3 * D_MODEL

LANES = 128
SUBLANES = 8
BF16_ROWS = 16
PAIR = 2 * MOBA_HEAD_DIM
VT_ROWS = MOBA_HEAD_DIM + BF16_ROWS
MASKED = -1e30
MIXER_SUBS = 2
VMEM_LIMIT_MIXER = 60 * 1024 * 1024
VMEM_LIMIT_FFN = 48 * 1024 * 1024
FFN_TILE = 512
FFN_CHUNKS = ((0, 1024), (1024, 1024), (2048, 768))

_NT = (((1,), (1,)), ((), ()))


def _dot(a, b):
    return jnp.dot(a, b, preferred_element_type=jnp.float32)


def _dot_nt(a, b):
    return lax.dot_general(a, b, _NT, preferred_element_type=jnp.float32)


def _rmsnorm(x, g):
    return x * lax.rsqrt(jnp.mean(x * x, axis=-1, keepdims=True) + EPS) * g


def _bf16(x):
    return x.astype(jnp.bfloat16)


def _memkv_kernel(mem_ref, g_ref, w_ref, kg_ref, k_ref, v_ref):
    mn = _rmsnorm(mem_ref[0], g_ref[...])
    kv = _dot(_bf16(mn), w_ref[...])
    for h in range(MEM_HEADS):
        lo = h * MEM_HEAD_DIM
        kh = _rmsnorm(kv[:, lo:lo + MEM_HEAD_DIM], kg_ref[...])
        k_ref[0, :, lo:lo + MEM_HEAD_DIM] = _bf16(kh)
    v_ref[0] = _bf16(kv[:, MEM_WIDTH:])


def _memkv(mem, g_mem, w_mem_kv, memk_gain):
    bsz, m, d = mem.shape
    const = lambda b: (0, 0)
    return pl.pallas_call(
        _memkv_kernel,
        grid=(bsz,),
        in_specs=[
            pl.BlockSpec((1, m, d), lambda b: (b, 0, 0)),
            pl.BlockSpec((1, d), const),
            pl.BlockSpec((d, 2 * MEM_WIDTH), const),
            pl.BlockSpec((1, MEM_HEAD_DIM), const),
        ],
        out_specs=[
            pl.BlockSpec((1, m, MEM_WIDTH), lambda b: (b, 0, 0)),
            pl.BlockSpec((1, m, MEM_WIDTH), lambda b: (b, 0, 0)),
        ],
        out_shape=[
            jax.ShapeDtypeStruct((bsz, m, MEM_WIDTH), jnp.bfloat16),
            jax.ShapeDtypeStruct((bsz, m, MEM_WIDTH), jnp.bfloat16),
        ],
        name="memkv",
    )(mem, g_mem.reshape(1, d), _bf16(w_mem_kv), memk_gain.reshape(1, MEM_HEAD_DIM))


def _mixer_kernel(x_ref, gmix_ref, wmain_ref, wqvt_ref, bgate_ref, convw_ref, convb_ref, qgt_ref, kg_ref, gsum_ref,
                  memk_ref, memv_ref, mqg_ref, wbc_ref, wbm_ref, wbx_ref, wo_ref,
                  out_ref,
                  k_s, vt_s, kbar_s, uext_s, rhs_s, sa_s, sb_s, ma_s, mb_s, acc_s, m_s):
    @pl.when(pl.program_id(1) == 0)
    def _():
        uext_s[0:SUBLANES, :] = jnp.zeros((SUBLANES, CONV_CH), jnp.float32)
        kbar_s[...] = jnp.zeros(kbar_s.shape, jnp.float32)

    for sub in range(MIXER_SUBS):
        _mixer_block(pl.program_id(1) * MIXER_SUBS + sub, slice(sub * MOBA_BLOCK, (sub + 1) * MOBA_BLOCK),
                     x_ref, gmix_ref, wmain_ref, wqvt_ref, bgate_ref, convw_ref, convb_ref, qgt_ref, kg_ref, gsum_ref,
                     memk_ref, memv_ref, mqg_ref, wbc_ref, wbm_ref, wbx_ref, wo_ref,
                     out_ref,
                     k_s, vt_s, kbar_s, uext_s, rhs_s, sa_s, sb_s, ma_s, mb_s, acc_s, m_s)


def _mixer_block(i, rows, x_ref, gmix_ref, wmain_ref, wqvt_ref, bgate_ref, convw_ref, convb_ref, qgt_ref, kg_ref,
                 gsum_ref, memk_ref, memv_ref, mqg_ref, wbc_ref, wbm_ref, wbx_ref, wo_ref,
                 out_ref,
                 k_s, vt_s, kbar_s, uext_s, rhs_s, sa_s, sb_s, ma_s, mb_s, acc_s, m_s):
    blk = MOBA_BLOCK
    nbr = kbar_s.shape[0]
    x = x_ref[0, rows, :]
    hb = _bf16(_rmsnorm(x, gmix_ref[...]))

    def proj(lo, width):
        return _dot(hb, wmain_ref[:, lo:lo + width])

    def gate(g, n):
        return jax.nn.sigmoid(g + bgate_ref[:, n * D_MODEL:(n + 1) * D_MODEL])

    pc = proj(W_CONV, 3 * CONV_CH)
    k_raw = proj(W_K, MOBA_WIDTH)
    qvt = _dot_nt(wqvt_ref[...], hb)
    qm = proj(W_QMEM, MEM_WIDTH)

    u = pc[:, CONV_CH:2 * CONV_CH] * pc[:, 2 * CONV_CH:]
    uext_s[SUBLANES:SUBLANES + blk, :] = u
    u1 = uext_s[SUBLANES - 1:SUBLANES - 1 + blk, :]
    u2 = uext_s[SUBLANES - 2:SUBLANES - 2 + blk, :]
    uext_s[0:SUBLANES, :] = u[blk - SUBLANES:, :]
    conv = u2 * convw_ref[0:1, :] + u1 * convw_ref[1:2, :] + u * convw_ref[2:3, :] + convb_ref[...]
    y_conv = _bf16(pc[:, :CONV_CH] * conv)

    k2_b = _bf16(k_raw * k_raw)

    q3 = qvt[:MOBA_WIDTH].reshape(MOBA_HEADS, MOBA_HEAD_DIM, blk)
    ssq = jnp.sum(q3 * q3, axis=1, keepdims=True)
    qbt = _bf16((q3 * lax.rsqrt(ssq * (1.0 / MOBA_HEAD_DIM) + EPS)).reshape(MOBA_WIDTH, blk) * qgt_ref[...])
    vbt = _bf16(qvt[MOBA_WIDTH:])

    qmb = [_bf16(_rmsnorm(qm[:, h * MEM_HEAD_DIM:(h + 1) * MEM_HEAD_DIM], mqg_ref[...])) for h in range(MEM_HEADS)]

    g0 = proj(W_GATE, D_MODEL)
    kss = _dot(k2_b, gsum_ref[...])
    p_conv = _dot(y_conv, wbc_ref[...])
    s_mem = [_dot_nt(qmb[h], memk_ref[0, :, h * MEM_HEAD_DIM:(h + 1) * MEM_HEAD_DIM]) for h in range(MEM_HEADS)]
    g2 = proj(W_GATE + 2 * D_MODEL, D_MODEL)

    merged = gate(g0, 0) * p_conv
    kn = k_raw * lax.rsqrt(kss * (1.0 / MOBA_HEAD_DIM) + EPS) * kg_ref[...]
    kb = _bf16(kn)
    k_s[i] = kb
    kbar_row = jnp.mean(kn, axis=0, keepdims=True)
    kb_rows = lax.broadcasted_iota(jnp.int32, kbar_s.shape, 0)
    kbar_prev = kbar_s[...]
    kbar_s[...] = jnp.where(kb_rows == i, kbar_row, kbar_prev)
    kbar_b = _bf16(kbar_prev)

    zeros_h = jnp.zeros((MOBA_HEAD_DIM, blk), jnp.bfloat16)
    qpairs = []
    for h in range(MOBA_HEADS):
        qh = qbt[h * MOBA_HEAD_DIM:(h + 1) * MOBA_HEAD_DIM]
        qpairs.append(jnp.concatenate([qh, zeros_h] if h % 2 == 0 else [zeros_h, qh], axis=0))

    p_mem = [jnp.exp(s - jnp.max(s, axis=-1, keepdims=True)) for s in s_mem]

    pair_cols = lambda h: slice((h // 2) * PAIR, (h // 2 + 1) * PAIR)
    gates = [_dot(kbar_b[:, pair_cols(h)], qpairs[h]) for h in range(MOBA_HEADS)]
    s_own = [_dot(kb[:, pair_cols(h)], qpairs[h]) for h in range(MOBA_HEADS)]
    o_mem = [_dot(_bf16(p_mem[h]), memv_ref[0, :, h * MEM_HEAD_DIM:(h + 1) * MEM_HEAD_DIM])
             / jnp.sum(p_mem[h], axis=-1, keepdims=True) for h in range(MEM_HEADS)]
    merged = merged + gate(g2, 2) * _dot(_bf16(jnp.concatenate(o_mem, axis=-1)), wbx_ref[...])

    lane = lax.broadcasted_iota(jnp.int32, (blk, LANES), 1)
    brow = lax.broadcasted_iota(jnp.int32, (nbr, blk), 0)
    browf = brow.astype(jnp.float32)
    neg_inf = jnp.float32(-jnp.inf)
    sel_pad = jnp.full((LANES - nbr, blk), MASKED, jnp.bfloat16)
    ones_r = jnp.ones((BF16_ROWS, blk), jnp.bfloat16)
    vts = []
    for h in range(MOBA_HEADS):
        g = jnp.where(brow < i, gates[h], neg_inf)
        sel = jnp.full((nbr, blk), MASKED, jnp.float32)
        for _ in range(MOBA_TOPK):
            mx = jnp.max(g, axis=0, keepdims=True)
            idx = jnp.min(jnp.where(g == mx, browf, float(nbr)), axis=0, keepdims=True)
            pick = browf == idx
            sel = jnp.where(pick & (mx > neg_inf), 0.0, sel)
            g = jnp.where(pick, neg_inf, g)
        rhs_s[h] = jnp.concatenate([qpairs[h], _bf16(sel), sel_pad], axis=0)
        vt = jnp.concatenate([vbt[h * MOBA_HEAD_DIM:(h + 1) * MOBA_HEAD_DIM], ones_r], axis=0)
        vt_s[i, h] = vt
        vts.append(vt)

    def pair_scores(dst, h, tp):
        dst_s, dst_mx = dst
        parts = []
        for jn in (2 * tp, 2 * tp + 1):
            en = jnp.where(lane == jnp.where(jn < i, jn, LANES - 1), 1.0, 0.0).astype(jnp.bfloat16)
            parts.append(jnp.concatenate([k_s[jnp.minimum(jn, i), :, pair_cols(h)], en], axis=1))
        s = _dot(jnp.concatenate(parts, axis=0), rhs_s[h])
        dst_s[h] = s
        dst_mx[h] = jnp.broadcast_to(jnp.max(s, axis=0, keepdims=True), (SUBLANES, blk))

    causal = (lax.broadcasted_iota(jnp.int32, (blk, blk), 0) <= lax.broadcasted_iota(jnp.int32, (blk, blk), 1))
    set_a, set_b = (sa_s, ma_s), (sb_s, mb_s)
    for h in range(MOBA_HEADS):
        pair_scores(set_a, h, 0)
        s = jnp.where(causal, s_own[h], MASKED)
        m0 = jnp.max(s, axis=0, keepdims=True)
        acc_s[h] = _dot(vts[h], jnp.exp2(_bf16(s - m0)))
        m_s[h] = jnp.broadcast_to(m0, (SUBLANES, blk))

    def attend(src, dst, tp):
        src_s, src_mx = src
        ja = jnp.minimum(2 * tp, i)
        jb = jnp.minimum(2 * tp + 1, i)
        for h in range(MOBA_HEADS):
            if dst is not None:
                pair_scores(dst, h, tp + 1)
            m_old = m_s[h]
            m_new = jnp.maximum(m_old, src_mx[h])
            p = jnp.exp2(_bf16(src_s[h] - m_new[0:1]))
            pv = _dot(jnp.concatenate([vt_s[ja, h], vt_s[jb, h]], axis=1), p)
            acc_s[h] = acc_s[h] * jnp.exp2(m_old[0:1] - m_new[0:1]) + pv
            m_s[h] = m_new

    n_pairs = (i + 1) // 2

    def two_pairs(u, carry):
        attend(set_a, set_b, 2 * u)
        attend(set_b, set_a, 2 * u + 1)
        return carry

    lax.fori_loop(0, n_pairs // 2, two_pairs, 0)

    @pl.when(n_pairs % 2 == 1)
    def _():
        attend(set_a, None, n_pairs - 1)

    g1 = proj(W_GATE + D_MODEL, D_MODEL)
    outs = []
    for h in range(MOBA_HEADS):
        a = acc_s[h]
        outs.append(a[:MOBA_HEAD_DIM] / a[MOBA_HEAD_DIM:MOBA_HEAD_DIM + 1])
    y_moba = jnp.concatenate(outs, axis=0).T
    merged = merged + gate(g1, 1) * _dot(_bf16(y_moba), wbm_ref[...])

    out_ref[0, rows, :] = x + _dot(_bf16(merged), wo_ref[...])


def _resident(shape):
    return pl.BlockSpec(shape, lambda b, i: (0,) * len(shape), pipeline_mode=pl.Buffered(1))


def _mixer(x, memk, memv, g_mix, w_in, b_gate, conv_w, conv_b, moba_q_gain, moba_k_gain, memq_gain,
           w_br_conv, w_br_moba, w_br_mem, w_o):
    bsz, s, d = x.shape
    blk = MOBA_BLOCK
    assert s % (MIXER_SUBS * blk) == 0 and d == D_MODEL
    nb = s // blk
    nbr = pl.cdiv(nb, BF16_ROWS) * BF16_ROWS
    assert nbr < LANES
    w_main = _bf16(w_in)
    w_qv = lax.optimization_barrier(jnp.concatenate([w_in[:, W_Q:W_K], w_in[:, W_V:W_QMEM]], axis=1))
    w_qvt = _bf16(w_qv.T)
    qgt = jnp.broadcast_to(
        (jnp.tile(moba_q_gain, MOBA_HEADS) * (MOBA_HEAD_DIM ** -0.5 * math.log2(math.e)))[:, None], (MOBA_WIDTH, blk))
    kg = jnp.tile(moba_k_gain, MOBA_HEADS).reshape(1, MOBA_WIDTH)
    mqg = memq_gain.reshape(1, MEM_HEAD_DIM) * (MEM_HEAD_DIM ** -0.5)
    head_of = jnp.arange(MOBA_WIDTH) // MOBA_HEAD_DIM
    gsum = (head_of[:, None] == head_of[None, :]).astype(jnp.bfloat16)
    return pl.pallas_call(
        _mixer_kernel,
        grid=(bsz, nb // MIXER_SUBS),
        in_specs=[
            pl.BlockSpec((1, MIXER_SUBS * blk, d), lambda b, i: (b, i, 0)),
            _resident((1, d)),
            _resident((d, IN_COLS)),
            _resident((2 * MOBA_WIDTH, d)),
            _resident((1, 3 * d)),
            _resident((3, CONV_CH)),
            _resident((1, CONV_CH)),
            _resident((MOBA_WIDTH, blk)),
            _resident((1, MOBA_WIDTH)),
            _resident((MOBA_WIDTH, MOBA_WIDTH)),
            pl.BlockSpec((1, MEM_LEN, MEM_WIDTH), lambda b, i: (b, 0, 0)),
            pl.BlockSpec((1, MEM_LEN, MEM_WIDTH), lambda b, i: (b, 0, 0)),
            _resident((1, MEM_HEAD_DIM)),
            _resident((CONV_CH, d)),
            _resident((MOBA_WIDTH, d)),
            _resident((MEM_WIDTH, d)),
            _resident((d, d)),
        ],
        out_specs=pl.BlockSpec((1, MIXER_SUBS * blk, d), lambda b, i: (b, i, 0)),
        out_shape=jax.ShapeDtypeStruct((bsz, s, d), jnp.float32),
        scratch_shapes=[
            pltpu.VMEM((nb, blk, MOBA_WIDTH), jnp.bfloat16),
            pltpu.VMEM((nb, MOBA_HEADS, VT_ROWS, blk), jnp.bfloat16),
            pltpu.VMEM((nbr, MOBA_WIDTH), jnp.float32),
            pltpu.VMEM((SUBLANES + blk, CONV_CH), jnp.float32),
            pltpu.VMEM((MOBA_HEADS, 2 * LANES, blk), jnp.bfloat16),
            pltpu.VMEM((MOBA_HEADS, 2 * blk, blk), jnp.float32),
            pltpu.VMEM((MOBA_HEADS, 2 * blk, blk), jnp.float32),
            pltpu.VMEM((MOBA_HEADS, SUBLANES, blk), jnp.float32),
            pltpu.VMEM((MOBA_HEADS, SUBLANES, blk), jnp.float32),
            pltpu.VMEM((MOBA_HEADS, VT_ROWS, blk), jnp.float32),
            pltpu.VMEM((MOBA_HEADS, SUBLANES, blk), jnp.float32),
        ],
        compiler_params=pltpu.CompilerParams(
            dimension_semantics=("arbitrary", "arbitrary"),
            vmem_limit_bytes=VMEM_LIMIT_MIXER),
        name="mixer",
    )(x, g_mix.reshape(1, d), w_main, w_qvt, b_gate.reshape(1, 3 * d), conv_w, conv_b.reshape(1, CONV_CH),
      qgt, kg, gsum, memk, memv, mqg, _bf16(w_br_conv), _bf16(w_br_moba), _bf16(w_br_mem), _bf16(w_o))


def _ffn_kernel(x_ref, g_ref, wup_ref, cw_ref, cb_ref, wdown_ref, out_ref, aext_s, act_s):
    t = pl.program_id(1)
    tm = FFN_TILE
    x = x_ref[0]
    hb = _bf16(_rmsnorm(x, g_ref[...]))

    @pl.when(t == 0)
    def _():
        aext_s[0:SUBLANES, :] = jnp.zeros((SUBLANES, D_FF), jnp.float32)

    for c0, cw in FFN_CHUNKS:
        cs = slice(c0, c0 + cw)
        a = _dot(hb, wup_ref[:, c0:c0 + cw])
        b = _dot(hb, wup_ref[:, D_FF + c0:D_FF + c0 + cw])
        aext_s[SUBLANES:SUBLANES + tm, cs] = a
        a1 = aext_s[SUBLANES - 1:SUBLANES - 1 + tm, cs]
        a2 = aext_s[SUBLANES - 2:SUBLANES - 2 + tm, cs]
        aext_s[0:SUBLANES, cs] = a[tm - SUBLANES:, :]
        ac = a2 * cw_ref[0:1, cs] + a1 * cw_ref[1:2, cs] + a * cw_ref[2:3, cs] + cb_ref[:, cs]
        act_s[:, cs] = _bf16(ac * jax.nn.sigmoid(ac) * b)
    out_ref[0] = x + _dot(act_s[...], wdown_ref[...])


def _ffn(x, g_ffn, w_up, ffn_conv_w, ffn_conv_b, w_down):
    bsz, s, d = x.shape
    tm = FFN_TILE
    assert s % tm == 0
    return pl.pallas_call(
        _ffn_kernel,
        grid=(bsz, s // tm),
        in_specs=[
            pl.BlockSpec((1, tm, d), lambda b, t: (b, t, 0)),
            _resident((1, d)),
            _resident((d, 2 * D_FF)),
            _resident((3, D_FF)),
            _resident((1, D_FF)),
            _resident((D_FF, d)),
        ],
        out_specs=pl.BlockSpec((1, tm, d), lambda b, t: (b, t, 0)),
        out_shape=jax.ShapeDtypeStruct((bsz, s, d), jnp.float32),
        scratch_shapes=[
            pltpu.VMEM((SUBLANES + tm, D_FF), jnp.float32),
            pltpu.VMEM((tm, D_FF), jnp.bfloat16),
        ],
        compiler_params=pltpu.CompilerParams(
            dimension_semantics=("arbitrary", "arbitrary"),
            vmem_limit_bytes=VMEM_LIMIT_FFN),
        name="ffn",
    )(x, g_ffn.reshape(1, d), _bf16(w_up), ffn_conv_w, ffn_conv_b.reshape(1, D_FF), _bf16(w_down))


def kernel(x, mem, g_mix, w_in, b_gate, conv_w, conv_b, moba_q_gain, moba_k_gain, g_mem, w_mem_kv, memq_gain,
           memk_gain, w_br_conv, w_br_moba, w_br_mem, w_o, g_ffn, w_up, ffn_conv_w, ffn_conv_b, w_down):
    memk, memv = _memkv(mem, g_mem, w_mem_kv, memk_gain)
    x = _mixer(x, memk, memv, g_mix, w_in, b_gate, conv_w, conv_b, moba_q_gain, moba_k_gain, memq_gain,
               w_br_conv, w_br_moba, w_br_mem, w_o)
    return _ffn(x, g_ffn, w_up, ffn_conv_w, ffn_conv_b, w_down)
```

```python
import math

import jax
import jax.numpy as jnp
from jax import lax
from jax.experimental import pallas as pl
from jax.experimental.pallas import tpu as pltpu

D_MODEL = 1024
MEM_LEN = 256
CONV_CH = 512
MOBA_HEADS = 8
MOBA_HEAD_DIM = 64
MOBA_WIDTH = MOBA_HEADS * MOBA_HEAD_DIM
MOBA_BLOCK = 256
MOBA_TOPK = 3
MEM_HEADS = 4
MEM_HEAD_DIM = 128
MEM_WIDTH = MEM_HEADS * MEM_HEAD_DIM
D_FF = 2816
EPS = 1e-6

W_CONV = 0
W_Q = 3 * CONV_CH
W_K = W_Q + MOBA_WIDTH
W_V = W_K + MOBA_WIDTH
W_QMEM = W_V + MOBA_WIDTH
W_GATE = W_QMEM + MEM_WIDTH
IN_COLS = W_GATE + 3 * D_MODEL

LANES = 128
SUBLANES = 8
BF16_ROWS = 16
PAIR = 2 * MOBA_HEAD_DIM
VT_ROWS = MOBA_HEAD_DIM + BF16_ROWS
MASKED = -1e30
VEC_GMIX, VEC_BGATE, VEC_CONV01, VEC_CONV2B, VEC_GAINS, VEC_ROWS = 0, 1, 4, 5, 6, 16
MIXER_SUBS = 2
VMEM_LIMIT_MIXER = 60 * 1024 * 1024
VMEM_LIMIT_FFN = 48 * 1024 * 1024
FFN_TILE = 512
FFN_CHUNKS = ((0, 1024), (1024, 1024), (2048, 768))

_NT = (((1,), (1,)), ((), ()))


def _dot(a, b):
    return jnp.dot(a, b, preferred_element_type=jnp.float32)


def _dot_nt(a, b):
    return lax.dot_general(a, b, _NT, preferred_element_type=jnp.float32)


def _rmsnorm(x, g):
    return x * lax.rsqrt(jnp.mean(x * x, axis=-1, keepdims=True) + EPS) * g


def _bf16(x):
    return x.astype(jnp.bfloat16)


def _memkv_kernel(mem_ref, g_ref, w_ref, kg_ref, k_ref, v_ref):
    mn = _rmsnorm(mem_ref[0], g_ref[...])
    kv = _dot(_bf16(mn), w_ref[...])
    for h in range(MEM_HEADS):
        lo = h * MEM_HEAD_DIM
        kh = _rmsnorm(kv[:, lo:lo + MEM_HEAD_DIM], kg_ref[...])
        k_ref[0, :, lo:lo + MEM_HEAD_DIM] = _bf16(kh)
    v_ref[0] = _bf16(kv[:, MEM_WIDTH:])


def _memkv(mem, g_mem, w_mem_kv, memk_gain):
    bsz, m, d = mem.shape
    const = lambda b: (0, 0)
    return pl.pallas_call(
        _memkv_kernel,
        grid=(bsz,),
        in_specs=[
            pl.BlockSpec((1, m, d), lambda b: (b, 0, 0)),
            pl.BlockSpec((1, d), const),
            pl.BlockSpec((d, 2 * MEM_WIDTH), const),
            pl.BlockSpec((1, MEM_HEAD_DIM), const),
        ],
        out_specs=[
            pl.BlockSpec((1, m, MEM_WIDTH), lambda b: (b, 0, 0)),
            pl.BlockSpec((1, m, MEM_WIDTH), lambda b: (b, 0, 0)),
        ],
        out_shape=[
            jax.ShapeDtypeStruct((bsz, m, MEM_WIDTH), jnp.bfloat16),
            jax.ShapeDtypeStruct((bsz, m, MEM_WIDTH), jnp.bfloat16),
        ],
        name="memkv",
    )(mem, g_mem.reshape(1, d), _bf16(w_mem_kv), memk_gain.reshape(1, MEM_HEAD_DIM))


def _mixer_kernel(x_ref, vec_ref, wmain_ref, wqvt_ref, qgt_ref, gsum_ref,
                  memk_ref, memv_ref, wbc_ref, wbm_ref, wbx_ref, wo_ref,
                  out_ref,
                  k_s, vt_s, kbar_s, uext_s, rhs_s, sa_s, sb_s, ma_s, mb_s, acc_s, m_s):
    @pl.when(pl.program_id(1) == 0)
    def _():
        uext_s[0:SUBLANES, :] = jnp.zeros((SUBLANES, CONV_CH), jnp.float32)
        kbar_s[...] = jnp.zeros(kbar_s.shape, jnp.float32)

    for sub in range(MIXER_SUBS):
        _mixer_block(pl.program_id(1) * MIXER_SUBS + sub, slice(sub * MOBA_BLOCK, (sub + 1) * MOBA_BLOCK),
                     x_ref, vec_ref, wmain_ref, wqvt_ref, qgt_ref, gsum_ref,
                     memk_ref, memv_ref, wbc_ref, wbm_ref, wbx_ref, wo_ref,
                     out_ref,
                     k_s, vt_s, kbar_s, uext_s, rhs_s, sa_s, sb_s, ma_s, mb_s, acc_s, m_s)


def _mixer_block(i, rows, x_ref, vec_ref, wmain_ref, wqvt_ref, qgt_ref, gsum_ref,
                 memk_ref, memv_ref, wbc_ref, wbm_ref, wbx_ref, wo_ref,
                 out_ref,
                 k_s, vt_s, kbar_s, uext_s, rhs_s, sa_s, sb_s, ma_s, mb_s, acc_s, m_s):
    blk = MOBA_BLOCK
    nbr = kbar_s.shape[0]
    x = x_ref[0, rows, :]
    hb = _bf16(_rmsnorm(x, vec_ref[VEC_GMIX:VEC_GMIX + 1, :]))

    def proj(lo, width):
        return _dot(hb, wmain_ref[:, lo:lo + width])

    def gate(g, n):
        return jax.nn.sigmoid(g + vec_ref[VEC_BGATE + n:VEC_BGATE + n + 1, :])

    pc = proj(W_CONV, 3 * CONV_CH)
    k_raw = proj(W_K, MOBA_WIDTH)
    qvt = _dot_nt(wqvt_ref[...], hb)
    qm = proj(W_QMEM, MEM_WIDTH)

    u = pc[:, CONV_CH:2 * CONV_CH] * pc[:, 2 * CONV_CH:]
    uext_s[SUBLANES:SUBLANES + blk, :] = u
    u1 = uext_s[SUBLANES - 1:SUBLANES - 1 + blk, :]
    u2 = uext_s[SUBLANES - 2:SUBLANES - 2 + blk, :]
    uext_s[0:SUBLANES, :] = u[blk - SUBLANES:, :]
    conv = (u2 * vec_ref[VEC_CONV01:VEC_CONV01 + 1, :CONV_CH] + u1 * vec_ref[VEC_CONV01:VEC_CONV01 + 1, CONV_CH:]
            + u * vec_ref[VEC_CONV2B:VEC_CONV2B + 1, :CONV_CH] + vec_ref[VEC_CONV2B:VEC_CONV2B + 1, CONV_CH:])
    y_conv = _bf16(pc[:, :CONV_CH] * conv)

    k2_b = _bf16(k_raw * k_raw)

    q3 = qvt[:MOBA_WIDTH].reshape(MOBA_HEADS, MOBA_HEAD_DIM, blk)
    ssq = jnp.sum(q3 * q3, axis=1, keepdims=True)
    qbt = _bf16((q3 * lax.rsqrt(ssq * (1.0 / MOBA_HEAD_DIM) + EPS)).reshape(MOBA_WIDTH, blk) * qgt_ref[...])
    vbt = _bf16(qvt[MOBA_WIDTH:])

    mqg = vec_ref[VEC_GAINS:VEC_GAINS + 1, MOBA_WIDTH:MOBA_WIDTH + MEM_HEAD_DIM]
    qmb = [_bf16(_rmsnorm(qm[:, h * MEM_HEAD_DIM:(h + 1) * MEM_HEAD_DIM], mqg)) for h in range(MEM_HEADS)]

    g0 = proj(W_GATE, D_MODEL)
    kss = _dot(k2_b, gsum_ref[...])
    p_conv = _dot(y_conv, wbc_ref[...])
    s_mem = [_dot_nt(qmb[h], memk_ref[0, :, h * MEM_HEAD_DIM:(h + 1) * MEM_HEAD_DIM]) for h in range(MEM_HEADS)]
    g2 = proj(W_GATE + 2 * D_MODEL, D_MODEL)

    merged = gate(g0, 0) * p_conv
    kn = k_raw * lax.rsqrt(kss * (1.0 / MOBA_HEAD_DIM) + EPS) * vec_ref[VEC_GAINS:VEC_GAINS + 1, :MOBA_WIDTH]
    kb = _bf16(kn)
    k_s[i] = kb
    kbar_row = jnp.mean(kn, axis=0, keepdims=True)
    kb_rows = lax.broadcasted_iota(jnp.int32, kbar_s.shape, 0)
    kbar_prev = kbar_s[...]
    kbar_s[...] = jnp.where(kb_rows == i, kbar_row, kbar_prev)
    kbar_b = _bf16(kbar_prev)

    zeros_h = jnp.zeros((MOBA_HEAD_DIM, blk), jnp.bfloat16)
    qpairs = []
    for h in range(MOBA_HEADS):
        qh = qbt[h * MOBA_HEAD_DIM:(h + 1) * MOBA_HEAD_DIM]
        qpairs.append(jnp.concatenate([qh, zeros_h] if h % 2 == 0 else [zeros_h, qh], axis=0))

    p_mem = [jnp.exp(s - jnp.max(s, axis=-1, keepdims=True)) for s in s_mem]

    pair_cols = lambda h: slice((h // 2) * PAIR, (h // 2 + 1) * PAIR)
    gates = [_dot(kbar_b[:, pair_cols(h)], qpairs[h]) for h in range(MOBA_HEADS)]
    s_own = [_dot(kb[:, pair_cols(h)], qpairs[h]) for h in range(MOBA_HEADS)]
    o_mem = [_dot(_bf16(p_mem[h]), memv_ref[0, :, h * MEM_HEAD_DIM:(h + 1) * MEM_HEAD_DIM])
             / jnp.sum(p_mem[h], axis=-1, keepdims=True) for h in range(MEM_HEADS)]
    merged = merged + gate(g2, 2) * _dot(_bf16(jnp.concatenate(o_mem, axis=-1)), wbx_ref[...])

    lane = lax.broadcasted_iota(jnp.int32, (blk, LANES), 1)
    brow = lax.broadcasted_iota(jnp.int32, (nbr, blk), 0)
    browf = brow.astype(jnp.float32)
    neg_inf = jnp.float32(-jnp.inf)
    sel_pad = jnp.full((LANES - nbr, blk), MASKED, jnp.bfloat16)
    ones_r = jnp.ones((BF16_ROWS, blk), jnp.bfloat16)
    vts = []
    for h in range(MOBA_HEADS):
        g = jnp.where(brow < i, gates[h], neg_inf)
        sel = jnp.full((nbr, blk), MASKED, jnp.float32)
        for _ in range(MOBA_TOPK):
            mx = jnp.max(g, axis=0, keepdims=True)
            idx = jnp.min(jnp.where(g == mx, browf, float(nbr)), axis=0, keepdims=True)
            pick = browf == idx
            sel = jnp.where(pick & (mx > neg_inf), 0.0, sel)
            g = jnp.where(pick, neg_inf, g)
        rhs_s[h] = jnp.concatenate([qpairs[h], _bf16(sel), sel_pad], axis=0)
        vt = jnp.concatenate([vbt[h * MOBA_HEAD_DIM:(h + 1) * MOBA_HEAD_DIM], ones_r], axis=0)
        vt_s[i, h] = vt
        vts.append(vt)

    def pair_scores(dst, h, tp):
        dst_s, dst_mx = dst
        parts = []
        for jn in (2 * tp, 2 * tp + 1):
            en = jnp.where(lane == jnp.where(jn < i, jn, LANES - 1), 1.0, 0.0).astype(jnp.bfloat16)
            parts.append(jnp.concatenate([k_s[jnp.minimum(jn, i), :, pair_cols(h)], en], axis=1))
        s = _dot(jnp.concatenate(parts, axis=0), rhs_s[h])
        dst_s[h] = s
        dst_mx[h] = jnp.broadcast_to(jnp.max(s, axis=0, keepdims=True), (SUBLANES, blk))

    causal = (lax.broadcasted_iota(jnp.int32, (blk, blk), 0) <= lax.broadcasted_iota(jnp.int32, (blk, blk), 1))
    set_a, set_b = (sa_s, ma_s), (sb_s, mb_s)
    for h in range(MOBA_HEADS):
        pair_scores(set_a, h, 0)
        s = jnp.where(causal, s_own[h], MASKED)
        m0 = jnp.max(s, axis=0, keepdims=True)
        acc_s[h] = _dot(vts[h], jnp.exp2(_bf16(s - m0)))
        m_s[h] = jnp.broadcast_to(m0, (SUBLANES, blk))

    def attend(src, dst, tp):
        src_s, src_mx = src
        ja = jnp.minimum(2 * tp, i)
        jb = jnp.minimum(2 * tp + 1, i)
        for h in range(MOBA_HEADS):
            if dst is not None:
                pair_scores(dst, h, tp + 1)
            m_old = m_s[h]
            m_new = jnp.maximum(m_old, src_mx[h])
            p = jnp.exp2(_bf16(src_s[h] - m_new[0:1]))
            pv = _dot(jnp.concatenate([vt_s[ja, h], vt_s[jb, h]], axis=1), p)
            acc_s[h] = acc_s[h] * jnp.exp2(m_old[0:1] - m_new[0:1]) + pv
            m_s[h] = m_new

    n_pairs = (i + 1) // 2

    def two_pairs(u, carry):
        attend(set_a, set_b, 2 * u)
        attend(set_b, set_a, 2 * u + 1)
        return carry

    lax.fori_loop(0, n_pairs // 2, two_pairs, 0)

    @pl.when(n_pairs % 2 == 1)
    def _():
        attend(set_a, None, n_pairs - 1)

    g1 = proj(W_GATE + D_MODEL, D_MODEL)
    outs = []
    for h in range(MOBA_HEADS):
        a = acc_s[h]
        outs.append(a[:MOBA_HEAD_DIM] / a[MOBA_HEAD_DIM:MOBA_HEAD_DIM + 1])
    y_moba = jnp.concatenate(outs, axis=0).T
    merged = merged + gate(g1, 1) * _dot(_bf16(y_moba), wbm_ref[...])

    out_ref[0, rows, :] = x + _dot(_bf16(merged), wo_ref[...])


def _resident(shape):
    return pl.BlockSpec(shape, lambda b, i: (0,) * len(shape), pipeline_mode=pl.Buffered(1))


def _mixer(x, memk, memv, g_mix, w_in, b_gate, conv_w, conv_b, moba_q_gain, moba_k_gain, memq_gain,
           w_br_conv, w_br_moba, w_br_mem, w_o):
    bsz, s, d = x.shape
    blk = MOBA_BLOCK
    assert s % (MIXER_SUBS * blk) == 0 and d == D_MODEL
    nb = s // blk
    nbr = pl.cdiv(nb, BF16_ROWS) * BF16_ROWS
    assert nbr < LANES
    w_main = _bf16(w_in)
    w_qv = lax.optimization_barrier(jnp.concatenate([w_in[:, W_Q:W_K], w_in[:, W_V:W_QMEM]], axis=1))
    w_qvt = _bf16(w_qv.T)
    qgt = jnp.broadcast_to(
        (jnp.tile(moba_q_gain, MOBA_HEADS) * (MOBA_HEAD_DIM ** -0.5 * math.log2(math.e)))[:, None], (MOBA_WIDTH, blk))
    gains = jnp.concatenate([jnp.tile(moba_k_gain, MOBA_HEADS), memq_gain * (MEM_HEAD_DIM ** -0.5),
                             jnp.zeros((d - MOBA_WIDTH - MEM_HEAD_DIM,), jnp.float32)])
    vec = jnp.concatenate([g_mix.reshape(1, d), b_gate.reshape(3, d), conv_w[0:2].reshape(1, d),
                           jnp.concatenate([conv_w[2], conv_b]).reshape(1, d), gains.reshape(1, d),
                           jnp.zeros((VEC_ROWS - VEC_GAINS - 1, d), jnp.float32)], axis=0)
    head_of = jnp.arange(MOBA_WIDTH) // MOBA_HEAD_DIM
    gsum = (head_of[:, None] == head_of[None, :]).astype(jnp.bfloat16)
    return pl.pallas_call(
        _mixer_kernel,
        grid=(bsz, nb // MIXER_SUBS),
        in_specs=[
            pl.BlockSpec((1, MIXER_SUBS * blk, d), lambda b, i: (b, i, 0)),
            _resident((VEC_ROWS, d)),
            _resident((d, IN_COLS)),
            _resident((2 * MOBA_WIDTH, d)),
            _resident((MOBA_WIDTH, blk)),
            _resident((MOBA_WIDTH, MOBA_WIDTH)),
            pl.BlockSpec((1, MEM_LEN, MEM_WIDTH), lambda b, i: (b, 0, 0)),
            pl.BlockSpec((1, MEM_LEN, MEM_WIDTH), lambda b, i: (b, 0, 0)),
            _resident((CONV_CH, d)),
            _resident((MOBA_WIDTH, d)),
            _resident((MEM_WIDTH, d)),
            _resident((d, d)),
        ],
        out_specs=pl.BlockSpec((1, MIXER_SUBS * blk, d), lambda b, i: (b, i, 0)),
        out_shape=jax.ShapeDtypeStruct((bsz, s, d), jnp.float32),
        scratch_shapes=[
            pltpu.VMEM((nb, blk, MOBA_WIDTH), jnp.bfloat16),
            pltpu.VMEM((nb, MOBA_HEADS, VT_ROWS, blk), jnp.bfloat16),
            pltpu.VMEM((nbr, MOBA_WIDTH), jnp.float32),
            pltpu.VMEM((SUBLANES + blk, CONV_CH), jnp.float32),
            pltpu.VMEM((MOBA_HEADS, 2 * LANES, blk), jnp.bfloat16),
            pltpu.VMEM((MOBA_HEADS, 2 * blk, blk), jnp.float32),
            pltpu.VMEM((MOBA_HEADS, 2 * blk, blk), jnp.float32),
            pltpu.VMEM((MOBA_HEADS, SUBLANES, blk), jnp.float32),
            pltpu.VMEM((MOBA_HEADS, SUBLANES, blk), jnp.float32),
            pltpu.VMEM((MOBA_HEADS, VT_ROWS, blk), jnp.float32),
            pltpu.VMEM((MOBA_HEADS, SUBLANES, blk), jnp.float32),
        ],
        compiler_params=pltpu.CompilerParams(
            dimension_semantics=("arbitrary", "arbitrary"),
            vmem_limit_bytes=VMEM_LIMIT_MIXER),
        name="mixer",
    )(x, vec, w_main, w_qvt, qgt, gsum, memk, memv, _bf16(w_br_conv), _bf16(w_br_moba), _bf16(w_br_mem), _bf16(w_o))


def _ffn_kernel(x_ref, g_ref, wup_ref, cw_ref, cb_ref, wdown_ref, out_ref, aext_s, act_s):
    t = pl.program_id(1)
    tm = FFN_TILE
    x = x_ref[0]
    hb = _bf16(_rmsnorm(x, g_ref[...]))

    @pl.when(t == 0)
    def _():
        aext_s[0:SUBLANES, :] = jnp.zeros((SUBLANES, D_FF), jnp.float32)

    for c0, cw in FFN_CHUNKS:
        cs = slice(c0, c0 + cw)
        a = _dot(hb, wup_ref[:, c0:c0 + cw])
        b = _dot(hb, wup_ref[:, D_FF + c0:D_FF + c0 + cw])
        aext_s[SUBLANES:SUBLANES + tm, cs] = a
        a1 = aext_s[SUBLANES - 1:SUBLANES - 1 + tm, cs]
        a2 = aext_s[SUBLANES - 2:SUBLANES - 2 + tm, cs]
        aext_s[0:SUBLANES, cs] = a[tm - SUBLANES:, :]
        ac = a2 * cw_ref[0:1, cs] + a1 * cw_ref[1:2, cs] + a * cw_ref[2:3, cs] + cb_ref[:, cs]
        act_s[:, cs] = _bf16(ac * jax.nn.sigmoid(ac) * b)
    out_ref[0] = x + _dot(act_s[...], wdown_ref[...])


def _ffn(x, g_ffn, w_up, ffn_conv_w, ffn_conv_b, w_down):
    bsz, s, d = x.shape
    tm = FFN_TILE
    assert s % tm == 0
    return pl.pallas_call(
        _ffn_kernel,
        grid=(bsz, s // tm),
        in_specs=[
            pl.BlockSpec((1, tm, d), lambda b, t: (b, t, 0)),
            _resident((1, d)),
            _resident((d, 2 * D_FF)),
            _resident((3, D_FF)),
            _resident((1, D_FF)),
            _resident((D_FF, d)),
        ],
        out_specs=pl.BlockSpec((1, tm, d), lambda b, t: (b, t, 0)),
        out_shape=jax.ShapeDtypeStruct((bsz, s, d), jnp.float32),
        scratch_shapes=[
            pltpu.VMEM((SUBLANES + tm, D_FF), jnp.float32),
            pltpu.VMEM((tm, D_FF), jnp.bfloat16),
        ],
        compiler_params=pltpu.CompilerParams(
            dimension_semantics=("arbitrary", "arbitrary"),
            vmem_limit_bytes=VMEM_LIMIT_FFN),
        name="ffn",
    )(x, g_ffn.reshape(1, d), _bf16(w_up), ffn_conv_w, ffn_conv_b.reshape(1, D_FF), _bf16(w_down))


def kernel(x, mem, g_mix, w_in, b_gate, conv_w, conv_b, moba_q_gain, moba_k_gain, g_mem, w_mem_kv, memq_gain,
           memk_gain, w_br_conv, w_br_moba, w_br_mem, w_o, g_ffn, w_up, ffn_conv_w, ffn_conv_b, w_down):
    memk, memv = _memkv(mem, g_mem, w_mem_kv, memk_gain)
    x = _mixer(x, memk, memv, g_mix, w_in, b_gate, conv_w, conv_b, moba_q_gain, moba_k_gain, memq_gain,
               w_br_conv, w_br_moba, w_br_mem, w_o)
    return _ffn(x, g_ffn, w_up, ffn_conv_w, ffn_conv_b, w_down)
```

```python
import math

import jax
import jax.numpy as jnp
from jax import lax
from jax.experimental import pallas as pl
from jax.experimental.pallas import tpu as pltpu

D_MODEL = 1024
MEM_LEN = 256
CONV_CH = 512
MOBA_HEADS = 8
MOBA_HEAD_DIM = 64
MOBA_WIDTH = MOBA_HEADS * MOBA_HEAD_DIM
MOBA_BLOCK = 256
MOBA_TOPK = 3
MEM_HEADS = 4
MEM_HEAD_DIM = 128
MEM_WIDTH = MEM_HEADS * MEM_HEAD_DIM
D_FF = 2816
EPS = 1e-6

W_CONV = 0
W_Q = 3 * CONV_CH
W_K = W_Q + MOBA_WIDTH
W_V = W_K + MOBA_WIDTH
W_QMEM = W_V + MOBA_WIDTH
W_GATE = W_QMEM + MEM_WIDTH
IN_COLS = W_GATE + 3 * D_MODEL

LANES = 128
SUBLANES = 8
BF16_ROWS = 16
PAIR = 2 * MOBA_HEAD_DIM
VT_ROWS = MOBA_HEAD_DIM + BF16_ROWS
MASKED = -1e30
VEC_GMIX, VEC_BGATE, VEC_CONV01, VEC_CONV2B, VEC_GAINS, VEC_ROWS = 0, 1, 4, 5, 6, 16
MIXER_SUBS = 2
VMEM_LIMIT_MIXER = 60 * 1024 * 1024
VMEM_LIMIT_FFN = 48 * 1024 * 1024
FFN_TILE = 512
FFN_CHUNKS = ((0, 1024), (1024, 1024), (2048, 768))

_NT = (((1,), (1,)), ((), ()))


def _dot(a, b):
    return jnp.dot(a, b, preferred_element_type=jnp.float32)


def _dot_nt(a, b):
    return lax.dot_general(a, b, _NT, preferred_element_type=jnp.float32)


def _rmsnorm(x, g):
    return x * lax.rsqrt(jnp.mean(x * x, axis=-1, keepdims=True) + EPS) * g


def _bf16(x):
    return x.astype(jnp.bfloat16)


def _memkv_kernel(mem_ref, g_ref, w_ref, kg_ref, k_ref, v_ref):
    mn = _rmsnorm(mem_ref[0], g_ref[...])
    kv = _dot(_bf16(mn), w_ref[...])
    for h in range(MEM_HEADS):
        lo = h * MEM_HEAD_DIM
        kh = _rmsnorm(kv[:, lo:lo + MEM_HEAD_DIM], kg_ref[...])
        k_ref[0, :, lo:lo + MEM_HEAD_DIM] = _bf16(kh)
    v_ref[0] = _bf16(kv[:, MEM_WIDTH:])


def _memkv(mem, g_mem, w_mem_kv, memk_gain):
    bsz, m, d = mem.shape
    const = lambda b: (0, 0)
    return pl.pallas_call(
        _memkv_kernel,
        grid=(bsz,),
        in_specs=[
            pl.BlockSpec((1, m, d), lambda b: (b, 0, 0)),
            pl.BlockSpec((1, d), const),
            pl.BlockSpec((d, 2 * MEM_WIDTH), const),
            pl.BlockSpec((1, MEM_HEAD_DIM), const),
        ],
        out_specs=[
            pl.BlockSpec((1, m, MEM_WIDTH), lambda b: (b, 0, 0)),
            pl.BlockSpec((1, m, MEM_WIDTH), lambda b: (b, 0, 0)),
        ],
        out_shape=[
            jax.ShapeDtypeStruct((bsz, m, MEM_WIDTH), jnp.bfloat16),
            jax.ShapeDtypeStruct((bsz, m, MEM_WIDTH), jnp.bfloat16),
        ],
        name="memkv",
    )(mem, g_mem.reshape(1, d), _bf16(w_mem_kv), memk_gain.reshape(1, MEM_HEAD_DIM))


def _mixer_kernel(x_hbm, vec_ref, wmain_ref, wqvt_ref, qgt_ref, gsum_ref,
                  memk_ref, memv_ref, wbc_ref, wbm_ref, wbx_ref, wo_ref,
                  out_hbm,
                  k_s, vt_s, kbar_s, uext_s, rhs_s, sa_s, sb_s, ma_s, mb_s, acc_s, m_s,
                  xbuf, obuf, sem_in, sem_out):
    b = pl.program_id(0)
    win = MIXER_SUBS * MOBA_BLOCK
    n_win = x_hbm.shape[1] // win

    def x_copy(w, slot):
        return pltpu.make_async_copy(x_hbm.at[b, pl.ds(w * win, win), :], xbuf.at[slot], sem_in.at[slot])

    def out_copy(w, slot):
        return pltpu.make_async_copy(obuf.at[slot], out_hbm.at[b, pl.ds(w * win, win), :], sem_out.at[slot])

    x_copy(0, 0).start()
    uext_s[0:SUBLANES, :] = jnp.zeros((SUBLANES, CONV_CH), jnp.float32)
    kbar_s[...] = jnp.zeros(kbar_s.shape, jnp.float32)

    def window(w, carry):
        slot = lax.rem(w, 2)

        @pl.when(w + 1 < n_win)
        def _():
            x_copy(w + 1, 1 - slot).start()

        x_copy(w, slot).wait()

        @pl.when(w >= 2)
        def _():
            out_copy(w - 2, slot).wait()

        for sub in range(MIXER_SUBS):
            _mixer_block(w * MIXER_SUBS + sub, slice(sub * MOBA_BLOCK, (sub + 1) * MOBA_BLOCK),
                         xbuf.at[slot], vec_ref, wmain_ref, wqvt_ref, qgt_ref, gsum_ref,
                         memk_ref, memv_ref, wbc_ref, wbm_ref, wbx_ref, wo_ref,
                         obuf.at[slot],
                         k_s, vt_s, kbar_s, uext_s, rhs_s, sa_s, sb_s, ma_s, mb_s, acc_s, m_s)
        out_copy(w, slot).start()
        return carry

    lax.fori_loop(0, n_win, window, 0)
    for w in range(max(n_win - 2, 0), n_win):
        out_copy(w, w % 2).wait()


def _mixer_block(i, rows, x_ref, vec_ref, wmain_ref, wqvt_ref, qgt_ref, gsum_ref,
                 memk_ref, memv_ref, wbc_ref, wbm_ref, wbx_ref, wo_ref,
                 out_ref,
                 k_s, vt_s, kbar_s, uext_s, rhs_s, sa_s, sb_s, ma_s, mb_s, acc_s, m_s):
    blk = MOBA_BLOCK
    nbr = kbar_s.shape[0]
    x = x_ref[rows, :]
    hb = _bf16(_rmsnorm(x, vec_ref[VEC_GMIX:VEC_GMIX + 1, :]))

    def proj(lo, width):
        return _dot(hb, wmain_ref[:, lo:lo + width])

    def gate(g, n):
        return jax.nn.sigmoid(g + vec_ref[VEC_BGATE + n:VEC_BGATE + n + 1, :])

    pc = proj(W_CONV, 3 * CONV_CH)
    k_raw = proj(W_K, MOBA_WIDTH)
    qvt = _dot_nt(wqvt_ref[...], hb)
    qm = proj(W_QMEM, MEM_WIDTH)

    u = pc[:, CONV_CH:2 * CONV_CH] * pc[:, 2 * CONV_CH:]
    uext_s[SUBLANES:SUBLANES + blk, :] = u
    u1 = uext_s[SUBLANES - 1:SUBLANES - 1 + blk, :]
    u2 = uext_s[SUBLANES - 2:SUBLANES - 2 + blk, :]
    uext_s[0:SUBLANES, :] = u[blk - SUBLANES:, :]
    conv = (u2 * vec_ref[VEC_CONV01:VEC_CONV01 + 1, :CONV_CH] + u1 * vec_ref[VEC_CONV01:VEC_CONV01 + 1, CONV_CH:]
            + u * vec_ref[VEC_CONV2B:VEC_CONV2B + 1, :CONV_CH] + vec_ref[VEC_CONV2B:VEC_CONV2B + 1, CONV_CH:])
    y_conv = _bf16(pc[:, :CONV_CH] * conv)

    k2_b = _bf16(k_raw * k_raw)

    q3 = qvt[:MOBA_WIDTH].reshape(MOBA_HEADS, MOBA_HEAD_DIM, blk)
    ssq = jnp.sum(q3 * q3, axis=1, keepdims=True)
    qbt = _bf16((q3 * lax.rsqrt(ssq * (1.0 / MOBA_HEAD_DIM) + EPS)).reshape(MOBA_WIDTH, blk) * qgt_ref[...])
    vbt = _bf16(qvt[MOBA_WIDTH:])

    mqg = vec_ref[VEC_GAINS:VEC_GAINS + 1, MOBA_WIDTH:MOBA_WIDTH + MEM_HEAD_DIM]
    qmb = [_bf16(_rmsnorm(qm[:, h * MEM_HEAD_DIM:(h + 1) * MEM_HEAD_DIM], mqg)) for h in range(MEM_HEADS)]

    g0 = proj(W_GATE, D_MODEL)
    kss = _dot(k2_b, gsum_ref[...])
    p_conv = _dot(y_conv, wbc_ref[...])
    s_mem = [_dot_nt(qmb[h], memk_ref[0, :, h * MEM_HEAD_DIM:(h + 1) * MEM_HEAD_DIM]) for h in range(MEM_HEADS)]
    g2 = proj(W_GATE + 2 * D_MODEL, D_MODEL)

    merged = gate(g0, 0) * p_conv
    kn = k_raw * lax.rsqrt(kss * (1.0 / MOBA_HEAD_DIM) + EPS) * vec_ref[VEC_GAINS:VEC_GAINS + 1, :MOBA_WIDTH]
    kb = _bf16(kn)
    k_s[i] = kb
    kbar_row = jnp.mean(kn, axis=0, keepdims=True)
    kb_rows = lax.broadcasted_iota(jnp.int32, kbar_s.shape, 0)
    kbar_prev = kbar_s[...]
    kbar_s[...] = jnp.where(kb_rows == i, kbar_row, kbar_prev)
    kbar_b = _bf16(kbar_prev)

    zeros_h = jnp.zeros((MOBA_HEAD_DIM, blk), jnp.bfloat16)
    qpairs = []
    for h in range(MOBA_HEADS):
        qh = qbt[h * MOBA_HEAD_DIM:(h + 1) * MOBA_HEAD_DIM]
        qpairs.append(jnp.concatenate([qh, zeros_h] if h % 2 == 0 else [zeros_h, qh], axis=0))

    p_mem = [jnp.exp(s - jnp.max(s, axis=-1, keepdims=True)) for s in s_mem]

    pair_cols = lambda h: slice((h // 2) * PAIR, (h // 2 + 1) * PAIR)
    gates = [_dot(kbar_b[:, pair_cols(h)], qpairs[h]) for h in range(MOBA_HEADS)]
    s_own = [_dot(kb[:, pair_cols(h)], qpairs[h]) for h in range(MOBA_HEADS)]
    o_mem = [_dot(_bf16(p_mem[h]), memv_ref[0, :, h * MEM_HEAD_DIM:(h + 1) * MEM_HEAD_DIM])
             / jnp.sum(p_mem[h], axis=-1, keepdims=True) for h in range(MEM_HEADS)]
    merged = merged + gate(g2, 2) * _dot(_bf16(jnp.concatenate(o_mem, axis=-1)), wbx_ref[...])

    lane = lax.broadcasted_iota(jnp.int32, (blk, LANES), 1)
    brow = lax.broadcasted_iota(jnp.int32, (nbr, blk), 0)
    browf = brow.astype(jnp.float32)
    neg_inf = jnp.float32(-jnp.inf)
    sel_pad = jnp.full((LANES - nbr, blk), MASKED, jnp.bfloat16)
    ones_r = jnp.ones((BF16_ROWS, blk), jnp.bfloat16)
    vts = []
    for h in range(MOBA_HEADS):
        g = jnp.where(brow < i, gates[h], neg_inf)
        sel = jnp.full((nbr, blk), MASKED, jnp.float32)
        for _ in range(MOBA_TOPK):
            mx = jnp.max(g, axis=0, keepdims=True)
            idx = jnp.min(jnp.where(g == mx, browf, float(nbr)), axis=0, keepdims=True)
            pick = browf == idx
            sel = jnp.where(pick & (mx > neg_inf), 0.0, sel)
            g = jnp.where(pick, neg_inf, g)
        rhs_s[h] = jnp.concatenate([qpairs[h], _bf16(sel), sel_pad], axis=0)
        vt = jnp.concatenate([vbt[h * MOBA_HEAD_DIM:(h + 1) * MOBA_HEAD_DIM], ones_r], axis=0)
        vt_s[i, h] = vt
        vts.append(vt)

    def pair_scores(dst, h, tp):
        dst_s, dst_mx = dst
        parts = []
        for jn in (2 * tp, 2 * tp + 1):
            en = jnp.where(lane == jnp.where(jn < i, jn, LANES - 1), 1.0, 0.0).astype(jnp.bfloat16)
            parts.append(jnp.concatenate([k_s[jnp.minimum(jn, i), :, pair_cols(h)], en], axis=1))
        s = _dot(jnp.concatenate(parts, axis=0), rhs_s[h])
        dst_s[h] = s
        dst_mx[h] = jnp.broadcast_to(jnp.max(s, axis=0, keepdims=True), (SUBLANES, blk))

    causal = (lax.broadcasted_iota(jnp.int32, (blk, blk), 0) <= lax.broadcasted_iota(jnp.int32, (blk, blk), 1))
    set_a, set_b = (sa_s, ma_s), (sb_s, mb_s)
    for h in range(MOBA_HEADS):
        pair_scores(set_a, h, 0)
        s = jnp.where(causal, s_own[h], MASKED)
        m0 = jnp.max(s, axis=0, keepdims=True)
        acc_s[h] = _dot(vts[h], jnp.exp2(_bf16(s - m0)))
        m_s[h] = jnp.broadcast_to(m0, (SUBLANES, blk))

    def attend(src, dst, tp):
        src_s, src_mx = src
        ja = jnp.minimum(2 * tp, i)
        jb = jnp.minimum(2 * tp + 1, i)
        for h in range(MOBA_HEADS):
            if dst is not None:
                pair_scores(dst, h, tp + 1)
            m_old = m_s[h]
            m_new = jnp.maximum(m_old, src_mx[h])
            p = jnp.exp2(_bf16(src_s[h] - m_new[0:1]))
            pv = _dot(jnp.concatenate([vt_s[ja, h], vt_s[jb, h]], axis=1), p)
            acc_s[h] = acc_s[h] * jnp.exp2(m_old[0:1] - m_new[0:1]) + pv
            m_s[h] = m_new

    n_pairs = (i + 1) // 2

    def two_pairs(u, carry):
        attend(set_a, set_b, 2 * u)
        attend(set_b, set_a, 2 * u + 1)
        return carry

    lax.fori_loop(0, n_pairs // 2, two_pairs, 0)

    @pl.when(n_pairs % 2 == 1)
    def _():
        attend(set_a, None, n_pairs - 1)

    g1 = proj(W_GATE + D_MODEL, D_MODEL)
    outs = []
    for h in range(MOBA_HEADS):
        a = acc_s[h]
        outs.append(a[:MOBA_HEAD_DIM] / a[MOBA_HEAD_DIM:MOBA_HEAD_DIM + 1])
    y_moba = jnp.concatenate(outs, axis=0).T
    merged = merged + gate(g1, 1) * _dot(_bf16(y_moba), wbm_ref[...])

    out_ref[rows, :] = x + _dot(_bf16(merged), wo_ref[...])


def _resident(shape):
    return pl.BlockSpec(shape, lambda *grid_idx: (0,) * len(shape), pipeline_mode=pl.Buffered(1))


def _mixer(x, memk, memv, g_mix, w_in, b_gate, conv_w, conv_b, moba_q_gain, moba_k_gain, memq_gain,
           w_br_conv, w_br_moba, w_br_mem, w_o):
    bsz, s, d = x.shape
    blk = MOBA_BLOCK
    assert s % (MIXER_SUBS * blk) == 0 and d == D_MODEL
    nb = s // blk
    nbr = pl.cdiv(nb, BF16_ROWS) * BF16_ROWS
    assert nbr < LANES
    w_main = _bf16(w_in)
    w_qv = lax.optimization_barrier(jnp.concatenate([w_in[:, W_Q:W_K], w_in[:, W_V:W_QMEM]], axis=1))
    w_qvt = _bf16(w_qv.T)
    qgt = jnp.broadcast_to(
        (jnp.tile(moba_q_gain, MOBA_HEADS) * (MOBA_HEAD_DIM ** -0.5 * math.log2(math.e)))[:, None], (MOBA_WIDTH, blk))
    gains = jnp.concatenate([jnp.tile(moba_k_gain, MOBA_HEADS), memq_gain * (MEM_HEAD_DIM ** -0.5),
                             jnp.zeros((d - MOBA_WIDTH - MEM_HEAD_DIM,), jnp.float32)])
    vec = jnp.concatenate([g_mix.reshape(1, d), b_gate.reshape(3, d), conv_w[0:2].reshape(1, d),
                           jnp.concatenate([conv_w[2], conv_b]).reshape(1, d), gains.reshape(1, d),
                           jnp.zeros((VEC_ROWS - VEC_GAINS - 1, d), jnp.float32)], axis=0)
    head_of = jnp.arange(MOBA_WIDTH) // MOBA_HEAD_DIM
    gsum = (head_of[:, None] == head_of[None, :]).astype(jnp.bfloat16)
    return pl.pallas_call(
        _mixer_kernel,
        grid=(bsz,),
        in_specs=[
            pl.BlockSpec(memory_space=pl.ANY),
            _resident((VEC_ROWS, d)),
            _resident((d, IN_COLS)),
            _resident((2 * MOBA_WIDTH, d)),
            _resident((MOBA_WIDTH, blk)),
            _resident((MOBA_WIDTH, MOBA_WIDTH)),
            pl.BlockSpec((1, MEM_LEN, MEM_WIDTH), lambda b: (b, 0, 0)),
            pl.BlockSpec((1, MEM_LEN, MEM_WIDTH), lambda b: (b, 0, 0)),
            _resident((CONV_CH, d)),
            _resident((MOBA_WIDTH, d)),
            _resident((MEM_WIDTH, d)),
            _resident((d, d)),
        ],
        out_specs=pl.BlockSpec(memory_space=pl.ANY),
        out_shape=jax.ShapeDtypeStruct((bsz, s, d), jnp.float32),
        scratch_shapes=[
            pltpu.VMEM((nb, blk, MOBA_WIDTH), jnp.bfloat16),
            pltpu.VMEM((nb, MOBA_HEADS, VT_ROWS, blk), jnp.bfloat16),
            pltpu.VMEM((nbr, MOBA_WIDTH), jnp.float32),
            pltpu.VMEM((SUBLANES + blk, CONV_CH), jnp.float32),
            pltpu.VMEM((MOBA_HEADS, 2 * LANES, blk), jnp.bfloat16),
            pltpu.VMEM((MOBA_HEADS, 2 * blk, blk), jnp.float32),
            pltpu.VMEM((MOBA_HEADS, 2 * blk, blk), jnp.float32),
            pltpu.VMEM((MOBA_HEADS, SUBLANES, blk), jnp.float32),
            pltpu.VMEM((MOBA_HEADS, SUBLANES, blk), jnp.float32),
            pltpu.VMEM((MOBA_HEADS, VT_ROWS, blk), jnp.float32),
            pltpu.VMEM((MOBA_HEADS, SUBLANES, blk), jnp.float32),
            pltpu.VMEM((2, MIXER_SUBS * blk, d), jnp.float32),
            pltpu.VMEM((2, MIXER_SUBS * blk, d), jnp.float32),
            pltpu.SemaphoreType.DMA((2,)),
            pltpu.SemaphoreType.DMA((2,)),
        ],
        compiler_params=pltpu.CompilerParams(
            dimension_semantics=("arbitrary",),
            vmem_limit_bytes=VMEM_LIMIT_MIXER),
        name="mixer",
    )(x, vec, w_main, w_qvt, qgt, gsum, memk, memv, _bf16(w_br_conv), _bf16(w_br_moba), _bf16(w_br_mem), _bf16(w_o))


def _ffn_kernel(x_ref, g_ref, wup_ref, cw_ref, cb_ref, wdown_ref, out_ref, aext_s, act_s):
    t = pl.program_id(1)
    tm = FFN_TILE
    x = x_ref[0]
    hb = _bf16(_rmsnorm(x, g_ref[...]))

    @pl.when(t == 0)
    def _():
        aext_s[0:SUBLANES, :] = jnp.zeros((SUBLANES, D_FF), jnp.float32)

    for c0, cw in FFN_CHUNKS:
        cs = slice(c0, c0 + cw)
        a = _dot(hb, wup_ref[:, c0:c0 + cw])
        b = _dot(hb, wup_ref[:, D_FF + c0:D_FF + c0 + cw])
        aext_s[SUBLANES:SUBLANES + tm, cs] = a
        a1 = aext_s[SUBLANES - 1:SUBLANES - 1 + tm, cs]
        a2 = aext_s[SUBLANES - 2:SUBLANES - 2 + tm, cs]
        aext_s[0:SUBLANES, cs] = a[tm - SUBLANES:, :]
        ac = a2 * cw_ref[0:1, cs] + a1 * cw_ref[1:2, cs] + a * cw_ref[2:3, cs] + cb_ref[:, cs]
        act_s[:, cs] = _bf16(ac * jax.nn.sigmoid(ac) * b)
    out_ref[0] = x + _dot(act_s[...], wdown_ref[...])


def _ffn(x, g_ffn, w_up, ffn_conv_w, ffn_conv_b, w_down):
    bsz, s, d = x.shape
    tm = FFN_TILE
    assert s % tm == 0
    return pl.pallas_call(
        _ffn_kernel,
        grid=(bsz, s // tm),
        in_specs=[
            pl.BlockSpec((1, tm, d), lambda b, t: (b, t, 0)),
            _resident((1, d)),
            _resident((d, 2 * D_FF)),
            _resident((3, D_FF)),
            _resident((1, D_FF)),
            _resident((D_FF, d)),
        ],
        out_specs=pl.BlockSpec((1, tm, d), lambda b, t: (b, t, 0)),
        out_shape=jax.ShapeDtypeStruct((bsz, s, d), jnp.float32),
        scratch_shapes=[
            pltpu.VMEM((SUBLANES + tm, D_FF), jnp.float32),
            pltpu.VMEM((tm, D_FF), jnp.bfloat16),
        ],
        compiler_params=pltpu.CompilerParams(
            dimension_semantics=("arbitrary", "arbitrary"),
            vmem_limit_bytes=VMEM_LIMIT_FFN),
        name="ffn",
    )(x, g_ffn.reshape(1, d), _bf16(w_up), ffn_conv_w, ffn_conv_b.reshape(1, D_FF), _bf16(w_down))


def kernel(x, mem, g_mix, w_in, b_gate, conv_w, conv_b, moba_q_gain, moba_k_gain, g_mem, w_mem_kv, memq_gain,
           memk_gain, w_br_conv, w_br_moba, w_br_mem, w_o, g_ffn, w_up, ffn_conv_w, ffn_conv_b, w_down):
    memk, memv = _memkv(mem, g_mem, w_mem_kv, memk_gain)
    x = _mixer(x, memk, memv, g_mix, w_in, b_gate, conv_w, conv_b, moba_q_gain, moba_k_gain, memq_gain,
               w_br_conv, w_br_moba, w_br_mem, w_o)
    return _ffn(x, g_ffn, w_up, ffn_conv_w, ffn_conv_b, w_down)
```

```python
import math

import jax
import jax.numpy as jnp
from jax import lax
from jax.experimental import pallas as pl
from jax.experimental.pallas import tpu as pltpu

D_MODEL = 1024
MEM_LEN = 256
CONV_CH = 512
MOBA_HEADS = 8
MOBA_HEAD_DIM = 64
MOBA_WIDTH = MOBA_HEADS * MOBA_HEAD_DIM
MOBA_BLOCK = 256
MOBA_TOPK = 3
MEM_HEADS = 4
MEM_HEAD_DIM = 128
MEM_WIDTH = MEM_HEADS * MEM_HEAD_DIM
D_FF = 2816
EPS = 1e-6

W_CONV = 0
W_Q = 3 * CONV_CH
W_K = W_Q + MOBA_WIDTH
W_V = W_K + MOBA_WIDTH
W_QMEM = W_V + MOBA_WIDTH
W_GATE = W_QMEM + MEM_WIDTH
IN_COLS = W_GATE + 3 * D_MODEL

LANES = 128
SUBLANES = 8
BF16_ROWS = 16
PAIR = 2 * MOBA_HEAD_DIM
VT_ROWS = MOBA_HEAD_DIM + BF16_ROWS
MASKED = -1e30
VEC_GMIX, VEC_BGATE, VEC_CONV01, VEC_CONV2B, VEC_GAINS, VEC_ROWS = 0, 1, 4, 5, 6, 16
MIXER_SUBS = 2
VMEM_LIMIT_MIXER = 60 * 1024 * 1024
VMEM_LIMIT_FFN = 48 * 1024 * 1024
FFN_TILE = 512
FFN_CHUNKS = ((0, 1024), (1024, 1024), (2048, 768))

_NT = (((1,), (1,)), ((), ()))


def _dot(a, b):
    return jnp.dot(a, b, preferred_element_type=jnp.float32)


def _dot_nt(a, b):
    return lax.dot_general(a, b, _NT, preferred_element_type=jnp.float32)


def _rmsnorm(x, g):
    return x * lax.rsqrt(jnp.mean(x * x, axis=-1, keepdims=True) + EPS) * g


def _bf16(x):
    return x.astype(jnp.bfloat16)


def _memkv_kernel(mem_ref, g_ref, w_ref, kg_ref, k_ref, v_ref):
    mn = _rmsnorm(mem_ref[0], g_ref[...])
    kv = _dot(_bf16(mn), w_ref[...])
    for h in range(MEM_HEADS):
        lo = h * MEM_HEAD_DIM
        kh = _rmsnorm(kv[:, lo:lo + MEM_HEAD_DIM], kg_ref[...])
        k_ref[0, :, lo:lo + MEM_HEAD_DIM] = _bf16(kh)
    v_ref[0] = _bf16(kv[:, MEM_WIDTH:])


def _memkv(mem, g_mem, w_mem_kv, memk_gain):
    bsz, m, d = mem.shape
    const = lambda b: (0, 0)
    return pl.pallas_call(
        _memkv_kernel,
        grid=(bsz,),
        in_specs=[
            pl.BlockSpec((1, m, d), lambda b: (b, 0, 0)),
            pl.BlockSpec((1, d), const),
            pl.BlockSpec((d, 2 * MEM_WIDTH), const),
            pl.BlockSpec((1, MEM_HEAD_DIM), const),
        ],
        out_specs=[
            pl.BlockSpec((1, m, MEM_WIDTH), lambda b: (b, 0, 0)),
            pl.BlockSpec((1, m, MEM_WIDTH), lambda b: (b, 0, 0)),
        ],
        out_shape=[
            jax.ShapeDtypeStruct((bsz, m, MEM_WIDTH), jnp.bfloat16),
            jax.ShapeDtypeStruct((bsz, m, MEM_WIDTH), jnp.bfloat16),
        ],
        name="memkv",
    )(mem, g_mem.reshape(1, d), _bf16(w_mem_kv), memk_gain.reshape(1, MEM_HEAD_DIM))


def _mixer_kernel(x_ref, vec_ref, wmain_ref, wqvt_ref, qgt_ref, gsum_ref,
                  memk_ref, memv_ref, wbc_ref, wbm_ref, wbx_ref, wo_ref,
                  out_ref,
                  k_s, vt_s, kbar_s, uext_s, rhs_s, sa_s, sb_s, ma_s, mb_s, acc_s, m_s):
    @pl.when(pl.program_id(1) == 0)
    def _():
        uext_s[0:SUBLANES, :] = jnp.zeros((SUBLANES, CONV_CH), jnp.float32)
        kbar_s[...] = jnp.zeros(kbar_s.shape, jnp.float32)

    for sub in range(MIXER_SUBS):
        _mixer_block(pl.program_id(1) * MIXER_SUBS + sub, slice(sub * MOBA_BLOCK, (sub + 1) * MOBA_BLOCK),
                     x_ref, vec_ref, wmain_ref, wqvt_ref, qgt_ref, gsum_ref,
                     memk_ref, memv_ref, wbc_ref, wbm_ref, wbx_ref, wo_ref,
                     out_ref,
                     k_s, vt_s, kbar_s, uext_s, rhs_s, sa_s, sb_s, ma_s, mb_s, acc_s, m_s)


def _mixer_block(i, rows, x_ref, vec_ref, wmain_ref, wqvt_ref, qgt_ref, gsum_ref,
                 memk_ref, memv_ref, wbc_ref, wbm_ref, wbx_ref, wo_ref,
                 out_ref,
                 k_s, vt_s, kbar_s, uext_s, rhs_s, sa_s, sb_s, ma_s, mb_s, acc_s, m_s):
    blk = MOBA_BLOCK
    nbr = kbar_s.shape[0]
    x = x_ref[0, rows, :]
    hb = _bf16(_rmsnorm(x, vec_ref[VEC_GMIX:VEC_GMIX + 1, :]))

    def proj(lo, width):
        return _dot(hb, wmain_ref[:, lo:lo + width])

    def gate(g, n):
        return jax.nn.sigmoid(g + vec_ref[VEC_BGATE + n:VEC_BGATE + n + 1, :])

    pc = proj(W_CONV, 3 * CONV_CH)
    k_raw = proj(W_K, MOBA_WIDTH)
    qvt = _dot_nt(wqvt_ref[...], hb)
    qm = proj(W_QMEM, MEM_WIDTH)

    u = pc[:, CONV_CH:2 * CONV_CH] * pc[:, 2 * CONV_CH:]
    uext_s[SUBLANES:SUBLANES + blk, :] = u
    u1 = uext_s[SUBLANES - 1:SUBLANES - 1 + blk, :]
    u2 = uext_s[SUBLANES - 2:SUBLANES - 2 + blk, :]
    uext_s[0:SUBLANES, :] = u[blk - SUBLANES:, :]
    conv = (u2 * vec_ref[VEC_CONV01:VEC_CONV01 + 1, :CONV_CH] + u1 * vec_ref[VEC_CONV01:VEC_CONV01 + 1, CONV_CH:]
            + u * vec_ref[VEC_CONV2B:VEC_CONV2B + 1, :CONV_CH] + vec_ref[VEC_CONV2B:VEC_CONV2B + 1, CONV_CH:])
    y_conv = _bf16(pc[:, :CONV_CH] * conv)

    k2_b = _bf16(k_raw * k_raw)

    q3 = qvt[:MOBA_WIDTH].reshape(MOBA_HEADS, MOBA_HEAD_DIM, blk)
    ssq = jnp.sum(q3 * q3, axis=1, keepdims=True)
    qbt = _bf16((q3 * lax.rsqrt(ssq * (1.0 / MOBA_HEAD_DIM) + EPS)).reshape(MOBA_WIDTH, blk) * qgt_ref[...])
    vbt = _bf16(qvt[MOBA_WIDTH:])

    mqg = vec_ref[VEC_GAINS:VEC_GAINS + 1, MOBA_WIDTH:MOBA_WIDTH + MEM_HEAD_DIM]
    qmb = [_bf16(_rmsnorm(qm[:, h * MEM_HEAD_DIM:(h + 1) * MEM_HEAD_DIM], mqg)) for h in range(MEM_HEADS)]

    g0 = proj(W_GATE, D_MODEL)
    kss = _dot(k2_b, gsum_ref[...])
    p_conv = _dot(y_conv, wbc_ref[...])
    s_mem = [_dot_nt(qmb[h], memk_ref[0, :, h * MEM_HEAD_DIM:(h + 1) * MEM_HEAD_DIM]) for h in range(MEM_HEADS)]
    g2 = proj(W_GATE + 2 * D_MODEL, D_MODEL)

    merged = gate(g0, 0) * p_conv
    kn = k_raw * lax.rsqrt(kss * (1.0 / MOBA_HEAD_DIM) + EPS) * vec_ref[VEC_GAINS:VEC_GAINS + 1, :MOBA_WIDTH]
    kb = _bf16(kn)
    k_s[i] = kb
    kbar_row = jnp.mean(kn, axis=0, keepdims=True)
    kb_rows = lax.broadcasted_iota(jnp.int32, kbar_s.shape, 0)
    kbar_prev = kbar_s[...]
    kbar_s[...] = jnp.where(kb_rows == i, kbar_row, kbar_prev)
    kbar_b = _bf16(kbar_prev)

    zeros_h = jnp.zeros((MOBA_HEAD_DIM, blk), jnp.bfloat16)
    qpairs = []
    for h in range(MOBA_HEADS):
        qh = qbt[h * MOBA_HEAD_DIM:(h + 1) * MOBA_HEAD_DIM]
        qpairs.append(jnp.concatenate([qh, zeros_h] if h % 2 == 0 else [zeros_h, qh], axis=0))

    p_mem = [jnp.exp(s - jnp.max(s, axis=-1, keepdims=True)) for s in s_mem]

    pair_cols = lambda h: slice((h // 2) * PAIR, (h // 2 + 1) * PAIR)
    gates = [_dot(kbar_b[:, pair_cols(h)], qpairs[h]) for h in range(MOBA_HEADS)]
    s_own = [_dot(kb[:, pair_cols(h)], qpairs[h]) for h in range(MOBA_HEADS)]
    o_mem = [_dot(_bf16(p_mem[h]), memv_ref[0, :, h * MEM_HEAD_DIM:(h + 1) * MEM_HEAD_DIM])
             / jnp.sum(p_mem[h], axis=-1, keepdims=True) for h in range(MEM_HEADS)]
    merged = merged + gate(g2, 2) * _dot(_bf16(jnp.concatenate(o_mem, axis=-1)), wbx_ref[...])

    lane = lax.broadcasted_iota(jnp.int32, (blk, LANES), 1)
    brow = lax.broadcasted_iota(jnp.int32, (nbr, blk), 0)
    browf = brow.astype(jnp.float32)
    neg_inf = jnp.float32(-jnp.inf)
    sel_pad = jnp.full((LANES - nbr, blk), MASKED, jnp.bfloat16)
    ones_r = jnp.ones((BF16_ROWS, blk), jnp.bfloat16)
    vts = []
    for h in range(MOBA_HEADS):
        g = jnp.where(brow < i, gates[h], neg_inf)
        sel = jnp.full((nbr, blk), MASKED, jnp.float32)
        for _ in range(MOBA_TOPK):
            mx = jnp.max(g, axis=0, keepdims=True)
            idx = jnp.min(jnp.where(g == mx, browf, float(nbr)), axis=0, keepdims=True)
            pick = browf == idx
            sel = jnp.where(pick & (mx > neg_inf), 0.0, sel)
            g = jnp.where(pick, neg_inf, g)
        rhs_s[h] = jnp.concatenate([qpairs[h], _bf16(sel), sel_pad], axis=0)
        vt = jnp.concatenate([vbt[h * MOBA_HEAD_DIM:(h + 1) * MOBA_HEAD_DIM], ones_r], axis=0)
        vt_s[i, h] = vt
        vts.append(vt)

    def pair_scores(dst, h, tp):
        dst_s, dst_mx = dst
        parts = []
        for jn in (2 * tp, 2 * tp + 1):
            en = jnp.where(lane == jnp.where(jn < i, jn, nbr - 1), 1.0, 0.0).astype(jnp.bfloat16)
            parts.append(jnp.concatenate([k_s[jnp.minimum(jn, i), :, pair_cols(h)], en], axis=1))
        s = _dot(jnp.concatenate(parts, axis=0)[:, :PAIR + nbr], rhs_s[h, 0:PAIR + nbr, :])
        dst_s[h] = s
        dst_mx[h] = jnp.broadcast_to(jnp.max(s, axis=0, keepdims=True), (SUBLANES, blk))

    causal = (lax.broadcasted_iota(jnp.int32, (blk, blk), 0) <= lax.broadcasted_iota(jnp.int32, (blk, blk), 1))
    set_a, set_b = (sa_s, ma_s), (sb_s, mb_s)
    for h in range(MOBA_HEADS):
        pair_scores(set_a, h, 0)
        s = jnp.where(causal, s_own[h], MASKED)
        m0 = jnp.max(s, axis=0, keepdims=True)
        acc_s[h] = _dot(vts[h], jnp.exp2(_bf16(s - m0)))
        m_s[h] = jnp.broadcast_to(m0, (SUBLANES, blk))

    def attend(src, dst, tp):
        src_s, src_mx = src
        ja = jnp.minimum(2 * tp, i)
        jb = jnp.minimum(2 * tp + 1, i)
        for h in range(MOBA_HEADS):
            if dst is not None:
                pair_scores(dst, h, tp + 1)
            m_old = m_s[h]
            m_new = jnp.maximum(m_old, src_mx[h])
            p = jnp.exp2(_bf16(src_s[h] - m_new[0:1]))
            pv = _dot(jnp.concatenate([vt_s[ja, h], vt_s[jb, h]], axis=1), p)
            acc_s[h] = acc_s[h] * jnp.exp2(m_old[0:1] - m_new[0:1]) + pv
            m_s[h] = m_new

    n_pairs = (i + 1) // 2

    def two_pairs(u, carry):
        attend(set_a, set_b, 2 * u)
        attend(set_b, set_a, 2 * u + 1)
        return carry

    lax.fori_loop(0, n_pairs // 2, two_pairs, 0)

    @pl.when(n_pairs % 2 == 1)
    def _():
        attend(set_a, None, n_pairs - 1)

    g1 = proj(W_GATE + D_MODEL, D_MODEL)
    outs = []
    for h in range(MOBA_HEADS):
        a = acc_s[h]
        outs.append(a[:MOBA_HEAD_DIM] / a[MOBA_HEAD_DIM:MOBA_HEAD_DIM + 1])
    y_moba = jnp.concatenate(outs, axis=0).T
    merged = merged + gate(g1, 1) * _dot(_bf16(y_moba), wbm_ref[...])

    out_ref[0, rows, :] = x + _dot(_bf16(merged), wo_ref[...])


def _resident(shape):
    return pl.BlockSpec(shape, lambda b, i: (0,) * len(shape), pipeline_mode=pl.Buffered(1))


def _mixer(x, memk, memv, g_mix, w_in, b_gate, conv_w, conv_b, moba_q_gain, moba_k_gain, memq_gain,
           w_br_conv, w_br_moba, w_br_mem, w_o):
    bsz, s, d = x.shape
    blk = MOBA_BLOCK
    assert s % (MIXER_SUBS * blk) == 0 and d == D_MODEL
    nb = s // blk
    nbr = pl.cdiv(nb, BF16_ROWS) * BF16_ROWS
    assert nbr < LANES
    w_main = _bf16(w_in)
    w_qv = lax.optimization_barrier(jnp.concatenate([w_in[:, W_Q:W_K], w_in[:, W_V:W_QMEM]], axis=1))
    w_qvt = _bf16(w_qv.T)
    qgt = jnp.broadcast_to(
        (jnp.tile(moba_q_gain, MOBA_HEADS) * (MOBA_HEAD_DIM ** -0.5 * math.log2(math.e)))[:, None], (MOBA_WIDTH, blk))
    gains = jnp.concatenate([jnp.tile(moba_k_gain, MOBA_HEADS), memq_gain * (MEM_HEAD_DIM ** -0.5),
                             jnp.zeros((d - MOBA_WIDTH - MEM_HEAD_DIM,), jnp.float32)])
    vec = jnp.concatenate([g_mix.reshape(1, d), b_gate.reshape(3, d), conv_w[0:2].reshape(1, d),
                           jnp.concatenate([conv_w[2], conv_b]).reshape(1, d), gains.reshape(1, d),
                           jnp.zeros((VEC_ROWS - VEC_GAINS - 1, d), jnp.float32)], axis=0)
    head_of = jnp.arange(MOBA_WIDTH) // MOBA_HEAD_DIM
    gsum = (head_of[:, None] == head_of[None, :]).astype(jnp.bfloat16)
    return pl.pallas_call(
        _mixer_kernel,
        grid=(bsz, nb // MIXER_SUBS),
        in_specs=[
            pl.BlockSpec((1, MIXER_SUBS * blk, d), lambda b, i: (b, i, 0)),
            _resident((VEC_ROWS, d)),
            _resident((d, IN_COLS)),
            _resident((2 * MOBA_WIDTH, d)),
            _resident((MOBA_WIDTH, blk)),
            _resident((MOBA_WIDTH, MOBA_WIDTH)),
            pl.BlockSpec((1, MEM_LEN, MEM_WIDTH), lambda b, i: (b, 0, 0)),
            pl.BlockSpec((1, MEM_LEN, MEM_WIDTH), lambda b, i: (b, 0, 0)),
            _resident((CONV_CH, d)),
            _resident((MOBA_WIDTH, d)),
            _resident((MEM_WIDTH, d)),
            _resident((d, d)),
        ],
        out_specs=pl.BlockSpec((1, MIXER_SUBS * blk, d), lambda b, i: (b, i, 0)),
        out_shape=jax.ShapeDtypeStruct((bsz, s, d), jnp.float32),
        scratch_shapes=[
            pltpu.VMEM((nb, blk, MOBA_WIDTH), jnp.bfloat16),
            pltpu.VMEM((nb, MOBA_HEADS, VT_ROWS, blk), jnp.bfloat16),
            pltpu.VMEM((nbr, MOBA_WIDTH), jnp.float32),
            pltpu.VMEM((SUBLANES + blk, CONV_CH), jnp.float32),
            pltpu.VMEM((MOBA_HEADS, 2 * LANES, blk), jnp.bfloat16),
            pltpu.VMEM((MOBA_HEADS, 2 * blk, blk), jnp.float32),
            pltpu.VMEM((MOBA_HEADS, 2 * blk, blk), jnp.float32),
            pltpu.VMEM((MOBA_HEADS, SUBLANES, blk), jnp.float32),
            pltpu.VMEM((MOBA_HEADS, SUBLANES, blk), jnp.float32),
            pltpu.VMEM((MOBA_HEADS, VT_ROWS, blk), jnp.float32),
            pltpu.VMEM((MOBA_HEADS, SUBLANES, blk), jnp.float32),
        ],
        compiler_params=pltpu.CompilerParams(
            dimension_semantics=("arbitrary", "arbitrary"),
            vmem_limit_bytes=VMEM_LIMIT_MIXER),
        name="mixer",
    )(x, vec, w_main, w_qvt, qgt, gsum, memk, memv, _bf16(w_br_conv), _bf16(w_br_moba), _bf16(w_br_mem), _bf16(w_o))


def _ffn_kernel(x_ref, g_ref, wup_ref, cw_ref, cb_ref, wdown_ref, out_ref, aext_s, act_s):
    t = pl.program_id(1)
    tm = FFN_TILE
    x = x_ref[0]
    hb = _bf16(_rmsnorm(x, g_ref[...]))

    @pl.when(t == 0)
    def _():
        aext_s[0:SUBLANES, :] = jnp.zeros((SUBLANES, D_FF), jnp.float32)

    for c0, cw in FFN_CHUNKS:
        cs = slice(c0, c0 + cw)
        a = _dot(hb, wup_ref[:, c0:c0 + cw])
        b = _dot(hb, wup_ref[:, D_FF + c0:D_FF + c0 + cw])
        aext_s[SUBLANES:SUBLANES + tm, cs] = a
        a1 = aext_s[SUBLANES - 1:SUBLANES - 1 + tm, cs]
        a2 = aext_s[SUBLANES - 2:SUBLANES - 2 + tm, cs]
        aext_s[0:SUBLANES, cs] = a[tm - SUBLANES:, :]
        ac = a2 * cw_ref[0:1, cs] + a1 * cw_ref[1:2, cs] + a * cw_ref[2:3, cs] + cb_ref[:, cs]
        act_s[:, cs] = _bf16(ac * jax.nn.sigmoid(ac) * b)
    out_ref[0] = x + _dot(act_s[...], wdown_ref[...])


def _ffn(x, g_ffn, w_up, ffn_conv_w, ffn_conv_b, w_down):
    bsz, s, d = x.shape
    tm = FFN_TILE
    assert s % tm == 0
    return pl.pallas_call(
        _ffn_kernel,
        grid=(bsz, s // tm),
        in_specs=[
            pl.BlockSpec((1, tm, d), lambda b, t: (b, t, 0)),
            _resident((1, d)),
            _resident((d, 2 * D_FF)),
            _resident((3, D_FF)),
            _resident((1, D_FF)),
            _resident((D_FF, d)),
        ],
        out_specs=pl.BlockSpec((1, tm, d), lambda b, t: (b, t, 0)),
        out_shape=jax.ShapeDtypeStruct((bsz, s, d), jnp.float32),
        scratch_shapes=[
            pltpu.VMEM((SUBLANES + tm, D_FF), jnp.float32),
            pltpu.VMEM((tm, D_FF), jnp.bfloat16),
        ],
        compiler_params=pltpu.CompilerParams(
            dimension_semantics=("arbitrary", "arbitrary"),
            vmem_limit_bytes=VMEM_LIMIT_FFN),
        name="ffn",
    )(x, g_ffn.reshape(1, d), _bf16(w_up), ffn_conv_w, ffn_conv_b.reshape(1, D_FF), _bf16(w_down))


def kernel(x, mem, g_mix, w_in, b_gate, conv_w, conv_b, moba_q_gain, moba_k_gain, g_mem, w_mem_kv, memq_gain,
           memk_gain, w_br_conv, w_br_moba, w_br_mem, w_o, g_ffn, w_up, ffn_conv_w, ffn_conv_b, w_down):
    memk, memv = _memkv(mem, g_mem, w_mem_kv, memk_gain)
    x = _mixer(x, memk, memv, g_mix, w_in, b_gate, conv_w, conv_b, moba_q_gain, moba_k_gain, memq_gain,
               w_br_conv, w_br_moba, w_br_mem, w_o)
    return _ffn(x, g_ffn, w_up, ffn_conv_w, ffn_conv_b, w_down)
```

```python
import math

import jax
import jax.numpy as jnp
from jax import lax
from jax.experimental import pallas as pl
from jax.experimental.pallas import tpu as pltpu

D_MODEL = 1024
MEM_LEN = 256
CONV_CH = 512
MOBA_HEADS = 8
MOBA_HEAD_DIM = 64
MOBA_WIDTH = MOBA_HEADS * MOBA_HEAD_DIM
MOBA_BLOCK = 256
MOBA_TOPK = 3
MEM_HEADS = 4
MEM_HEAD_DIM = 128
MEM_WIDTH = MEM_HEADS * MEM_HEAD_DIM
D_FF = 2816
EPS = 1e-6

W_CONV = 0
W_Q = 3 * CONV_CH
W_K = W_Q + MOBA_WIDTH
W_V = W_K + MOBA_WIDTH
W_QMEM = W_V + MOBA_WIDTH
W_GATE = W_QMEM + MEM_WIDTH
IN_COLS = W_GATE + 3 * D_MODEL

LANES = 128
SUBLANES = 8
BF16_ROWS = 16
PAIR = 2 * MOBA_HEAD_DIM
VT_ROWS = MOBA_HEAD_DIM + BF16_ROWS
MASKED = -1e30
VEC_GMIX, VEC_BGATE, VEC_CONV01, VEC_CONV2B, VEC_GAINS, VEC_ROWS = 0, 1, 4, 5, 6, 16
MIXER_SUBS = 2
VMEM_LIMIT_MIXER = 60 * 1024 * 1024
VMEM_LIMIT_FFN = 48 * 1024 * 1024
FFN_TILE = 512
FFN_CHUNKS = ((0, 1024), (1024, 1024), (2048, 768))

_NT = (((1,), (1,)), ((), ()))


def _dot(a, b):
    return jnp.dot(a, b, preferred_element_type=jnp.float32)


def _dot_nt(a, b):
    return lax.dot_general(a, b, _NT, preferred_element_type=jnp.float32)


def _rmsnorm(x, g):
    return x * lax.rsqrt(jnp.mean(x * x, axis=-1, keepdims=True) + EPS) * g


def _bf16(x):
    return x.astype(jnp.bfloat16)


def _memkv_kernel(mem_ref, g_ref, w_ref, kg_ref, k_ref, v_ref):
    mn = _rmsnorm(mem_ref[0], g_ref[...])
    kv = _dot(_bf16(mn), w_ref[...])
    for h in range(MEM_HEADS):
        lo = h * MEM_HEAD_DIM
        kh = _rmsnorm(kv[:, lo:lo + MEM_HEAD_DIM], kg_ref[...])
        k_ref[0, :, lo:lo + MEM_HEAD_DIM] = _bf16(kh)
    v_ref[0] = _bf16(kv[:, MEM_WIDTH:])


def _memkv(mem, g_mem, w_mem_kv, memk_gain):
    bsz, m, d = mem.shape
    const = lambda b: (0, 0)
    return pl.pallas_call(
        _memkv_kernel,
        grid=(bsz,),
        in_specs=[
            pl.BlockSpec((1, m, d), lambda b: (b, 0, 0)),
            pl.BlockSpec((1, d), const),
            pl.BlockSpec((d, 2 * MEM_WIDTH), const),
            pl.BlockSpec((1, MEM_HEAD_DIM), const),
        ],
        out_specs=[
            pl.BlockSpec((1, m, MEM_WIDTH), lambda b: (b, 0, 0)),
            pl.BlockSpec((1, m, MEM_WIDTH), lambda b: (b, 0, 0)),
        ],
        out_shape=[
            jax.ShapeDtypeStruct((bsz, m, MEM_WIDTH), jnp.bfloat16),
            jax.ShapeDtypeStruct((bsz, m, MEM_WIDTH), jnp.bfloat16),
        ],
        name="memkv",
    )(mem, g_mem.reshape(1, d), _bf16(w_mem_kv), memk_gain.reshape(1, MEM_HEAD_DIM))


def _mixer_kernel(x_ref, vec_ref, wmain_ref, wqvt_ref, qgt_ref, gsum_ref,
                  memk_ref, memv_ref, wbc_ref, wbm_ref, wbx_ref, wo_ref,
                  out_ref,
                  k_s, vt_s, kbar_s, uext_s, rhs_s, sa_s, sb_s, ma_s, mb_s, acc_s, m_s):
    @pl.when(pl.program_id(1) == 0)
    def _():
        uext_s[0:SUBLANES, :] = jnp.zeros((SUBLANES, CONV_CH), jnp.float32)
        kbar_s[...] = jnp.zeros(kbar_s.shape, jnp.float32)

    for sub in range(MIXER_SUBS):
        _mixer_block(pl.program_id(1) * MIXER_SUBS + sub, slice(sub * MOBA_BLOCK, (sub + 1) * MOBA_BLOCK),
                     x_ref, vec_ref, wmain_ref, wqvt_ref, qgt_ref, gsum_ref,
                     memk_ref, memv_ref, wbc_ref, wbm_ref, wbx_ref, wo_ref,
                     out_ref,
                     k_s, vt_s, kbar_s, uext_s, rhs_s, sa_s, sb_s, ma_s, mb_s, acc_s, m_s)


def _mixer_block(i, rows, x_ref, vec_ref, wmain_ref, wqvt_ref, qgt_ref, gsum_ref,
                 memk_ref, memv_ref, wbc_ref, wbm_ref, wbx_ref, wo_ref,
                 out_ref,
                 k_s, vt_s, kbar_s, uext_s, rhs_s, sa_s, sb_s, ma_s, mb_s, acc_s, m_s):
    blk = MOBA_BLOCK
    nbr = kbar_s.shape[0]
    x = x_ref[0, rows, :]
    hb = _bf16(_rmsnorm(x, vec_ref[VEC_GMIX:VEC_GMIX + 1, :]))

    def proj(lo, width):
        return _dot(hb, wmain_ref[:, lo:lo + width])

    def gate(g, n):
        return jax.nn.sigmoid(g + vec_ref[VEC_BGATE + n:VEC_BGATE + n + 1, :])

    pc = proj(W_CONV, 3 * CONV_CH)
    k_raw = proj(W_K, MOBA_WIDTH)
    qvt = _dot_nt(wqvt_ref[...], hb)
    qm = proj(W_QMEM, MEM_WIDTH)

    u = pc[:, CONV_CH:2 * CONV_CH] * pc[:, 2 * CONV_CH:]
    uext_s[SUBLANES:SUBLANES + blk, :] = u
    u1 = uext_s[SUBLANES - 1:SUBLANES - 1 + blk, :]
    u2 = uext_s[SUBLANES - 2:SUBLANES - 2 + blk, :]
    uext_s[0:SUBLANES, :] = u[blk - SUBLANES:, :]
    conv = (u2 * vec_ref[VEC_CONV01:VEC_CONV01 + 1, :CONV_CH] + u1 * vec_ref[VEC_CONV01:VEC_CONV01 + 1, CONV_CH:]
            + u * vec_ref[VEC_CONV2B:VEC_CONV2B + 1, :CONV_CH] + vec_ref[VEC_CONV2B:VEC_CONV2B + 1, CONV_CH:])
    y_conv = _bf16(pc[:, :CONV_CH] * conv)

    k2_b = _bf16(k_raw * k_raw)

    q3 = qvt[:MOBA_WIDTH].reshape(MOBA_HEADS, MOBA_HEAD_DIM, blk)
    ssq = jnp.sum(q3 * q3, axis=1, keepdims=True)
    qbt = _bf16((q3 * lax.rsqrt(ssq * (1.0 / MOBA_HEAD_DIM) + EPS)).reshape(MOBA_WIDTH, blk) * qgt_ref[...])
    vbt = _bf16(qvt[MOBA_WIDTH:])

    mqg = vec_ref[VEC_GAINS:VEC_GAINS + 1, MOBA_WIDTH:MOBA_WIDTH + MEM_HEAD_DIM]
    qmb = [_bf16(_rmsnorm(qm[:, h * MEM_HEAD_DIM:(h + 1) * MEM_HEAD_DIM], mqg)) for h in range(MEM_HEADS)]

    g0 = proj(W_GATE, D_MODEL)
    kss = _dot(k2_b, gsum_ref[...])
    p_conv = _dot(y_conv, wbc_ref[...])
    s_mem = [_dot_nt(qmb[h], memk_ref[0, :, h * MEM_HEAD_DIM:(h + 1) * MEM_HEAD_DIM]) for h in range(MEM_HEADS)]
    g2 = proj(W_GATE + 2 * D_MODEL, D_MODEL)

    merged = gate(g0, 0) * p_conv
    kn = k_raw * lax.rsqrt(kss * (1.0 / MOBA_HEAD_DIM) + EPS) * vec_ref[VEC_GAINS:VEC_GAINS + 1, :MOBA_WIDTH]
    kb = _bf16(kn)
    k_s[i] = kb
    kbar_row = jnp.mean(kn, axis=0, keepdims=True)
    kb_rows = lax.broadcasted_iota(jnp.int32, kbar_s.shape, 0)
    kbar_prev = kbar_s[...]
    kbar_s[...] = jnp.where(kb_rows == i, kbar_row, kbar_prev)
    kbar_b = _bf16(kbar_prev)

    zeros_h = jnp.zeros((MOBA_HEAD_DIM, blk), jnp.bfloat16)
    qpairs = []
    for h in range(MOBA_HEADS):
        qh = qbt[h * MOBA_HEAD_DIM:(h + 1) * MOBA_HEAD_DIM]
        qpairs.append(jnp.concatenate([qh, zeros_h] if h % 2 == 0 else [zeros_h, qh], axis=0))

    p_mem = [jnp.exp(s - jnp.max(s, axis=-1, keepdims=True)) for s in s_mem]

    pair_cols = lambda h: slice((h // 2) * PAIR, (h // 2 + 1) * PAIR)
    gate_own = [_dot(jnp.concatenate([kbar_b[:, pair_cols(h)], kb[:, pair_cols(h)]], axis=0), qpairs[h])
                for h in range(MOBA_HEADS)]
    gates = [g[:nbr] for g in gate_own]
    s_own = [g[nbr:] for g in gate_own]
    o_mem = [_dot(_bf16(p_mem[h]), memv_ref[0, :, h * MEM_HEAD_DIM:(h + 1) * MEM_HEAD_DIM])
             / jnp.sum(p_mem[h], axis=-1, keepdims=True) for h in range(MEM_HEADS)]
    merged = merged + gate(g2, 2) * _dot(_bf16(jnp.concatenate(o_mem, axis=-1)), wbx_ref[...])

    lane = lax.broadcasted_iota(jnp.int32, (blk, LANES), 1)
    brow = lax.broadcasted_iota(jnp.int32, (nbr, blk), 0)
    browf = brow.astype(jnp.float32)
    neg_inf = jnp.float32(-jnp.inf)
    sel_pad = jnp.full((LANES - nbr, blk), MASKED, jnp.bfloat16)
    ones_r = jnp.ones((BF16_ROWS, blk), jnp.bfloat16)
    vts = []
    for h in range(MOBA_HEADS):
        g = jnp.where(brow < i, gates[h], neg_inf)
        sel = jnp.full((nbr, blk), MASKED, jnp.float32)
        for _ in range(MOBA_TOPK):
            mx = jnp.max(g, axis=0, keepdims=True)
            idx = jnp.min(jnp.where(g == mx, browf, float(nbr)), axis=0, keepdims=True)
            pick = browf == idx
            sel = jnp.where(pick & (mx > neg_inf), 0.0, sel)
            g = jnp.where(pick, neg_inf, g)
        rhs_s[h] = jnp.concatenate([qpairs[h], _bf16(sel), sel_pad], axis=0)
        vt = jnp.concatenate([vbt[h * MOBA_HEAD_DIM:(h + 1) * MOBA_HEAD_DIM], ones_r], axis=0)
        vt_s[i, h] = vt
        vts.append(vt)

    def pair_scores(dst, h, tp):
        dst_s, dst_mx = dst
        parts = []
        for jn in (2 * tp, 2 * tp + 1):
            en = jnp.where(lane == jnp.where(jn < i, jn, nbr - 1), 1.0, 0.0).astype(jnp.bfloat16)
            parts.append(jnp.concatenate([k_s[jnp.minimum(jn, i), :, pair_cols(h)], en], axis=1))
        s = _dot(jnp.concatenate(parts, axis=0)[:, :PAIR + nbr], rhs_s[h, 0:PAIR + nbr, :])
        dst_s[h] = s
        dst_mx[h] = jnp.broadcast_to(jnp.max(s, axis=0, keepdims=True), (SUBLANES, blk))

    causal = (lax.broadcasted_iota(jnp.int32, (blk, blk), 0) <= lax.broadcasted_iota(jnp.int32, (blk, blk), 1))
    set_a, set_b = (sa_s, ma_s), (sb_s, mb_s)
    for h in range(MOBA_HEADS):
        pair_scores(set_a, h, 0)
        s = jnp.where(causal, s_own[h], MASKED)
        m0 = jnp.max(s, axis=0, keepdims=True)
        acc_s[h] = _dot(vts[h], jnp.exp2(_bf16(s - m0)))
        m_s[h] = jnp.broadcast_to(m0, (SUBLANES, blk))

    def attend(src, dst, tp):
        src_s, src_mx = src
        ja = jnp.minimum(2 * tp, i)
        jb = jnp.minimum(2 * tp + 1, i)
        for h in range(MOBA_HEADS):
            if dst is not None:
                pair_scores(dst, h, tp + 1)
            m_old = m_s[h]
            m_new = jnp.maximum(m_old, src_mx[h])
            p = jnp.exp2(_bf16(src_s[h] - m_new[0:1]))
            pv = _dot(jnp.concatenate([vt_s[ja, h], vt_s[jb, h]], axis=1), p)
            acc_s[h] = acc_s[h] * jnp.exp2(m_old[0:1] - m_new[0:1]) + pv
            m_s[h] = m_new

    n_pairs = (i + 1) // 2

    def two_pairs(u, carry):
        attend(set_a, set_b, 2 * u)
        attend(set_b, set_a, 2 * u + 1)
        return carry

    lax.fori_loop(0, n_pairs // 2, two_pairs, 0)

    @pl.when(n_pairs % 2 == 1)
    def _():
        attend(set_a, None, n_pairs - 1)

    g1 = proj(W_GATE + D_MODEL, D_MODEL)
    outs = []
    for h in range(MOBA_HEADS):
        a = acc_s[h]
        outs.append(a[:MOBA_HEAD_DIM] / a[MOBA_HEAD_DIM:MOBA_HEAD_DIM + 1])
    y_moba = jnp.concatenate(outs, axis=0).T
    merged = merged + gate(g1, 1) * _dot(_bf16(y_moba), wbm_ref[...])

    out_ref[0, rows, :] = x + _dot(_bf16(merged), wo_ref[...])


def _resident(shape):
    return pl.BlockSpec(shape, lambda b, i: (0,) * len(shape), pipeline_mode=pl.Buffered(1))


def _mixer(x, memk, memv, g_mix, w_in, b_gate, conv_w, conv_b, moba_q_gain, moba_k_gain, memq_gain,
           w_br_conv, w_br_moba, w_br_mem, w_o):
    bsz, s, d = x.shape
    blk = MOBA_BLOCK
    assert s % (MIXER_SUBS * blk) == 0 and d == D_MODEL
    nb = s // blk
    nbr = pl.cdiv(nb, BF16_ROWS) * BF16_ROWS
    assert nbr < LANES
    w_main = _bf16(w_in)
    w_qv = lax.optimization_barrier(jnp.concatenate([w_in[:, W_Q:W_K], w_in[:, W_V:W_QMEM]], axis=1))
    w_qvt = _bf16(w_qv.T)
    qgt = jnp.broadcast_to(
        (jnp.tile(moba_q_gain, MOBA_HEADS) * (MOBA_HEAD_DIM ** -0.5 * math.log2(math.e)))[:, None], (MOBA_WIDTH, blk))
    gains = jnp.concatenate([jnp.tile(moba_k_gain, MOBA_HEADS), memq_gain * (MEM_HEAD_DIM ** -0.5),
                             jnp.zeros((d - MOBA_WIDTH - MEM_HEAD_DIM,), jnp.float32)])
    vec = jnp.concatenate([g_mix.reshape(1, d), b_gate.reshape(3, d), conv_w[0:2].reshape(1, d),
                           jnp.concatenate([conv_w[2], conv_b]).reshape(1, d), gains.reshape(1, d),
                           jnp.zeros((VEC_ROWS - VEC_GAINS - 1, d), jnp.float32)], axis=0)
    head_of = jnp.arange(MOBA_WIDTH) // MOBA_HEAD_DIM
    gsum = (head_of[:, None] == head_of[None, :]).astype(jnp.bfloat16)
    return pl.pallas_call(
        _mixer_kernel,
        grid=(bsz, nb // MIXER_SUBS),
        in_specs=[
            pl.BlockSpec((1, MIXER_SUBS * blk, d), lambda b, i: (b, i, 0)),
            _resident((VEC_ROWS, d)),
            _resident((d, IN_COLS)),
            _resident((2 * MOBA_WIDTH, d)),
            _resident((MOBA_WIDTH, blk)),
            _resident((MOBA_WIDTH, MOBA_WIDTH)),
            pl.BlockSpec((1, MEM_LEN, MEM_WIDTH), lambda b, i: (b, 0, 0)),
            pl.BlockSpec((1, MEM_LEN, MEM_WIDTH), lambda b, i: (b, 0, 0)),
            _resident((CONV_CH, d)),
            _resident((MOBA_WIDTH, d)),
            _resident((MEM_WIDTH, d)),
            _resident((d, d)),
        ],
        out_specs=pl.BlockSpec((1, MIXER_SUBS * blk, d), lambda b, i: (b, i, 0)),
        out_shape=jax.ShapeDtypeStruct((bsz, s, d), jnp.float32),
        scratch_shapes=[
            pltpu.VMEM((nb, blk, MOBA_WIDTH), jnp.bfloat16),
            pltpu.VMEM((nb, MOBA_HEADS, VT_ROWS, blk), jnp.bfloat16),
            pltpu.VMEM((nbr, MOBA_WIDTH), jnp.float32),
            pltpu.VMEM((SUBLANES + blk, CONV_CH), jnp.float32),
            pltpu.VMEM((MOBA_HEADS, 2 * LANES, blk), jnp.bfloat16),
            pltpu.VMEM((MOBA_HEADS, 2 * blk, blk), jnp.float32),
            pltpu.VMEM((MOBA_HEADS, 2 * blk, blk), jnp.float32),
            pltpu.VMEM((MOBA_HEADS, SUBLANES, blk), jnp.float32),
            pltpu.VMEM((MOBA_HEADS, SUBLANES, blk), jnp.float32),
            pltpu.VMEM((MOBA_HEADS, VT_ROWS, blk), jnp.float32),
            pltpu.VMEM((MOBA_HEADS, SUBLANES, blk), jnp.float32),
        ],
        compiler_params=pltpu.CompilerParams(
            dimension_semantics=("arbitrary", "arbitrary"),
            vmem_limit_bytes=VMEM_LIMIT_MIXER),
        name="mixer",
    )(x, vec, w_main, w_qvt, qgt, gsum, memk, memv, _bf16(w_br_conv), _bf16(w_br_moba), _bf16(w_br_mem), _bf16(w_o))


def _ffn_kernel(x_ref, g_ref, wup_ref, cw_ref, cb_ref, wdown_ref, out_ref, aext_s, act_s):
    t = pl.program_id(1)
    tm = FFN_TILE
    x = x_ref[0]
    hb = _bf16(_rmsnorm(x, g_ref[...]))

    @pl.when(t == 0)
    def _():
        aext_s[0:SUBLANES, :] = jnp.zeros((SUBLANES, D_FF), jnp.float32)

    for c0, cw in FFN_CHUNKS:
        cs = slice(c0, c0 + cw)
        a = _dot(hb, wup_ref[:, c0:c0 + cw])
        b = _dot(hb, wup_ref[:, D_FF + c0:D_FF + c0 + cw])
        aext_s[SUBLANES:SUBLANES + tm, cs] = a
        a1 = aext_s[SUBLANES - 1:SUBLANES - 1 + tm, cs]
        a2 = aext_s[SUBLANES - 2:SUBLANES - 2 + tm, cs]
        aext_s[0:SUBLANES, cs] = a[tm - SUBLANES:, :]
        ac = a2 * cw_ref[0:1, cs] + a1 * cw_ref[1:2, cs] + a * cw_ref[2:3, cs] + cb_ref[:, cs]
        act_s[:, cs] = _bf16(ac * jax.nn.sigmoid(ac) * b)
    out_ref[0] = x + _dot(act_s[...], wdown_ref[...])


def _ffn(x, g_ffn, w_up, ffn_conv_w, ffn_conv_b, w_down):
    bsz, s, d = x.shape
    tm = FFN_TILE
    assert s % tm == 0
    return pl.pallas_call(
        _ffn_kernel,
        grid=(bsz, s // tm),
        in_specs=[
            pl.BlockSpec((1, tm, d), lambda b, t: (b, t, 0)),
            _resident((1, d)),
            _resident((d, 2 * D_FF)),
            _resident((3, D_FF)),
            _resident((1, D_FF)),
            _resident((D_FF, d)),
        ],
        out_specs=pl.BlockSpec((1, tm, d), lambda b, t: (b, t, 0)),
        out_shape=jax.ShapeDtypeStruct((bsz, s, d), jnp.float32),
        scratch_shapes=[
            pltpu.VMEM((SUBLANES + tm, D_FF), jnp.float32),
            pltpu.VMEM((tm, D_FF), jnp.bfloat16),
        ],
        compiler_params=pltpu.CompilerParams(
            dimension_semantics=("arbitrary", "arbitrary"),
            vmem_limit_bytes=VMEM_LIMIT_FFN),
        name="ffn",
    )(x, g_ffn.reshape(1, d), _bf16(w_up), ffn_conv_w, ffn_conv_b.reshape(1, D_FF), _bf16(w_down))


def kernel(x, mem, g_mix, w_in, b_gate, conv_w, conv_b, moba_q_gain, moba_k_gain, g_mem, w_mem_kv, memq_gain,
           memk_gain, w_br_conv, w_br_moba, w_br_mem, w_o, g_ffn, w_up, ffn_conv_w, ffn_conv_b, w_down):
    memk, memv = _memkv(mem, g_mem, w_mem_kv, memk_gain)
    x = _mixer(x, memk, memv, g_mix, w_in, b_gate, conv_w, conv_b, moba_q_gain, moba_k_gain, memq_gain,
               w_br_conv, w_br_moba, w_br_mem, w_o)
    return _ffn(x, g_ffn, w_up, ffn_conv_w, ffn_conv_b, w_down)
```

```python
import math

import jax
import jax.numpy as jnp
from jax import lax
from jax.experimental import pallas as pl
from jax.experimental.pallas import tpu as pltpu

D_MODEL = 1024
MEM_LEN = 256
CONV_CH = 512
MOBA_HEADS = 8
MOBA_HEAD_DIM = 64
MOBA_WIDTH = MOBA_HEADS * MOBA_HEAD_DIM
MOBA_BLOCK = 256
MOBA_TOPK = 3
MEM_HEADS = 4
MEM_HEAD_DIM = 128
MEM_WIDTH = MEM_HEADS * MEM_HEAD_DIM
D_FF = 2816
EPS = 1e-6

W_CONV = 0
W_Q = 3 * CONV_CH
W_K = W_Q + MOBA_WIDTH
W_V = W_K + MOBA_WIDTH
W_QMEM = W_V + MOBA_WIDTH
W_GATE = W_QMEM + MEM_WIDTH
IN_COLS = W_GATE + 3 * D_MODEL

LANES = 128
SUBLANES = 8
BF16_ROWS = 16
PAIR = 2 * MOBA_HEAD_DIM
VT_ROWS = MOBA_HEAD_DIM + BF16_ROWS
MASKED = -1e30
VEC_GMIX, VEC_BGATE, VEC_CONV01, VEC_CONV2B, VEC_GAINS, VEC_ROWS = 0, 1, 4, 5, 6, 16
MIXER_SUBS = 2
VMEM_LIMIT_MIXER = 60 * 1024 * 1024
VMEM_LIMIT_FFN = 48 * 1024 * 1024
FFN_TILE = 512
FFN_CHUNKS = ((0, 1024), (1024, 1024), (2048, 768))

_NT = (((1,), (1,)), ((), ()))


def _dot(a, b):
    return jnp.dot(a, b, preferred_element_type=jnp.float32)


def _dot_nt(a, b):
    return lax.dot_general(a, b, _NT, preferred_element_type=jnp.float32)


def _rmsnorm(x, g):
    return x * lax.rsqrt(jnp.mean(x * x, axis=-1, keepdims=True) + EPS) * g


def _bf16(x):
    return x.astype(jnp.bfloat16)


def _memkv_kernel(mem_ref, g_ref, w_ref, kg_ref, k_ref, v_ref):
    mn = _rmsnorm(mem_ref[0], g_ref[...])
    kv = _dot(_bf16(mn), w_ref[...])
    for h in range(MEM_HEADS):
        lo = h * MEM_HEAD_DIM
        kh = _rmsnorm(kv[:, lo:lo + MEM_HEAD_DIM], kg_ref[...])
        k_ref[0, :, lo:lo + MEM_HEAD_DIM] = _bf16(kh)
    v_ref[0] = _bf16(kv[:, MEM_WIDTH:])


def _memkv(mem, g_mem, w_mem_kv, memk_gain):
    bsz, m, d = mem.shape
    const = lambda b: (0, 0)
    return pl.pallas_call(
        _memkv_kernel,
        grid=(bsz,),
        in_specs=[
            pl.BlockSpec((1, m, d), lambda b: (b, 0, 0)),
            pl.BlockSpec((1, d), const),
            pl.BlockSpec((d, 2 * MEM_WIDTH), const),
            pl.BlockSpec((1, MEM_HEAD_DIM), const),
        ],
        out_specs=[
            pl.BlockSpec((1, m, MEM_WIDTH), lambda b: (b, 0, 0)),
            pl.BlockSpec((1, m, MEM_WIDTH), lambda b: (b, 0, 0)),
        ],
        out_shape=[
            jax.ShapeDtypeStruct((bsz, m, MEM_WIDTH), jnp.bfloat16),
            jax.ShapeDtypeStruct((bsz, m, MEM_WIDTH), jnp.bfloat16),
        ],
        name="memkv",
    )(mem, g_mem.reshape(1, d), _bf16(w_mem_kv), memk_gain.reshape(1, MEM_HEAD_DIM))


def _mixer_kernel(x_ref, vec_ref, wmain_ref, wqvt_ref, qgt_ref, gsum_ref,
                  memk_ref, memv_ref, wbc_ref, wbm_ref, wbx_ref, wo_ref,
                  out_ref,
                  k_s, vt_s, kbar_s, uext_s, rhs_s, sa_s, sb_s, ma_s, mb_s, acc_s, m_s):
    @pl.when(pl.program_id(1) == 0)
    def _():
        uext_s[0:SUBLANES, :] = jnp.zeros((SUBLANES, CONV_CH), jnp.float32)
        kbar_s[...] = jnp.zeros(kbar_s.shape, jnp.float32)

    for sub in range(MIXER_SUBS):
        _mixer_block(pl.program_id(1) * MIXER_SUBS + sub, slice(sub * MOBA_BLOCK, (sub + 1) * MOBA_BLOCK), sub % 2 == 1,
                     x_ref, vec_ref, wmain_ref, wqvt_ref, qgt_ref, gsum_ref,
                     memk_ref, memv_ref, wbc_ref, wbm_ref, wbx_ref, wo_ref,
                     out_ref,
                     k_s, vt_s, kbar_s, uext_s, rhs_s, sa_s, sb_s, ma_s, mb_s, acc_s, m_s)


def _mixer_block(i, rows, odd, x_ref, vec_ref, wmain_ref, wqvt_ref, qgt_ref, gsum_ref,
                 memk_ref, memv_ref, wbc_ref, wbm_ref, wbx_ref, wo_ref,
                 out_ref,
                 k_s, vt_s, kbar_s, uext_s, rhs_s, sa_s, sb_s, ma_s, mb_s, acc_s, m_s):
    blk = MOBA_BLOCK
    nbr = kbar_s.shape[0]
    x = x_ref[0, rows, :]
    hb = _bf16(_rmsnorm(x, vec_ref[VEC_GMIX:VEC_GMIX + 1, :]))

    def proj(lo, width):
        return _dot(hb, wmain_ref[:, lo:lo + width])

    def gate(g, n):
        return jax.nn.sigmoid(g + vec_ref[VEC_BGATE + n:VEC_BGATE + n + 1, :])

    pc = proj(W_CONV, 3 * CONV_CH)
    k_raw = proj(W_K, MOBA_WIDTH)
    qvt = _dot_nt(wqvt_ref[...], hb)
    qm = proj(W_QMEM, MEM_WIDTH)

    u = pc[:, CONV_CH:2 * CONV_CH] * pc[:, 2 * CONV_CH:]
    uext_s[SUBLANES:SUBLANES + blk, :] = u
    u1 = uext_s[SUBLANES - 1:SUBLANES - 1 + blk, :]
    u2 = uext_s[SUBLANES - 2:SUBLANES - 2 + blk, :]
    uext_s[0:SUBLANES, :] = u[blk - SUBLANES:, :]
    conv = (u2 * vec_ref[VEC_CONV01:VEC_CONV01 + 1, :CONV_CH] + u1 * vec_ref[VEC_CONV01:VEC_CONV01 + 1, CONV_CH:]
            + u * vec_ref[VEC_CONV2B:VEC_CONV2B + 1, :CONV_CH] + vec_ref[VEC_CONV2B:VEC_CONV2B + 1, CONV_CH:])
    y_conv = _bf16(pc[:, :CONV_CH] * conv)

    k2_b = _bf16(k_raw * k_raw)

    q3 = qvt[:MOBA_WIDTH].reshape(MOBA_HEADS, MOBA_HEAD_DIM, blk)
    ssq = jnp.sum(q3 * q3, axis=1, keepdims=True)
    qbt = _bf16((q3 * lax.rsqrt(ssq * (1.0 / MOBA_HEAD_DIM) + EPS)).reshape(MOBA_WIDTH, blk) * qgt_ref[...])
    vbt = _bf16(qvt[MOBA_WIDTH:])

    mqg = vec_ref[VEC_GAINS:VEC_GAINS + 1, MOBA_WIDTH:MOBA_WIDTH + MEM_HEAD_DIM]
    qmb = [_bf16(_rmsnorm(qm[:, h * MEM_HEAD_DIM:(h + 1) * MEM_HEAD_DIM], mqg)) for h in range(MEM_HEADS)]

    g0 = proj(W_GATE, D_MODEL)
    kss = _dot(k2_b, gsum_ref[...])
    p_conv = _dot(y_conv, wbc_ref[...])
    s_mem = [_dot_nt(qmb[h], memk_ref[0, :, h * MEM_HEAD_DIM:(h + 1) * MEM_HEAD_DIM]) for h in range(MEM_HEADS)]
    g2 = proj(W_GATE + 2 * D_MODEL, D_MODEL)

    merged = gate(g0, 0) * p_conv
    kn = k_raw * lax.rsqrt(kss * (1.0 / MOBA_HEAD_DIM) + EPS) * vec_ref[VEC_GAINS:VEC_GAINS + 1, :MOBA_WIDTH]
    kb = _bf16(kn)
    k_s[i] = kb
    kbar_row = jnp.mean(kn, axis=0, keepdims=True)
    kb_rows = lax.broadcasted_iota(jnp.int32, kbar_s.shape, 0)
    kbar_prev = kbar_s[...]
    kbar_s[...] = jnp.where(kb_rows == i, kbar_row, kbar_prev)
    kbar_b = _bf16(kbar_prev)

    zeros_h = jnp.zeros((MOBA_HEAD_DIM, blk), jnp.bfloat16)
    qpairs = []
    for h in range(MOBA_HEADS):
        qh = qbt[h * MOBA_HEAD_DIM:(h + 1) * MOBA_HEAD_DIM]
        qpairs.append(jnp.concatenate([qh, zeros_h] if h % 2 == 0 else [zeros_h, qh], axis=0))

    p_mem = [jnp.exp(s - jnp.max(s, axis=-1, keepdims=True)) for s in s_mem]

    pair_cols = lambda h: slice((h // 2) * PAIR, (h // 2 + 1) * PAIR)
    gate_own = [_dot(jnp.concatenate([kbar_b[:, pair_cols(h)], kb[:, pair_cols(h)]]
                                     + ([k_s[i - 1, :, pair_cols(h)]] if odd else []), axis=0), qpairs[h])
                for h in range(MOBA_HEADS)]
    gates = [g[:nbr] for g in gate_own]
    s_own = [g[nbr:nbr + blk] for g in gate_own]
    s_prev = [g[nbr + blk:] for g in gate_own]
    o_mem = [_dot(_bf16(p_mem[h]), memv_ref[0, :, h * MEM_HEAD_DIM:(h + 1) * MEM_HEAD_DIM])
             / jnp.sum(p_mem[h], axis=-1, keepdims=True) for h in range(MEM_HEADS)]
    merged = merged + gate(g2, 2) * _dot(_bf16(jnp.concatenate(o_mem, axis=-1)), wbx_ref[...])

    lane = lax.broadcasted_iota(jnp.int32, (blk, LANES), 1)
    brow = lax.broadcasted_iota(jnp.int32, (nbr, blk), 0)
    browf = brow.astype(jnp.float32)
    neg_inf = jnp.float32(-jnp.inf)
    sel_pad = jnp.full((LANES - nbr, blk), MASKED, jnp.bfloat16)
    ones_r = jnp.ones((BF16_ROWS, blk), jnp.bfloat16)
    vts = []
    bias_prev = []
    for h in range(MOBA_HEADS):
        g = jnp.where(brow < i, gates[h], neg_inf)
        sel = jnp.full((nbr, blk), MASKED, jnp.float32)
        for _ in range(MOBA_TOPK):
            mx = jnp.max(g, axis=0, keepdims=True)
            idx = jnp.min(jnp.where(g == mx, browf, float(nbr)), axis=0, keepdims=True)
            pick = browf == idx
            sel = jnp.where(pick & (mx > neg_inf), 0.0, sel)
            g = jnp.where(pick, neg_inf, g)
        rhs_s[h] = jnp.concatenate([qpairs[h], _bf16(sel), sel_pad], axis=0)
        if odd:
            bias_prev.append(jnp.sum(jnp.where(brow == i - 1, sel, 0.0), axis=0, keepdims=True))
        vt = jnp.concatenate([vbt[h * MOBA_HEAD_DIM:(h + 1) * MOBA_HEAD_DIM], ones_r], axis=0)
        vt_s[i, h] = vt
        vts.append(vt)

    def pair_scores(dst, h, tp):
        dst_s, dst_mx = dst
        parts = []
        for jn in (2 * tp, 2 * tp + 1):
            en = jnp.where(lane == jnp.where(jn < i, jn, nbr - 1), 1.0, 0.0).astype(jnp.bfloat16)
            parts.append(jnp.concatenate([k_s[jnp.minimum(jn, i), :, pair_cols(h)], en], axis=1))
        s = _dot(jnp.concatenate(parts, axis=0)[:, :PAIR + nbr], rhs_s[h, 0:PAIR + nbr, :])
        dst_s[h] = s
        dst_mx[h] = jnp.broadcast_to(jnp.max(s, axis=0, keepdims=True), (SUBLANES, blk))

    causal = (lax.broadcasted_iota(jnp.int32, (blk, blk), 0) <= lax.broadcasted_iota(jnp.int32, (blk, blk), 1))
    set_a, set_b = (sa_s, ma_s), (sb_s, mb_s)
    for h in range(MOBA_HEADS):
        pair_scores(set_a, h, 0)
        s = jnp.where(causal, s_own[h], MASKED)
        vt = vts[h]
        if odd:
            s = jnp.concatenate([s_prev[h] + bias_prev[h], s], axis=0)
            vt = jnp.concatenate([vt_s[i - 1, h], vt], axis=1)
        m0 = jnp.max(s, axis=0, keepdims=True)
        acc_s[h] = _dot(vt, jnp.exp2(_bf16(s - m0)))
        m_s[h] = jnp.broadcast_to(m0, (SUBLANES, blk))

    def attend(src, dst, tp):
        src_s, src_mx = src
        ja = jnp.minimum(2 * tp, i)
        jb = jnp.minimum(2 * tp + 1, i)
        for h in range(MOBA_HEADS):
            if dst is not None:
                pair_scores(dst, h, tp + 1)
            m_old = m_s[h]
            m_new = jnp.maximum(m_old, src_mx[h])
            p = jnp.exp2(_bf16(src_s[h] - m_new[0:1]))
            pv = _dot(jnp.concatenate([vt_s[ja, h], vt_s[jb, h]], axis=1), p)
            acc_s[h] = acc_s[h] * jnp.exp2(m_old[0:1] - m_new[0:1]) + pv
            m_s[h] = m_new

    n_pairs = i // 2

    def two_pairs(u, carry):
        attend(set_a, set_b, 2 * u)
        attend(set_b, set_a, 2 * u + 1)
        return carry

    lax.fori_loop(0, n_pairs // 2, two_pairs, 0)

    @pl.when(n_pairs % 2 == 1)
    def _():
        attend(set_a, None, n_pairs - 1)

    g1 = proj(W_GATE + D_MODEL, D_MODEL)
    outs = []
    for h in range(MOBA_HEADS):
        a = acc_s[h]
        outs.append(a[:MOBA_HEAD_DIM] / a[MOBA_HEAD_DIM:MOBA_HEAD_DIM + 1])
    y_moba = jnp.concatenate(outs, axis=0).T
    merged = merged + gate(g1, 1) * _dot(_bf16(y_moba), wbm_ref[...])

    out_ref[0, rows, :] = x + _dot(_bf16(merged), wo_ref[...])


def _resident(shape):
    return pl.BlockSpec(shape, lambda b, i: (0,) * len(shape), pipeline_mode=pl.Buffered(1))


def _mixer(x, memk, memv, g_mix, w_in, b_gate, conv_w, conv_b, moba_q_gain, moba_k_gain, memq_gain,
           w_br_conv, w_br_moba, w_br_mem, w_o):
    bsz, s, d = x.shape
    blk = MOBA_BLOCK
    assert s % (MIXER_SUBS * blk) == 0 and d == D_MODEL and MIXER_SUBS % 2 == 0
    nb = s // blk
    nbr = pl.cdiv(nb, BF16_ROWS) * BF16_ROWS
    assert nbr < LANES
    w_main = _bf16(w_in)
    w_qv = lax.optimization_barrier(jnp.concatenate([w_in[:, W_Q:W_K], w_in[:, W_V:W_QMEM]], axis=1))
    w_qvt = _bf16(w_qv.T)
    qgt = jnp.broadcast_to(
        (jnp.tile(moba_q_gain, MOBA_HEADS) * (MOBA_HEAD_DIM ** -0.5 * math.log2(math.e)))[:, None], (MOBA_WIDTH, blk))
    gains = jnp.concatenate([jnp.tile(moba_k_gain, MOBA_HEADS), memq_gain * (MEM_HEAD_DIM ** -0.5),
                             jnp.zeros((d - MOBA_WIDTH - MEM_HEAD_DIM,), jnp.float32)])
    vec = jnp.concatenate([g_mix.reshape(1, d), b_gate.reshape(3, d), conv_w[0:2].reshape(1, d),
                           jnp.concatenate([conv_w[2], conv_b]).reshape(1, d), gains.reshape(1, d),
                           jnp.zeros((VEC_ROWS - VEC_GAINS - 1, d), jnp.float32)], axis=0)
    head_of = jnp.arange(MOBA_WIDTH) // MOBA_HEAD_DIM
    gsum = (head_of[:, None] == head_of[None, :]).astype(jnp.bfloat16)
    return pl.pallas_call(
        _mixer_kernel,
        grid=(bsz, nb // MIXER_SUBS),
        in_specs=[
            pl.BlockSpec((1, MIXER_SUBS * blk, d), lambda b, i: (b, i, 0)),
            _resident((VEC_ROWS, d)),
            _resident((d, IN_COLS)),
            _resident((2 * MOBA_WIDTH, d)),
            _resident((MOBA_WIDTH, blk)),
            _resident((MOBA_WIDTH, MOBA_WIDTH)),
            pl.BlockSpec((1, MEM_LEN, MEM_WIDTH), lambda b, i: (b, 0, 0)),
            pl.BlockSpec((1, MEM_LEN, MEM_WIDTH), lambda b, i: (b, 0, 0)),
            _resident((CONV_CH, d)),
            _resident((MOBA_WIDTH, d)),
            _resident((MEM_WIDTH, d)),
            _resident((d, d)),
        ],
        out_specs=pl.BlockSpec((1, MIXER_SUBS * blk, d), lambda b, i: (b, i, 0)),
        out_shape=jax.ShapeDtypeStruct((bsz, s, d), jnp.float32),
        scratch_shapes=[
            pltpu.VMEM((nb, blk, MOBA_WIDTH), jnp.bfloat16),
            pltpu.VMEM((nb, MOBA_HEADS, VT_ROWS, blk), jnp.bfloat16),
            pltpu.VMEM((nbr, MOBA_WIDTH), jnp.float32),
            pltpu.VMEM((SUBLANES + blk, CONV_CH), jnp.float32),
            pltpu.VMEM((MOBA_HEADS, 2 * LANES, blk), jnp.bfloat16),
            pltpu.VMEM((MOBA_HEADS, 2 * blk, blk), jnp.float32),
            pltpu.VMEM((MOBA_HEADS, 2 * blk, blk), jnp.float32),
            pltpu.VMEM((MOBA_HEADS, SUBLANES, blk), jnp.float32),
            pltpu.VMEM((MOBA_HEADS, SUBLANES, blk), jnp.float32),
            pltpu.VMEM((MOBA_HEADS, VT_ROWS, blk), jnp.float32),
            pltpu.VMEM((MOBA_HEADS, SUBLANES, blk), jnp.float32),
        ],
        compiler_params=pltpu.CompilerParams(
            dimension_semantics=("arbitrary", "arbitrary"),
            vmem_limit_bytes=VMEM_LIMIT_MIXER),
        name="mixer",
    )(x, vec, w_main, w_qvt, qgt, gsum, memk, memv, _bf16(w_br_conv), _bf16(w_br_moba), _bf16(w_br_mem), _bf16(w_o))


def _ffn_kernel(x_ref, g_ref, wup_ref, cw_ref, cb_ref, wdown_ref, out_ref, aext_s, act_s):
    t = pl.program_id(1)
    tm = FFN_TILE
    x = x_ref[0]
    hb = _bf16(_rmsnorm(x, g_ref[...]))

    @pl.when(t == 0)
    def _():
        aext_s[0:SUBLANES, :] = jnp.zeros((SUBLANES, D_FF), jnp.float32)

    for c0, cw in FFN_CHUNKS:
        cs = slice(c0, c0 + cw)
        a = _dot(hb, wup_ref[:, c0:c0 + cw])
        b = _dot(hb, wup_ref[:, D_FF + c0:D_FF + c0 + cw])
        aext_s[SUBLANES:SUBLANES + tm, cs] = a
        a1 = aext_s[SUBLANES - 1:SUBLANES - 1 + tm, cs]
        a2 = aext_s[SUBLANES - 2:SUBLANES - 2 + tm, cs]
        aext_s[0:SUBLANES, cs] = a[tm - SUBLANES:, :]
        ac = a2 * cw_ref[0:1, cs] + a1 * cw_ref[1:2, cs] + a * cw_ref[2:3, cs] + cb_ref[:, cs]
        act_s[:, cs] = _bf16(ac * jax.nn.sigmoid(ac) * b)
    out_ref[0] = x + _dot(act_s[...], wdown_ref[...])


def _ffn(x, g_ffn, w_up, ffn_conv_w, ffn_conv_b, w_down):
    bsz, s, d = x.shape
    tm = FFN_TILE
    assert s % tm == 0
    return pl.pallas_call(
        _ffn_kernel,
        grid=(bsz, s // tm),
        in_specs=[
            pl.BlockSpec((1, tm, d), lambda b, t: (b, t, 0)),
            _resident((1, d)),
            _resident((d, 2 * D_FF)),
            _resident((3, D_FF)),
            _resident((1, D_FF)),
            _resident((D_FF, d)),
        ],
        out_specs=pl.BlockSpec((1, tm, d), lambda b, t: (b, t, 0)),
        out_shape=jax.ShapeDtypeStruct((bsz, s, d), jnp.float32),
        scratch_shapes=[
            pltpu.VMEM((SUBLANES + tm, D_FF), jnp.float32),
            pltpu.VMEM((tm, D_FF), jnp.bfloat16),
        ],
        compiler_params=pltpu.CompilerParams(
            dimension_semantics=("arbitrary", "arbitrary"),
            vmem_limit_bytes=VMEM_LIMIT_FFN),
        name="ffn",
    )(x, g_ffn.reshape(1, d), _bf16(w_up), ffn_conv_w, ffn_conv_b.reshape(1, D_FF), _bf16(w_down))


def kernel(x, mem, g_mix, w_in, b_gate, conv_w, conv_b, moba_q_gain, moba_k_gain, g_mem, w_mem_kv, memq_gain,
           memk_gain, w_br_conv, w_br_moba, w_br_mem, w_o, g_ffn, w_up, ffn_conv_w, ffn_conv_b, w_down):
    memk, memv = _memkv(mem, g_mem, w_mem_kv, memk_gain)
    x = _mixer(x, memk, memv, g_mix, w_in, b_gate, conv_w, conv_b, moba_q_gain, moba_k_gain, memq_gain,
               w_br_conv, w_br_moba, w_br_mem, w_o)
    return _ffn(x, g_ffn, w_up, ffn_conv_w, ffn_conv_b, w_down)
```

```python
import math

import jax
import jax.numpy as jnp
from jax import lax
from jax.experimental import pallas as pl
from jax.experimental.pallas import tpu as pltpu

D_MODEL = 1024
MEM_LEN = 256
CONV_CH = 512
MOBA_HEADS = 8
MOBA_HEAD_DIM = 64
MOBA_WIDTH = MOBA_HEADS * MOBA_HEAD_DIM
MOBA_BLOCK = 256
MOBA_TOPK = 3
MEM_HEADS = 4
MEM_HEAD_DIM = 128
MEM_WIDTH = MEM_HEADS * MEM_HEAD_DIM
D_FF = 2816
EPS = 1e-6

W_CONV = 0
W_Q = 3 * CONV_CH
W_K = W_Q + MOBA_WIDTH
W_V = W_K + MOBA_WIDTH
W_QMEM = W_V + MOBA_WIDTH
W_GATE = W_QMEM + MEM_WIDTH
IN_COLS = W_GATE + 3 * D_MODEL

LANES = 128
SUBLANES = 8
BF16_ROWS = 16
PAIR = 2 * MOBA_HEAD_DIM
VT_ROWS = MOBA_HEAD_DIM + BF16_ROWS
MASKED = -1e30
VEC_GMIX, VEC_BGATE, VEC_CONV01, VEC_CONV2B, VEC_GAINS, VEC_ROWS = 0, 1, 4, 5, 6, 16
MIXER_SUBS = 2
VMEM_LIMIT_MIXER = 60 * 1024 * 1024
VMEM_LIMIT_FFN = 48 * 1024 * 1024
FFN_TILE = 512
FFN_CHUNKS = ((0, 1024), (1024, 1024), (2048, 768))

_NT = (((1,), (1,)), ((), ()))


def _dot(a, b):
    return jnp.dot(a, b, preferred_element_type=jnp.float32)


def _dot_nt(a, b):
    return lax.dot_general(a, b, _NT, preferred_element_type=jnp.float32)


def _rmsnorm(x, g):
    return x * lax.rsqrt(jnp.mean(x * x, axis=-1, keepdims=True) + EPS) * g


def _bf16(x):
    return x.astype(jnp.bfloat16)


def _memkv_kernel(mem_ref, g_ref, w_ref, kg_ref, k_ref, v_ref):
    mn = _rmsnorm(mem_ref[0], g_ref[...])
    kv = _dot(_bf16(mn), w_ref[...])
    for h in range(MEM_HEADS):
        lo = h * MEM_HEAD_DIM
        kh = _rmsnorm(kv[:, lo:lo + MEM_HEAD_DIM], kg_ref[...])
        k_ref[0, :, lo:lo + MEM_HEAD_DIM] = _bf16(kh)
    v_ref[0] = _bf16(kv[:, MEM_WIDTH:])


def _memkv(mem, g_mem, w_mem_kv, memk_gain):
    bsz, m, d = mem.shape
    const = lambda b: (0, 0)
    return pl.pallas_call(
        _memkv_kernel,
        grid=(bsz,),
        in_specs=[
            pl.BlockSpec((1, m, d), lambda b: (b, 0, 0)),
            pl.BlockSpec((1, d), const),
            pl.BlockSpec((d, 2 * MEM_WIDTH), const),
            pl.BlockSpec((1, MEM_HEAD_DIM), const),
        ],
        out_specs=[
            pl.BlockSpec((1, m, MEM_WIDTH), lambda b: (b, 0, 0)),
            pl.BlockSpec((1, m, MEM_WIDTH), lambda b: (b, 0, 0)),
        ],
        out_shape=[
            jax.ShapeDtypeStruct((bsz, m, MEM_WIDTH), jnp.bfloat16),
            jax.ShapeDtypeStruct((bsz, m, MEM_WIDTH), jnp.bfloat16),
        ],
        name="memkv",
    )(mem, g_mem.reshape(1, d), _bf16(w_mem_kv), memk_gain.reshape(1, MEM_HEAD_DIM))


def _mixer_kernel(x_ref, vec_ref, wmain_ref, wqvt_ref, qgt_ref, gsum_ref,
                  memk_ref, memv_ref, wbc_ref, wbm_ref, wbx_ref, wo_ref,
                  out_ref,
                  k_s, vt_s, kbar_s, uext_s, rhs_s, sa_s, sb_s, ma_s, mb_s, acc_s, m_s):
    @pl.when(pl.program_id(1) == 0)
    def _():
        uext_s[0:SUBLANES, :] = jnp.zeros((SUBLANES, CONV_CH), jnp.float32)
        kbar_s[...] = jnp.zeros(kbar_s.shape, jnp.float32)

    for sub in range(MIXER_SUBS):
        _mixer_block(pl.program_id(1) * MIXER_SUBS + sub, slice(sub * MOBA_BLOCK, (sub + 1) * MOBA_BLOCK), sub % 2 == 1,
                     x_ref, vec_ref, wmain_ref, wqvt_ref, qgt_ref, gsum_ref,
                     memk_ref, memv_ref, wbc_ref, wbm_ref, wbx_ref, wo_ref,
                     out_ref,
                     k_s, vt_s, kbar_s, uext_s, rhs_s, sa_s, sb_s, ma_s, mb_s, acc_s, m_s)


def _mixer_block(i, rows, odd, x_ref, vec_ref, wmain_ref, wqvt_ref, qgt_ref, gsum_ref,
                 memk_ref, memv_ref, wbc_ref, wbm_ref, wbx_ref, wo_ref,
                 out_ref,
                 k_s, vt_s, kbar_s, uext_s, rhs_s, sa_s, sb_s, ma_s, mb_s, acc_s, m_s):
    blk = MOBA_BLOCK
    nbr = kbar_s.shape[0]
    x = x_ref[0, rows, :]
    hb = _bf16(_rmsnorm(x, vec_ref[VEC_GMIX:VEC_GMIX + 1, :]))

    def proj(lo, width):
        return _dot(hb, wmain_ref[:, lo:lo + width])

    def gate(g, n):
        return jax.nn.sigmoid(g + vec_ref[VEC_BGATE + n:VEC_BGATE + n + 1, :])

    pc = proj(W_CONV, 3 * CONV_CH)
    k_raw = proj(W_K, MOBA_WIDTH)
    qvt = _dot_nt(wqvt_ref[...], hb)
    qm = proj(W_QMEM, MEM_WIDTH)

    u = pc[:, CONV_CH:2 * CONV_CH] * pc[:, 2 * CONV_CH:]
    uext_s[SUBLANES:SUBLANES + blk, :] = u
    u1 = uext_s[SUBLANES - 1:SUBLANES - 1 + blk, :]
    u2 = uext_s[SUBLANES - 2:SUBLANES - 2 + blk, :]
    uext_s[0:SUBLANES, :] = u[blk - SUBLANES:, :]
    conv = (u2 * vec_ref[VEC_CONV01:VEC_CONV01 + 1, :CONV_CH] + u1 * vec_ref[VEC_CONV01:VEC_CONV01 + 1, CONV_CH:]
            + u * vec_ref[VEC_CONV2B:VEC_CONV2B + 1, :CONV_CH] + vec_ref[VEC_CONV2B:VEC_CONV2B + 1, CONV_CH:])
    y_conv = _bf16(pc[:, :CONV_CH] * conv)

    k2_b = _bf16(k_raw * k_raw)

    q3 = qvt[:MOBA_WIDTH].reshape(MOBA_HEADS, MOBA_HEAD_DIM, blk)
    ssq = jnp.sum(q3 * q3, axis=1, keepdims=True)
    qbt = _bf16((q3 * lax.rsqrt(ssq * (1.0 / MOBA_HEAD_DIM) + EPS)).reshape(MOBA_WIDTH, blk) * qgt_ref[...])
    vbt = _bf16(qvt[MOBA_WIDTH:])

    mqg = vec_ref[VEC_GAINS:VEC_GAINS + 1, MOBA_WIDTH:MOBA_WIDTH + MEM_HEAD_DIM]
    qmb = [_bf16(_rmsnorm(qm[:, h * MEM_HEAD_DIM:(h + 1) * MEM_HEAD_DIM], mqg)) for h in range(MEM_HEADS)]

    g0 = proj(W_GATE, D_MODEL)
    kss = _dot(k2_b, gsum_ref[...])
    p_conv = _dot(y_conv, wbc_ref[...])
    s_mem = [_dot_nt(qmb[h], memk_ref[0, :, h * MEM_HEAD_DIM:(h + 1) * MEM_HEAD_DIM]) for h in range(MEM_HEADS)]
    g2 = proj(W_GATE + 2 * D_MODEL, D_MODEL)

    merged = gate(g0, 0) * p_conv
    kn = k_raw * lax.rsqrt(kss * (1.0 / MOBA_HEAD_DIM) + EPS) * vec_ref[VEC_GAINS:VEC_GAINS + 1, :MOBA_WIDTH]
    kb = _bf16(kn)
    k_s[i] = kb
    kbar_row = jnp.mean(kn, axis=0, keepdims=True)
    kb_rows = lax.broadcasted_iota(jnp.int32, kbar_s.shape, 0)
    kbar_prev = kbar_s[...]
    kbar_s[...] = jnp.where(kb_rows == i, kbar_row, kbar_prev)
    kbar_b = _bf16(kbar_prev)

    zeros_h = jnp.zeros((MOBA_HEAD_DIM, blk), jnp.bfloat16)
    qpairs = []
    for h in range(MOBA_HEADS):
        qh = qbt[h * MOBA_HEAD_DIM:(h + 1) * MOBA_HEAD_DIM]
        qpairs.append(jnp.concatenate([qh, zeros_h] if h % 2 == 0 else [zeros_h, qh], axis=0))

    p_mem = [jnp.exp(s - jnp.max(s, axis=-1, keepdims=True)) for s in s_mem]

    pair_cols = lambda h: slice((h // 2) * PAIR, (h // 2 + 1) * PAIR)
    gate_own = [_dot(jnp.concatenate([kbar_b[:, pair_cols(h)], kb[:, pair_cols(h)]]
                                     + ([k_s[i - 1, :, pair_cols(h)]] if odd else []), axis=0), qpairs[h])
                for h in range(MOBA_HEADS)]
    gates = [g[:nbr] for g in gate_own]
    s_own = [g[nbr:nbr + blk] for g in gate_own]
    s_prev = [g[nbr + blk:] for g in gate_own]
    o_mem = [_dot(_bf16(p_mem[h]), memv_ref[0, :, h * MEM_HEAD_DIM:(h + 1) * MEM_HEAD_DIM])
             / jnp.sum(p_mem[h], axis=-1, keepdims=True) for h in range(MEM_HEADS)]
    merged = merged + gate(g2, 2) * _dot(_bf16(jnp.concatenate(o_mem, axis=-1)), wbx_ref[...])

    lane = lax.broadcasted_iota(jnp.int32, (blk, LANES), 1)
    brow = lax.broadcasted_iota(jnp.int32, (nbr, blk), 0)
    browf = brow.astype(jnp.float32)
    neg_inf = jnp.float32(-jnp.inf)
    sel_pad = jnp.full((LANES - nbr, blk), MASKED, jnp.bfloat16)
    ones_r = jnp.ones((BF16_ROWS, blk), jnp.bfloat16)
    vts = []
    bias_prev = []
    for h in range(MOBA_HEADS):
        g = jnp.where(brow < i, gates[h], neg_inf)
        sel = jnp.full((nbr, blk), MASKED, jnp.float32)
        for _ in range(MOBA_TOPK):
            mx = jnp.max(g, axis=0, keepdims=True)
            idx = jnp.min(jnp.where(g == mx, browf, float(nbr)), axis=0, keepdims=True)
            pick = browf == idx
            sel = jnp.where(pick & (mx > neg_inf), 0.0, sel)
            g = jnp.where(pick, neg_inf, g)
        rhs_s[h] = jnp.concatenate([qpairs[h], _bf16(sel), sel_pad], axis=0)
        if odd:
            bias_prev.append(jnp.sum(jnp.where(brow == i - 1, sel, 0.0), axis=0, keepdims=True))
        vt = jnp.concatenate([vbt[h * MOBA_HEAD_DIM:(h + 1) * MOBA_HEAD_DIM], ones_r], axis=0)
        vt_s[i, h] = vt
        vts.append(vt)

    def pair_scores(dst, h, tp):
        dst_s, dst_mx = dst
        parts = []
        for jn in (2 * tp, 2 * tp + 1):
            en = jnp.where(lane == jnp.where(jn < i, jn, nbr - 1), 1.0, 0.0).astype(jnp.bfloat16)
            parts.append(jnp.concatenate([k_s[jnp.minimum(jn, i), :, pair_cols(h)], en], axis=1))
        s = _dot(jnp.concatenate(parts, axis=0)[:, :PAIR + nbr], rhs_s[h, 0:PAIR + nbr, :])
        dst_s[h] = s
        dst_mx[h] = jnp.broadcast_to(jnp.max(s, axis=0, keepdims=True), (SUBLANES, blk))

    causal = (lax.broadcasted_iota(jnp.int32, (blk, blk), 0) <= lax.broadcasted_iota(jnp.int32, (blk, blk), 1))
    set_a, set_b = (sa_s, ma_s), (sb_s, mb_s)
    for h in range(MOBA_HEADS):
        pair_scores(set_a, h, 0)
        s = jnp.where(causal, s_own[h], MASKED)
        vt = vts[h]
        if odd:
            s = jnp.concatenate([s_prev[h] + bias_prev[h], s], axis=0)
            vt = jnp.concatenate([vt_s[i - 1, h], vt], axis=1)
        m0 = jnp.max(s, axis=0, keepdims=True)
        acc_s[h] = _dot(vt, jnp.exp2(_bf16(s - m0)))
        m_s[h] = jnp.broadcast_to(m0, (SUBLANES, blk))

    def attend(src, dst, tp):
        src_s, src_mx = src
        ja = jnp.minimum(2 * tp, i)
        jb = jnp.minimum(2 * tp + 1, i)
        for h in range(MOBA_HEADS):
            if dst is not None:
                pair_scores(dst, h, tp + 1)
            m_old = m_s[h]
            m_new = jnp.maximum(m_old, src_mx[h])
            p = jnp.exp2(_bf16(src_s[h] - m_new[0:1]))
            pv = _dot(jnp.concatenate([vt_s[ja, h], vt_s[jb, h]], axis=1), p)
            acc_s[h] = acc_s[h] * jnp.exp2(m_old[0:1] - m_new[0:1]) + pv
            m_s[h] = m_new

    n_pairs = i // 2

    def two_pairs(u, carry):
        attend(set_a, set_b, 2 * u)
        attend(set_b, set_a, 2 * u + 1)
        return carry

    n_loop = (n_pairs - 1) // 2
    lax.fori_loop(0, n_loop, two_pairs, 0)

    @pl.when(n_pairs % 2 == 1)
    def _():
        attend(set_a, None, n_pairs - 1)

    @pl.when((n_pairs % 2 == 0) & (n_pairs > 0))
    def _():
        attend(set_a, set_b, n_pairs - 2)
        attend(set_b, None, n_pairs - 1)

    g1 = proj(W_GATE + D_MODEL, D_MODEL)
    outs = []
    for h in range(MOBA_HEADS):
        a = acc_s[h]
        outs.append(a[:MOBA_HEAD_DIM] / a[MOBA_HEAD_DIM:MOBA_HEAD_DIM + 1])
    y_moba = jnp.concatenate(outs, axis=0).T
    merged = merged + gate(g1, 1) * _dot(_bf16(y_moba), wbm_ref[...])

    out_ref[0, rows, :] = x + _dot(_bf16(merged), wo_ref[...])


def _resident(shape):
    return pl.BlockSpec(shape, lambda b, i: (0,) * len(shape), pipeline_mode=pl.Buffered(1))


def _mixer(x, memk, memv, g_mix, w_in, b_gate, conv_w, conv_b, moba_q_gain, moba_k_gain, memq_gain,
           w_br_conv, w_br_moba, w_br_mem, w_o):
    bsz, s, d = x.shape
    blk = MOBA_BLOCK
    assert s % (MIXER_SUBS * blk) == 0 and d == D_MODEL and MIXER_SUBS % 2 == 0
    nb = s // blk
    nbr = pl.cdiv(nb, BF16_ROWS) * BF16_ROWS
    assert nbr < LANES
    w_main = _bf16(w_in)
    w_qv = lax.optimization_barrier(jnp.concatenate([w_in[:, W_Q:W_K], w_in[:, W_V:W_QMEM]], axis=1))
    w_qvt = _bf16(w_qv.T)
    qgt = jnp.broadcast_to(
        (jnp.tile(moba_q_gain, MOBA_HEADS) * (MOBA_HEAD_DIM ** -0.5 * math.log2(math.e)))[:, None], (MOBA_WIDTH, blk))
    gains = jnp.concatenate([jnp.tile(moba_k_gain, MOBA_HEADS), memq_gain * (MEM_HEAD_DIM ** -0.5),
                             jnp.zeros((d - MOBA_WIDTH - MEM_HEAD_DIM,), jnp.float32)])
    vec = jnp.concatenate([g_mix.reshape(1, d), b_gate.reshape(3, d), conv_w[0:2].reshape(1, d),
                           jnp.concatenate([conv_w[2], conv_b]).reshape(1, d), gains.reshape(1, d),
                           jnp.zeros((VEC_ROWS - VEC_GAINS - 1, d), jnp.float32)], axis=0)
    head_of = jnp.arange(MOBA_WIDTH) // MOBA_HEAD_DIM
    gsum = (head_of[:, None] == head_of[None, :]).astype(jnp.bfloat16)
    return pl.pallas_call(
        _mixer_kernel,
        grid=(bsz, nb // MIXER_SUBS),
        in_specs=[
            pl.BlockSpec((1, MIXER_SUBS * blk, d), lambda b, i: (b, i, 0)),
            _resident((VEC_ROWS, d)),
            _resident((d, IN_COLS)),
            _resident((2 * MOBA_WIDTH, d)),
            _resident((MOBA_WIDTH, blk)),
            _resident((MOBA_WIDTH, MOBA_WIDTH)),
            pl.BlockSpec((1, MEM_LEN, MEM_WIDTH), lambda b, i: (b, 0, 0)),
            pl.BlockSpec((1, MEM_LEN, MEM_WIDTH), lambda b, i: (b, 0, 0)),
            _resident((CONV_CH, d)),
            _resident((MOBA_WIDTH, d)),
            _resident((MEM_WIDTH, d)),
            _resident((d, d)),
        ],
        out_specs=pl.BlockSpec((1, MIXER_SUBS * blk, d), lambda b, i: (b, i, 0)),
        out_shape=jax.ShapeDtypeStruct((bsz, s, d), jnp.float32),
        scratch_shapes=[
            pltpu.VMEM((nb, blk, MOBA_WIDTH), jnp.bfloat16),
            pltpu.VMEM((nb, MOBA_HEADS, VT_ROWS, blk), jnp.bfloat16),
            pltpu.VMEM((nbr, MOBA_WIDTH), jnp.float32),
            pltpu.VMEM((SUBLANES + blk, CONV_CH), jnp.float32),
            pltpu.VMEM((MOBA_HEADS, 2 * LANES, blk), jnp.bfloat16),
            pltpu.VMEM((MOBA_HEADS, 2 * blk, blk), jnp.float32),
            pltpu.VMEM((MOBA_HEADS, 2 * blk, blk), jnp.float32),
            pltpu.VMEM((MOBA_HEADS, SUBLANES, blk), jnp.float32),
            pltpu.VMEM((MOBA_HEADS, SUBLANES, blk), jnp.float32),
            pltpu.VMEM((MOBA_HEADS, VT_ROWS, blk), jnp.float32),
            pltpu.VMEM((MOBA_HEADS, SUBLANES, blk), jnp.float32),
        ],
        compiler_params=pltpu.CompilerParams(
            dimension_semantics=("arbitrary", "arbitrary"),
            vmem_limit_bytes=VMEM_LIMIT_MIXER),
        name="mixer",
    )(x, vec, w_main, w_qvt, qgt, gsum, memk, memv, _bf16(w_br_conv), _bf16(w_br_moba), _bf16(w_br_mem), _bf16(w_o))


def _ffn_kernel(x_ref, g_ref, wup_ref, cw_ref, cb_ref, wdown_ref, out_ref, aext_s, act_s):
    t = pl.program_id(1)
    tm = FFN_TILE
    x = x_ref[0]
    hb = _bf16(_rmsnorm(x, g_ref[...]))

    @pl.when(t == 0)
    def _():
        aext_s[0:SUBLANES, :] = jnp.zeros((SUBLANES, D_FF), jnp.float32)

    for c0, cw in FFN_CHUNKS:
        cs = slice(c0, c0 + cw)
        a = _dot(hb, wup_ref[:, c0:c0 + cw])
        b = _dot(hb, wup_ref[:, D_FF + c0:D_FF + c0 + cw])
        aext_s[SUBLANES:SUBLANES + tm, cs] = a
        a1 = aext_s[SUBLANES - 1:SUBLANES - 1 + tm, cs]
        a2 = aext_s[SUBLANES - 2:SUBLANES - 2 + tm, cs]
        aext_s[0:SUBLANES, cs] = a[tm - SUBLANES:, :]
        ac = a2 * cw_ref[0:1, cs] + a1 * cw_ref[1:2, cs] + a * cw_ref[2:3, cs] + cb_ref[:, cs]
        act_s[:, cs] = _bf16(ac * jax.nn.sigmoid(ac) * b)
    out_ref[0] = x + _dot(act_s[...], wdown_ref[...])


def _ffn(x, g_ffn, w_up, ffn_conv_w, ffn_conv_b, w_down):
    bsz, s, d = x.shape
    tm = FFN_TILE
    assert s % tm == 0
    return pl.pallas_call(
        _ffn_kernel,
        grid=(bsz, s // tm),
        in_specs=[
            pl.BlockSpec((1, tm, d), lambda b, t: (b, t, 0)),
            _resident((1, d)),
            _resident((d, 2 * D_FF)),
            _resident((3, D_FF)),
            _resident((1, D_FF)),
            _resident((D_FF, d)),
        ],
        out_specs=pl.BlockSpec((1, tm, d), lambda b, t: (b, t, 0)),
        out_shape=jax.ShapeDtypeStruct((bsz, s, d), jnp.float32),
        scratch_shapes=[
            pltpu.VMEM((SUBLANES + tm, D_FF), jnp.float32),
            pltpu.VMEM((tm, D_FF), jnp.bfloat16),
        ],
        compiler_params=pltpu.CompilerParams(
            dimension_semantics=("arbitrary", "arbitrary"),
            vmem_limit_bytes=VMEM_LIMIT_FFN),
        name="ffn",
    )(x, g_ffn.reshape(1, d), _bf16(w_up), ffn_conv_w, ffn_conv_b.reshape(1, D_FF), _bf16(w_down))


def kernel(x, mem, g_mix, w_in, b_gate, conv_w, conv_b, moba_q_gain, moba_k_gain, g_mem, w_mem_kv, memq_gain,
           memk_gain, w_br_conv, w_br_moba, w_br_mem, w_o, g_ffn, w_up, ffn_conv_w, ffn_conv_b, w_down):
    memk, memv = _memkv(mem, g_mem, w_mem_kv, memk_gain)
    x = _mixer(x, memk, memv, g_mix, w_in, b_gate, conv_w, conv_b, moba_q_gain, moba_k_gain, memq_gain,
               w_br_conv, w_br_moba, w_br_mem, w_o)
    return _ffn(x, g_ffn, w_up, ffn_conv_w, ffn_conv_b, w_down)
```

```python
import math

import jax
import jax.numpy as jnp
from jax import lax
from jax.experimental import pallas as pl
from jax.experimental.pallas import tpu as pltpu

D_MODEL = 1024
MEM_LEN = 256
CONV_CH = 512
MOBA_HEADS = 8
MOBA_HEAD_DIM = 64
MOBA_WIDTH = MOBA_HEADS * MOBA_HEAD_DIM
MOBA_BLOCK = 256
MOBA_TOPK = 3
MEM_HEADS = 4
MEM_HEAD_DIM = 128
MEM_WIDTH = MEM_HEADS * MEM_HEAD_DIM
D_FF = 2816
EPS = 1e-6

W_CONV = 0
W_Q = 3 * CONV_CH
W_K = W_Q + MOBA_WIDTH
W_V = W_K + MOBA_WIDTH
W_QMEM = W_V + MOBA_WIDTH
W_GATE = W_QMEM + MEM_WIDTH
IN_COLS = W_GATE + 3 * D_MODEL

LANES = 128
SUBLANES = 8
BF16_ROWS = 16
PAIR = 2 * MOBA_HEAD_DIM
VT_ROWS = MOBA_HEAD_DIM + BF16_ROWS
MASKED = -1e30
VEC_GMIX, VEC_BGATE, VEC_CONV01, VEC_CONV2B, VEC_GAINS, VEC_ROWS = 0, 1, 4, 5, 6, 16
MIXER_SUBS = 2
TAIL_GATE_COLS = 512
VMEM_LIMIT_MIXER = 60 * 1024 * 1024
VMEM_LIMIT_FFN = 48 * 1024 * 1024
FFN_TILE = 512
FFN_CHUNKS = ((0, 1024), (1024, 1024), (2048, 768))

_NT = (((1,), (1,)), ((), ()))


def _dot(a, b):
    return jnp.dot(a, b, preferred_element_type=jnp.float32)


def _dot_nt(a, b):
    return lax.dot_general(a, b, _NT, preferred_element_type=jnp.float32)


def _rmsnorm(x, g):
    return x * lax.rsqrt(jnp.mean(x * x, axis=-1, keepdims=True) + EPS) * g


def _bf16(x):
    return x.astype(jnp.bfloat16)


def _memkv_kernel(mem_ref, g_ref, w_ref, kg_ref, k_ref, v_ref):
    mn = _rmsnorm(mem_ref[0], g_ref[...])
    kv = _dot(_bf16(mn), w_ref[...])
    for h in range(MEM_HEADS):
        lo = h * MEM_HEAD_DIM
        kh = _rmsnorm(kv[:, lo:lo + MEM_HEAD_DIM], kg_ref[...])
        k_ref[0, :, lo:lo + MEM_HEAD_DIM] = _bf16(kh)
    v_ref[0] = _bf16(kv[:, MEM_WIDTH:])


def _memkv(mem, g_mem, w_mem_kv, memk_gain):
    bsz, m, d = mem.shape
    const = lambda b: (0, 0)
    return pl.pallas_call(
        _memkv_kernel,
        grid=(bsz,),
        in_specs=[
            pl.BlockSpec((1, m, d), lambda b: (b, 0, 0)),
            pl.BlockSpec((1, d), const),
            pl.BlockSpec((d, 2 * MEM_WIDTH), const),
            pl.BlockSpec((1, MEM_HEAD_DIM), const),
        ],
        out_specs=[
            pl.BlockSpec((1, m, MEM_WIDTH), lambda b: (b, 0, 0)),
            pl.BlockSpec((1, m, MEM_WIDTH), lambda b: (b, 0, 0)),
        ],
        out_shape=[
            jax.ShapeDtypeStruct((bsz, m, MEM_WIDTH), jnp.bfloat16),
            jax.ShapeDtypeStruct((bsz, m, MEM_WIDTH), jnp.bfloat16),
        ],
        name="memkv",
    )(mem, g_mem.reshape(1, d), _bf16(w_mem_kv), memk_gain.reshape(1, MEM_HEAD_DIM))


def _mixer_kernel(x_ref, vec_ref, wmain_ref, wqvt_ref, qgt_ref, gsum_ref,
                  memk_ref, memv_ref, wbc_ref, wbm_ref, wbx_ref, wo_ref,
                  out_ref,
                  k_s, vt_s, kbar_s, uext_s, rhs_s, sa_s, sb_s, ma_s, mb_s, acc_s, m_s, g1_s):
    @pl.when(pl.program_id(1) == 0)
    def _():
        uext_s[0:SUBLANES, :] = jnp.zeros((SUBLANES, CONV_CH), jnp.float32)
        kbar_s[...] = jnp.zeros(kbar_s.shape, jnp.float32)

    for sub in range(MIXER_SUBS):
        _mixer_block(pl.program_id(1) * MIXER_SUBS + sub, slice(sub * MOBA_BLOCK, (sub + 1) * MOBA_BLOCK), sub % 2 == 1,
                     x_ref, vec_ref, wmain_ref, wqvt_ref, qgt_ref, gsum_ref,
                     memk_ref, memv_ref, wbc_ref, wbm_ref, wbx_ref, wo_ref,
                     out_ref,
                     k_s, vt_s, kbar_s, uext_s, rhs_s, sa_s, sb_s, ma_s, mb_s, acc_s, m_s, g1_s)


def _mixer_block(i, rows, odd, x_ref, vec_ref, wmain_ref, wqvt_ref, qgt_ref, gsum_ref,
                 memk_ref, memv_ref, wbc_ref, wbm_ref, wbx_ref, wo_ref,
                 out_ref,
                 k_s, vt_s, kbar_s, uext_s, rhs_s, sa_s, sb_s, ma_s, mb_s, acc_s, m_s, g1_s):
    blk = MOBA_BLOCK
    nbr = kbar_s.shape[0]
    x = x_ref[0, rows, :]
    hb = _bf16(_rmsnorm(x, vec_ref[VEC_GMIX:VEC_GMIX + 1, :]))

    def proj(lo, width):
        return _dot(hb, wmain_ref[:, lo:lo + width])

    def gate(g, n):
        return jax.nn.sigmoid(g + vec_ref[VEC_BGATE + n:VEC_BGATE + n + 1, :])

    pc = proj(W_CONV, 3 * CONV_CH)
    k_raw = proj(W_K, MOBA_WIDTH)
    qvt = _dot_nt(wqvt_ref[...], hb)
    qm = proj(W_QMEM, MEM_WIDTH)

    u = pc[:, CONV_CH:2 * CONV_CH] * pc[:, 2 * CONV_CH:]
    uext_s[SUBLANES:SUBLANES + blk, :] = u
    u1 = uext_s[SUBLANES - 1:SUBLANES - 1 + blk, :]
    u2 = uext_s[SUBLANES - 2:SUBLANES - 2 + blk, :]
    uext_s[0:SUBLANES, :] = u[blk - SUBLANES:, :]
    conv = (u2 * vec_ref[VEC_CONV01:VEC_CONV01 + 1, :CONV_CH] + u1 * vec_ref[VEC_CONV01:VEC_CONV01 + 1, CONV_CH:]
            + u * vec_ref[VEC_CONV2B:VEC_CONV2B + 1, :CONV_CH] + vec_ref[VEC_CONV2B:VEC_CONV2B + 1, CONV_CH:])
    y_conv = _bf16(pc[:, :CONV_CH] * conv)

    k2_b = _bf16(k_raw * k_raw)

    q3 = qvt[:MOBA_WIDTH].reshape(MOBA_HEADS, MOBA_HEAD_DIM, blk)
    ssq = jnp.sum(q3 * q3, axis=1, keepdims=True)
    qbt = _bf16((q3 * lax.rsqrt(ssq * (1.0 / MOBA_HEAD_DIM) + EPS)).reshape(MOBA_WIDTH, blk) * qgt_ref[...])
    vbt = _bf16(qvt[MOBA_WIDTH:])

    mqg = vec_ref[VEC_GAINS:VEC_GAINS + 1, MOBA_WIDTH:MOBA_WIDTH + MEM_HEAD_DIM]
    qmb = [_bf16(_rmsnorm(qm[:, h * MEM_HEAD_DIM:(h + 1) * MEM_HEAD_DIM], mqg)) for h in range(MEM_HEADS)]

    g0 = proj(W_GATE, D_MODEL)
    kss = _dot(k2_b, gsum_ref[...])
    p_conv = _dot(y_conv, wbc_ref[...])
    s_mem = [_dot_nt(qmb[h], memk_ref[0, :, h * MEM_HEAD_DIM:(h + 1) * MEM_HEAD_DIM]) for h in range(MEM_HEADS)]
    g2 = proj(W_GATE + 2 * D_MODEL, D_MODEL)

    merged = gate(g0, 0) * p_conv
    kn = k_raw * lax.rsqrt(kss * (1.0 / MOBA_HEAD_DIM) + EPS) * vec_ref[VEC_GAINS:VEC_GAINS + 1, :MOBA_WIDTH]
    kb = _bf16(kn)
    k_s[i] = kb
    kbar_row = jnp.mean(kn, axis=0, keepdims=True)
    kb_rows = lax.broadcasted_iota(jnp.int32, kbar_s.shape, 0)
    kbar_prev = kbar_s[...]
    kbar_s[...] = jnp.where(kb_rows == i, kbar_row, kbar_prev)
    kbar_b = _bf16(kbar_prev)

    zeros_h = jnp.zeros((MOBA_HEAD_DIM, blk), jnp.bfloat16)
    qpairs = []
    for h in range(MOBA_HEADS):
        qh = qbt[h * MOBA_HEAD_DIM:(h + 1) * MOBA_HEAD_DIM]
        qpairs.append(jnp.concatenate([qh, zeros_h] if h % 2 == 0 else [zeros_h, qh], axis=0))

    p_mem = [jnp.exp(s - jnp.max(s, axis=-1, keepdims=True)) for s in s_mem]

    pair_cols = lambda h: slice((h // 2) * PAIR, (h // 2 + 1) * PAIR)
    gate_own = [_dot(jnp.concatenate([kbar_b[:, pair_cols(h)], kb[:, pair_cols(h)]]
                                     + ([k_s[i - 1, :, pair_cols(h)]] if odd else []), axis=0), qpairs[h])
                for h in range(MOBA_HEADS)]
    gates = [g[:nbr] for g in gate_own]
    s_own = [g[nbr:nbr + blk] for g in gate_own]
    s_prev = [g[nbr + blk:] for g in gate_own]
    o_mem = [_dot(_bf16(p_mem[h]), memv_ref[0, :, h * MEM_HEAD_DIM:(h + 1) * MEM_HEAD_DIM])
             / jnp.sum(p_mem[h], axis=-1, keepdims=True) for h in range(MEM_HEADS)]
    merged = merged + gate(g2, 2) * _dot(_bf16(jnp.concatenate(o_mem, axis=-1)), wbx_ref[...])

    lane = lax.broadcasted_iota(jnp.int32, (blk, LANES), 1)
    brow = lax.broadcasted_iota(jnp.int32, (nbr, blk), 0)
    browf = brow.astype(jnp.float32)
    neg_inf = jnp.float32(-jnp.inf)
    sel_pad = jnp.full((LANES - nbr, blk), MASKED, jnp.bfloat16)
    ones_r = jnp.ones((BF16_ROWS, blk), jnp.bfloat16)
    vts = []
    bias_prev = []
    for h in range(MOBA_HEADS):
        g = jnp.where(brow < i, gates[h], neg_inf)
        sel = jnp.full((nbr, blk), MASKED, jnp.float32)
        for _ in range(MOBA_TOPK):
            mx = jnp.max(g, axis=0, keepdims=True)
            idx = jnp.min(jnp.where(g == mx, browf, float(nbr)), axis=0, keepdims=True)
            pick = browf == idx
            sel = jnp.where(pick & (mx > neg_inf), 0.0, sel)
            g = jnp.where(pick, neg_inf, g)
        rhs_s[h] = jnp.concatenate([qpairs[h], _bf16(sel), sel_pad], axis=0)
        if odd:
            bias_prev.append(jnp.sum(jnp.where(brow == i - 1, sel, 0.0), axis=0, keepdims=True))
        vt = jnp.concatenate([vbt[h * MOBA_HEAD_DIM:(h + 1) * MOBA_HEAD_DIM], ones_r], axis=0)
        vt_s[i, h] = vt
        vts.append(vt)

    def pair_scores(dst, h, tp):
        dst_s, dst_mx = dst
        parts = []
        for jn in (2 * tp, 2 * tp + 1):
            en = jnp.where(lane == jnp.where(jn < i, jn, nbr - 1), 1.0, 0.0).astype(jnp.bfloat16)
            parts.append(jnp.concatenate([k_s[jnp.minimum(jn, i), :, pair_cols(h)], en], axis=1))
        s = _dot(jnp.concatenate(parts, axis=0)[:, :PAIR + nbr], rhs_s[h, 0:PAIR + nbr, :])
        dst_s[h] = s
        dst_mx[h] = jnp.broadcast_to(jnp.max(s, axis=0, keepdims=True), (SUBLANES, blk))

    causal = (lax.broadcasted_iota(jnp.int32, (blk, blk), 0) <= lax.broadcasted_iota(jnp.int32, (blk, blk), 1))
    set_a, set_b = (sa_s, ma_s), (sb_s, mb_s)
    for h in range(MOBA_HEADS):
        pair_scores(set_a, h, 0)
        s = jnp.where(causal, s_own[h], MASKED)
        vt = vts[h]
        if odd:
            s = jnp.concatenate([s_prev[h] + bias_prev[h], s], axis=0)
            vt = jnp.concatenate([vt_s[i - 1, h], vt], axis=1)
        m0 = jnp.max(s, axis=0, keepdims=True)
        acc_s[h] = _dot(vt, jnp.exp2(_bf16(s - m0)))
        m_s[h] = jnp.broadcast_to(m0, (SUBLANES, blk))

    def attend(src, dst, tp, fill=None):
        src_s, src_mx = src
        ja = jnp.minimum(2 * tp, i)
        jb = jnp.minimum(2 * tp + 1, i)
        for h in range(MOBA_HEADS):
            if dst is not None:
                pair_scores(dst, h, tp + 1)
            if fill is not None:
                fill(h)
            m_old = m_s[h]
            m_new = jnp.maximum(m_old, src_mx[h])
            p = jnp.exp2(_bf16(src_s[h] - m_new[0:1]))
            pv = _dot(jnp.concatenate([vt_s[ja, h], vt_s[jb, h]], axis=1), p)
            acc_s[h] = acc_s[h] * jnp.exp2(m_old[0:1] - m_new[0:1]) + pv
            m_s[h] = m_new

    n_pairs = i // 2

    def two_pairs(u, carry):
        attend(set_a, set_b, 2 * u)
        attend(set_b, set_a, 2 * u + 1)
        return carry

    n_loop = (n_pairs - 1) // 2
    lax.fori_loop(0, n_loop, two_pairs, 0)

    gate_cols = TAIL_GATE_COLS // 2

    def gate1(h):
        if h % (MOBA_HEADS // 2) == 0:
            c = h // (MOBA_HEADS // 2)
            g1_s[:, c * gate_cols:(c + 1) * gate_cols] = proj(W_GATE + D_MODEL + c * gate_cols, gate_cols)

    @pl.when(n_pairs % 2 == 1)
    def _():
        attend(set_a, None, n_pairs - 1, gate1)

    @pl.when((n_pairs % 2 == 0) & (n_pairs > 0))
    def _():
        attend(set_a, set_b, n_pairs - 2)
        attend(set_b, None, n_pairs - 1, gate1)

    @pl.when(n_pairs == 0)
    def _():
        for h in range(MOBA_HEADS):
            gate1(h)

    g1 = jnp.concatenate([g1_s[...], proj(W_GATE + D_MODEL + TAIL_GATE_COLS, D_MODEL - TAIL_GATE_COLS)], axis=1)
    outs = []
    for h in range(MOBA_HEADS):
        a = acc_s[h]
        outs.append(a[:MOBA_HEAD_DIM] / a[MOBA_HEAD_DIM:MOBA_HEAD_DIM + 1])
    y_moba = jnp.concatenate(outs, axis=0).T
    merged = merged + gate(g1, 1) * _dot(_bf16(y_moba), wbm_ref[...])

    out_ref[0, rows, :] = x + _dot(_bf16(merged), wo_ref[...])


def _resident(shape):
    return pl.BlockSpec(shape, lambda b, i: (0,) * len(shape), pipeline_mode=pl.Buffered(1))


def _mixer(x, memk, memv, g_mix, w_in, b_gate, conv_w, conv_b, moba_q_gain, moba_k_gain, memq_gain,
           w_br_conv, w_br_moba, w_br_mem, w_o):
    bsz, s, d = x.shape
    blk = MOBA_BLOCK
    assert s % (MIXER_SUBS * blk) == 0 and d == D_MODEL and MIXER_SUBS % 2 == 0
    nb = s // blk
    nbr = pl.cdiv(nb, BF16_ROWS) * BF16_ROWS
    assert nbr < LANES
    w_main = _bf16(w_in)
    w_qv = lax.optimization_barrier(jnp.concatenate([w_in[:, W_Q:W_K], w_in[:, W_V:W_QMEM]], axis=1))
    w_qvt = _bf16(w_qv.T)
    qgt = jnp.broadcast_to(
        (jnp.tile(moba_q_gain, MOBA_HEADS) * (MOBA_HEAD_DIM ** -0.5 * math.log2(math.e)))[:, None], (MOBA_WIDTH, blk))
    gains = jnp.concatenate([jnp.tile(moba_k_gain, MOBA_HEADS), memq_gain * (MEM_HEAD_DIM ** -0.5),
                             jnp.zeros((d - MOBA_WIDTH - MEM_HEAD_DIM,), jnp.float32)])
    vec = jnp.concatenate([g_mix.reshape(1, d), b_gate.reshape(3, d), conv_w[0:2].reshape(1, d),
                           jnp.concatenate([conv_w[2], conv_b]).reshape(1, d), gains.reshape(1, d),
                           jnp.zeros((VEC_ROWS - VEC_GAINS - 1, d), jnp.float32)], axis=0)
    head_of = jnp.arange(MOBA_WIDTH) // MOBA_HEAD_DIM
    gsum = (head_of[:, None] == head_of[None, :]).astype(jnp.bfloat16)
    return pl.pallas_call(
        _mixer_kernel,
        grid=(bsz, nb // MIXER_SUBS),
        in_specs=[
            pl.BlockSpec((1, MIXER_SUBS * blk, d), lambda b, i: (b, i, 0)),
            _resident((VEC_ROWS, d)),
            _resident((d, IN_COLS)),
            _resident((2 * MOBA_WIDTH, d)),
            _resident((MOBA_WIDTH, blk)),
            _resident((MOBA_WIDTH, MOBA_WIDTH)),
            pl.BlockSpec((1, MEM_LEN, MEM_WIDTH), lambda b, i: (b, 0, 0)),
            pl.BlockSpec((1, MEM_LEN, MEM_WIDTH), lambda b, i: (b, 0, 0)),
            _resident((CONV_CH, d)),
            _resident((MOBA_WIDTH, d)),
            _resident((MEM_WIDTH, d)),
            _resident((d, d)),
        ],
        out_specs=pl.BlockSpec((1, MIXER_SUBS * blk, d), lambda b, i: (b, i, 0)),
        out_shape=jax.ShapeDtypeStruct((bsz, s, d), jnp.float32),
        scratch_shapes=[
            pltpu.VMEM((nb, blk, MOBA_WIDTH), jnp.bfloat16),
            pltpu.VMEM((nb, MOBA_HEADS, VT_ROWS, blk), jnp.bfloat16),
            pltpu.VMEM((nbr, MOBA_WIDTH), jnp.float32),
            pltpu.VMEM((SUBLANES + blk, CONV_CH), jnp.float32),
            pltpu.VMEM((MOBA_HEADS, 2 * LANES, blk), jnp.bfloat16),
            pltpu.VMEM((MOBA_HEADS, 2 * blk, blk), jnp.float32),
            pltpu.VMEM((MOBA_HEADS, 2 * blk, blk), jnp.float32),
            pltpu.VMEM((MOBA_HEADS, SUBLANES, blk), jnp.float32),
            pltpu.VMEM((MOBA_HEADS, SUBLANES, blk), jnp.float32),
            pltpu.VMEM((MOBA_HEADS, VT_ROWS, blk), jnp.float32),
            pltpu.VMEM((MOBA_HEADS, SUBLANES, blk), jnp.float32),
            pltpu.VMEM((blk, TAIL_GATE_COLS), jnp.float32),
        ],
        compiler_params=pltpu.CompilerParams(
            dimension_semantics=("arbitrary", "arbitrary"),
            vmem_limit_bytes=VMEM_LIMIT_MIXER),
        name="mixer",
    )(x, vec, w_main, w_qvt, qgt, gsum, memk, memv, _bf16(w_br_conv), _bf16(w_br_moba), _bf16(w_br_mem), _bf16(w_o))


def _ffn_kernel(x_ref, g_ref, wup_ref, cw_ref, cb_ref, wdown_ref, out_ref, aext_s, act_s):
    t = pl.program_id(1)
    tm = FFN_TILE
    x = x_ref[0]
    hb = _bf16(_rmsnorm(x, g_ref[...]))

    @pl.when(t == 0)
    def _():
        aext_s[0:SUBLANES, :] = jnp.zeros((SUBLANES, D_FF), jnp.float32)

    for c0, cw in FFN_CHUNKS:
        cs = slice(c0, c0 + cw)
        a = _dot(hb, wup_ref[:, c0:c0 + cw])
        b = _dot(hb, wup_ref[:, D_FF + c0:D_FF + c0 + cw])
        aext_s[SUBLANES:SUBLANES + tm, cs] = a
        a1 = aext_s[SUBLANES - 1:SUBLANES - 1 + tm, cs]
        a2 = aext_s[SUBLANES - 2:SUBLANES - 2 + tm, cs]
        aext_s[0:SUBLANES, cs] = a[tm - SUBLANES:, :]
        ac = a2 * cw_ref[0:1, cs] + a1 * cw_ref[1:2, cs] + a * cw_ref[2:3, cs] + cb_ref[:, cs]
        act_s[:, cs] = _bf16(ac * jax.nn.sigmoid(ac) * b)
    out_ref[0] = x + _dot(act_s[...], wdown_ref[...])


def _ffn(x, g_ffn, w_up, ffn_conv_w, ffn_conv_b, w_down):
    bsz, s, d = x.shape
    tm = FFN_TILE
    assert s % tm == 0
    return pl.pallas_call(
        _ffn_kernel,
        grid=(bsz, s // tm),
        in_specs=[
            pl.BlockSpec((1, tm, d), lambda b, t: (b, t, 0)),
            _resident((1, d)),
            _resident((d, 2 * D_FF)),
            _resident((3, D_FF)),
            _resident((1, D_FF)),
            _resident((D_FF, d)),
        ],
        out_specs=pl.BlockSpec((1, tm, d), lambda b, t: (b, t, 0)),
        out_shape=jax.ShapeDtypeStruct((bsz, s, d), jnp.float32),
        scratch_shapes=[
            pltpu.VMEM((SUBLANES + tm, D_FF), jnp.float32),
            pltpu.VMEM((tm, D_FF), jnp.bfloat16),
        ],
        compiler_params=pltpu.CompilerParams(
            dimension_semantics=("arbitrary", "arbitrary"),
            vmem_limit_bytes=VMEM_LIMIT_FFN),
        name="ffn",
    )(x, g_ffn.reshape(1, d), _bf16(w_up), ffn_conv_w, ffn_conv_b.reshape(1, D_FF), _bf16(w_down))


def kernel(x, mem, g_mix, w_in, b_gate, conv_w, conv_b, moba_q_gain, moba_k_gain, g_mem, w_mem_kv, memq_gain,
           memk_gain, w_br_conv, w_br_moba, w_br_mem, w_o, g_ffn, w_up, ffn_conv_w, ffn_conv_b, w_down):
    memk, memv = _memkv(mem, g_mem, w_mem_kv, memk_gain)
    x = _mixer(x, memk, memv, g_mix, w_in, b_gate, conv_w, conv_b, moba_q_gain, moba_k_gain, memq_gain,
               w_br_conv, w_br_moba, w_br_mem, w_o)
    return _ffn(x, g_ffn, w_up, ffn_conv_w, ffn_conv_b, w_down)
```

```python
import math

import jax
import jax.numpy as jnp
from jax import lax
from jax.experimental import pallas as pl
from jax.experimental.pallas import tpu as pltpu

D_MODEL = 1024
MEM_LEN = 256
CONV_CH = 512
MOBA_HEADS = 8
MOBA_HEAD_DIM = 64
MOBA_WIDTH = MOBA_HEADS * MOBA_HEAD_DIM
MOBA_BLOCK = 256
MOBA_TOPK = 3
MEM_HEADS = 4
MEM_HEAD_DIM = 128
MEM_WIDTH = MEM_HEADS * MEM_HEAD_DIM
D_FF = 2816
EPS = 1e-6

W_CONV = 0
W_Q = 3 * CONV_CH
W_K = W_Q + MOBA_WIDTH
W_V = W_K + MOBA_WIDTH
W_QMEM = W_V + MOBA_WIDTH
W_GATE = W_QMEM + MEM_WIDTH
IN_COLS = W_GATE + 3 * D_MODEL

LANES = 128
SUBLANES = 8
BF16_ROWS = 16
PAIR = 2 * MOBA_HEAD_DIM
VT_ROWS = MOBA_HEAD_DIM + BF16_ROWS
MASKED = -1e30
VEC_GMIX, VEC_BGATE, VEC_CONV01, VEC_CONV2B, VEC_GAINS, VEC_ROWS = 0, 1, 4, 5, 6, 16
MIXER_SUBS = 2
TAIL_GATE_COLS = 1024
VMEM_LIMIT_MIXER = 60 * 1024 * 1024
VMEM_LIMIT_FFN = 48 * 1024 * 1024
FFN_TILE = 512
FFN_CHUNKS = ((0, 1024), (1024, 1024), (2048, 768))

_NT = (((1,), (1,)), ((), ()))


def _dot(a, b):
    return jnp.dot(a, b, preferred_element_type=jnp.float32)


def _dot_nt(a, b):
    return lax.dot_general(a, b, _NT, preferred_element_type=jnp.float32)


def _rmsnorm(x, g):
    return x * lax.rsqrt(jnp.mean(x * x, axis=-1, keepdims=True) + EPS) * g


def _bf16(x):
    return x.astype(jnp.bfloat16)


def _memkv_kernel(mem_ref, g_ref, w_ref, kg_ref, k_ref, v_ref):
    mn = _rmsnorm(mem_ref[0], g_ref[...])
    kv = _dot(_bf16(mn), w_ref[...])
    for h in range(MEM_HEADS):
        lo = h * MEM_HEAD_DIM
        kh = _rmsnorm(kv[:, lo:lo + MEM_HEAD_DIM], kg_ref[...])
        k_ref[0, :, lo:lo + MEM_HEAD_DIM] = _bf16(kh)
    v_ref[0] = _bf16(kv[:, MEM_WIDTH:])


def _memkv(mem, g_mem, w_mem_kv, memk_gain):
    bsz, m, d = mem.shape
    const = lambda b: (0, 0)
    return pl.pallas_call(
        _memkv_kernel,
        grid=(bsz,),
        in_specs=[
            pl.BlockSpec((1, m, d), lambda b: (b, 0, 0)),
            pl.BlockSpec((1, d), const),
            pl.BlockSpec((d, 2 * MEM_WIDTH), const),
            pl.BlockSpec((1, MEM_HEAD_DIM), const),
        ],
        out_specs=[
            pl.BlockSpec((1, m, MEM_WIDTH), lambda b: (b, 0, 0)),
            pl.BlockSpec((1, m, MEM_WIDTH), lambda b: (b, 0, 0)),
        ],
        out_shape=[
            jax.ShapeDtypeStruct((bsz, m, MEM_WIDTH), jnp.bfloat16),
            jax.ShapeDtypeStruct((bsz, m, MEM_WIDTH), jnp.bfloat16),
        ],
        name="memkv",
    )(mem, g_mem.reshape(1, d), _bf16(w_mem_kv), memk_gain.reshape(1, MEM_HEAD_DIM))


def _mixer_kernel(x_ref, vec_ref, wmain_ref, wqvt_ref, qgt_ref, gsum_ref,
                  memk_ref, memv_ref, wbc_ref, wbm_ref, wbx_ref, wo_ref,
                  out_ref,
                  k_s, vt_s, kbar_s, uext_s, rhs_s, sa_s, sb_s, ma_s, mb_s, acc_s, m_s, g1_s):
    @pl.when(pl.program_id(1) == 0)
    def _():
        uext_s[0:SUBLANES, :] = jnp.zeros((SUBLANES, CONV_CH), jnp.float32)
        kbar_s[...] = jnp.zeros(kbar_s.shape, jnp.float32)

    for sub in range(MIXER_SUBS):
        _mixer_block(pl.program_id(1) * MIXER_SUBS + sub, slice(sub * MOBA_BLOCK, (sub + 1) * MOBA_BLOCK), sub % 2 == 1,
                     x_ref, vec_ref, wmain_ref, wqvt_ref, qgt_ref, gsum_ref,
                     memk_ref, memv_ref, wbc_ref, wbm_ref, wbx_ref, wo_ref,
                     out_ref,
                     k_s, vt_s, kbar_s, uext_s, rhs_s, sa_s, sb_s, ma_s, mb_s, acc_s, m_s, g1_s)


def _mixer_block(i, rows, odd, x_ref, vec_ref, wmain_ref, wqvt_ref, qgt_ref, gsum_ref,
                 memk_ref, memv_ref, wbc_ref, wbm_ref, wbx_ref, wo_ref,
                 out_ref,
                 k_s, vt_s, kbar_s, uext_s, rhs_s, sa_s, sb_s, ma_s, mb_s, acc_s, m_s, g1_s):
    blk = MOBA_BLOCK
    nbr = kbar_s.shape[0]
    x = x_ref[0, rows, :]
    hb = _bf16(_rmsnorm(x, vec_ref[VEC_GMIX:VEC_GMIX + 1, :]))

    def proj(lo, width):
        return _dot(hb, wmain_ref[:, lo:lo + width])

    def gate(g, n):
        return jax.nn.sigmoid(g + vec_ref[VEC_BGATE + n:VEC_BGATE + n + 1, :])

    pc = proj(W_CONV, 3 * CONV_CH)
    k_raw = proj(W_K, MOBA_WIDTH)
    qvt = _dot_nt(wqvt_ref[...], hb)
    qm = proj(W_QMEM, MEM_WIDTH)

    u = pc[:, CONV_CH:2 * CONV_CH] * pc[:, 2 * CONV_CH:]
    uext_s[SUBLANES:SUBLANES + blk, :] = u
    u1 = uext_s[SUBLANES - 1:SUBLANES - 1 + blk, :]
    u2 = uext_s[SUBLANES - 2:SUBLANES - 2 + blk, :]
    uext_s[0:SUBLANES, :] = u[blk - SUBLANES:, :]
    conv = (u2 * vec_ref[VEC_CONV01:VEC_CONV01 + 1, :CONV_CH] + u1 * vec_ref[VEC_CONV01:VEC_CONV01 + 1, CONV_CH:]
            + u * vec_ref[VEC_CONV2B:VEC_CONV2B + 1, :CONV_CH] + vec_ref[VEC_CONV2B:VEC_CONV2B + 1, CONV_CH:])
    y_conv = _bf16(pc[:, :CONV_CH] * conv)

    k2_b = _bf16(k_raw * k_raw)

    q3 = qvt[:MOBA_WIDTH].reshape(MOBA_HEADS, MOBA_HEAD_DIM, blk)
    ssq = jnp.sum(q3 * q3, axis=1, keepdims=True)
    qbt = _bf16((q3 * lax.rsqrt(ssq * (1.0 / MOBA_HEAD_DIM) + EPS)).reshape(MOBA_WIDTH, blk) * qgt_ref[...])
    vbt = _bf16(qvt[MOBA_WIDTH:])

    mqg = vec_ref[VEC_GAINS:VEC_GAINS + 1, MOBA_WIDTH:MOBA_WIDTH + MEM_HEAD_DIM]
    qmb = [_bf16(_rmsnorm(qm[:, h * MEM_HEAD_DIM:(h + 1) * MEM_HEAD_DIM], mqg)) for h in range(MEM_HEADS)]

    g0 = proj(W_GATE, D_MODEL)
    kss = _dot(k2_b, gsum_ref[...])
    p_conv = _dot(y_conv, wbc_ref[...])
    s_mem = [_dot_nt(qmb[h], memk_ref[0, :, h * MEM_HEAD_DIM:(h + 1) * MEM_HEAD_DIM]) for h in range(MEM_HEADS)]
    g2 = proj(W_GATE + 2 * D_MODEL, D_MODEL)

    merged = gate(g0, 0) * p_conv
    kn = k_raw * lax.rsqrt(kss * (1.0 / MOBA_HEAD_DIM) + EPS) * vec_ref[VEC_GAINS:VEC_GAINS + 1, :MOBA_WIDTH]
    kb = _bf16(kn)
    k_s[i] = kb
    kbar_row = jnp.mean(kn, axis=0, keepdims=True)
    kb_rows = lax.broadcasted_iota(jnp.int32, kbar_s.shape, 0)
    kbar_prev = kbar_s[...]
    kbar_s[...] = jnp.where(kb_rows == i, kbar_row, kbar_prev)
    kbar_b = _bf16(kbar_prev)

    zeros_h = jnp.zeros((MOBA_HEAD_DIM, blk), jnp.bfloat16)
    qpairs = []
    for h in range(MOBA_HEADS):
        qh = qbt[h * MOBA_HEAD_DIM:(h + 1) * MOBA_HEAD_DIM]
        qpairs.append(jnp.concatenate([qh, zeros_h] if h % 2 == 0 else [zeros_h, qh], axis=0))

    p_mem = [jnp.exp(s - jnp.max(s, axis=-1, keepdims=True)) for s in s_mem]

    pair_cols = lambda h: slice((h // 2) * PAIR, (h // 2 + 1) * PAIR)
    gate_own = [_dot(jnp.concatenate([kbar_b[:, pair_cols(h)], kb[:, pair_cols(h)]]
                                     + ([k_s[i - 1, :, pair_cols(h)]] if odd else []), axis=0), qpairs[h])
                for h in range(MOBA_HEADS)]
    gates = [g[:nbr] for g in gate_own]
    s_own = [g[nbr:nbr + blk] for g in gate_own]
    s_prev = [g[nbr + blk:] for g in gate_own]
    o_mem = [_dot(_bf16(p_mem[h]), memv_ref[0, :, h * MEM_HEAD_DIM:(h + 1) * MEM_HEAD_DIM])
             / jnp.sum(p_mem[h], axis=-1, keepdims=True) for h in range(MEM_HEADS)]
    merged = merged + gate(g2, 2) * _dot(_bf16(jnp.concatenate(o_mem, axis=-1)), wbx_ref[...])

    lane = lax.broadcasted_iota(jnp.int32, (blk, LANES), 1)
    brow = lax.broadcasted_iota(jnp.int32, (nbr, blk), 0)
    browf = brow.astype(jnp.float32)
    neg_inf = jnp.float32(-jnp.inf)
    sel_pad = jnp.full((LANES - nbr, blk), MASKED, jnp.bfloat16)
    ones_r = jnp.ones((BF16_ROWS, blk), jnp.bfloat16)
    vts = []
    bias_prev = []
    for h in range(MOBA_HEADS):
        g = jnp.where(brow < i, gates[h], neg_inf)
        sel = jnp.full((nbr, blk), MASKED, jnp.float32)
        for _ in range(MOBA_TOPK):
            mx = jnp.max(g, axis=0, keepdims=True)
            idx = jnp.min(jnp.where(g == mx, browf, float(nbr)), axis=0, keepdims=True)
            pick = browf == idx
            sel = jnp.where(pick & (mx > neg_inf), 0.0, sel)
            g = jnp.where(pick, neg_inf, g)
        rhs_s[h] = jnp.concatenate([qpairs[h], _bf16(sel), sel_pad], axis=0)
        if odd:
            bias_prev.append(jnp.sum(jnp.where(brow == i - 1, sel, 0.0), axis=0, keepdims=True))
        vt = jnp.concatenate([vbt[h * MOBA_HEAD_DIM:(h + 1) * MOBA_HEAD_DIM], ones_r], axis=0)
        vt_s[i, h] = vt
        vts.append(vt)

    def pair_scores(dst, h, tp):
        dst_s, dst_mx = dst
        parts = []
        for jn in (2 * tp, 2 * tp + 1):
            en = jnp.where(lane == jnp.where(jn < i, jn, nbr - 1), 1.0, 0.0).astype(jnp.bfloat16)
            parts.append(jnp.concatenate([k_s[jnp.minimum(jn, i), :, pair_cols(h)], en], axis=1))
        s = _dot(jnp.concatenate(parts, axis=0)[:, :PAIR + nbr], rhs_s[h, 0:PAIR + nbr, :])
        dst_s[h] = s
        dst_mx[h] = jnp.broadcast_to(jnp.max(s, axis=0, keepdims=True), (SUBLANES, blk))

    causal = (lax.broadcasted_iota(jnp.int32, (blk, blk), 0) <= lax.broadcasted_iota(jnp.int32, (blk, blk), 1))
    set_a, set_b = (sa_s, ma_s), (sb_s, mb_s)
    for h in range(MOBA_HEADS):
        pair_scores(set_a, h, 0)
        s = jnp.where(causal, s_own[h], MASKED)
        vt = vts[h]
        if odd:
            s = jnp.concatenate([s_prev[h] + bias_prev[h], s], axis=0)
            vt = jnp.concatenate([vt_s[i - 1, h], vt], axis=1)
        m0 = jnp.max(s, axis=0, keepdims=True)
        acc_s[h] = _dot(vt, jnp.exp2(_bf16(s - m0)))
        m_s[h] = jnp.broadcast_to(m0, (SUBLANES, blk))

    def attend(src, dst, tp, fill=None):
        src_s, src_mx = src
        ja = jnp.minimum(2 * tp, i)
        jb = jnp.minimum(2 * tp + 1, i)
        for h in range(MOBA_HEADS):
            if dst is not None:
                pair_scores(dst, h, tp + 1)
            if fill is not None:
                fill(h)
            m_old = m_s[h]
            m_new = jnp.maximum(m_old, src_mx[h])
            p = jnp.exp2(_bf16(src_s[h] - m_new[0:1]))
            pv = _dot(jnp.concatenate([vt_s[ja, h], vt_s[jb, h]], axis=1), p)
            acc_s[h] = acc_s[h] * jnp.exp2(m_old[0:1] - m_new[0:1]) + pv
            m_s[h] = m_new

    n_pairs = i // 2

    def two_pairs(u, carry):
        attend(set_a, set_b, 2 * u)
        attend(set_b, set_a, 2 * u + 1)
        return carry

    n_loop = (n_pairs - 1) // 2
    lax.fori_loop(0, n_loop, two_pairs, 0)

    gate_cols = TAIL_GATE_COLS // 2

    def gate1(h):
        if h % (MOBA_HEADS // 2) == 0:
            c = h // (MOBA_HEADS // 2)
            g1_s[:, c * gate_cols:(c + 1) * gate_cols] = proj(W_GATE + D_MODEL + c * gate_cols, gate_cols)

    @pl.when(n_pairs % 2 == 1)
    def _():
        attend(set_a, None, n_pairs - 1, gate1)

    @pl.when((n_pairs % 2 == 0) & (n_pairs > 0))
    def _():
        attend(set_a, set_b, n_pairs - 2)
        attend(set_b, None, n_pairs - 1, gate1)

    @pl.when(n_pairs == 0)
    def _():
        for h in range(MOBA_HEADS):
            gate1(h)

    g1 = g1_s[...]
    if TAIL_GATE_COLS < D_MODEL:
        g1 = jnp.concatenate([g1, proj(W_GATE + D_MODEL + TAIL_GATE_COLS, D_MODEL - TAIL_GATE_COLS)], axis=1)
    outs = []
    for h in range(MOBA_HEADS):
        a = acc_s[h]
        outs.append(a[:MOBA_HEAD_DIM] / a[MOBA_HEAD_DIM:MOBA_HEAD_DIM + 1])
    y_moba = jnp.concatenate(outs, axis=0).T
    merged = merged + gate(g1, 1) * _dot(_bf16(y_moba), wbm_ref[...])

    out_ref[0, rows, :] = x + _dot(_bf16(merged), wo_ref[...])


def _resident(shape):
    return pl.BlockSpec(shape, lambda b, i: (0,) * len(shape), pipeline_mode=pl.Buffered(1))


def _mixer(x, memk, memv, g_mix, w_in, b_gate, conv_w, conv_b, moba_q_gain, moba_k_gain, memq_gain,
           w_br_conv, w_br_moba, w_br_mem, w_o):
    bsz, s, d = x.shape
    blk = MOBA_BLOCK
    assert s % (MIXER_SUBS * blk) == 0 and d == D_MODEL and MIXER_SUBS % 2 == 0
    nb = s // blk
    nbr = pl.cdiv(nb, BF16_ROWS) * BF16_ROWS
    assert nbr < LANES
    w_main = _bf16(w_in)
    w_qv = lax.optimization_barrier(jnp.concatenate([w_in[:, W_Q:W_K], w_in[:, W_V:W_QMEM]], axis=1))
    w_qvt = _bf16(w_qv.T)
    qgt = jnp.broadcast_to(
        (jnp.tile(moba_q_gain, MOBA_HEADS) * (MOBA_HEAD_DIM ** -0.5 * math.log2(math.e)))[:, None], (MOBA_WIDTH, blk))
    gains = jnp.concatenate([jnp.tile(moba_k_gain, MOBA_HEADS), memq_gain * (MEM_HEAD_DIM ** -0.5),
                             jnp.zeros((d - MOBA_WIDTH - MEM_HEAD_DIM,), jnp.float32)])
    vec = jnp.concatenate([g_mix.reshape(1, d), b_gate.reshape(3, d), conv_w[0:2].reshape(1, d),
                           jnp.concatenate([conv_w[2], conv_b]).reshape(1, d), gains.reshape(1, d),
                           jnp.zeros((VEC_ROWS - VEC_GAINS - 1, d), jnp.float32)], axis=0)
    head_of = jnp.arange(MOBA_WIDTH) // MOBA_HEAD_DIM
    gsum = (head_of[:, None] == head_of[None, :]).astype(jnp.bfloat16)
    return pl.pallas_call(
        _mixer_kernel,
        grid=(bsz, nb // MIXER_SUBS),
        in_specs=[
            pl.BlockSpec((1, MIXER_SUBS * blk, d), lambda b, i: (b, i, 0)),
            _resident((VEC_ROWS, d)),
            _resident((d, IN_COLS)),
            _resident((2 * MOBA_WIDTH, d)),
            _resident((MOBA_WIDTH, blk)),
            _resident((MOBA_WIDTH, MOBA_WIDTH)),
            pl.BlockSpec((1, MEM_LEN, MEM_WIDTH), lambda b, i: (b, 0, 0)),
            pl.BlockSpec((1, MEM_LEN, MEM_WIDTH), lambda b, i: (b, 0, 0)),
            _resident((CONV_CH, d)),
            _resident((MOBA_WIDTH, d)),
            _resident((MEM_WIDTH, d)),
            _resident((d, d)),
        ],
        out_specs=pl.BlockSpec((1, MIXER_SUBS * blk, d), lambda b, i: (b, i, 0)),
        out_shape=jax.ShapeDtypeStruct((bsz, s, d), jnp.float32),
        scratch_shapes=[
            pltpu.VMEM((nb, blk, MOBA_WIDTH), jnp.bfloat16),
            pltpu.VMEM((nb, MOBA_HEADS, VT_ROWS, blk), jnp.bfloat16),
            pltpu.VMEM((nbr, MOBA_WIDTH), jnp.float32),
            pltpu.VMEM((SUBLANES + blk, CONV_CH), jnp.float32),
            pltpu.VMEM((MOBA_HEADS, 2 * LANES, blk), jnp.bfloat16),
            pltpu.VMEM((MOBA_HEADS, 2 * blk, blk), jnp.float32),
            pltpu.VMEM((MOBA_HEADS, 2 * blk, blk), jnp.float32),
            pltpu.VMEM((MOBA_HEADS, SUBLANES, blk), jnp.float32),
            pltpu.VMEM((MOBA_HEADS, SUBLANES, blk), jnp.float32),
            pltpu.VMEM((MOBA_HEADS, VT_ROWS, blk), jnp.float32),
            pltpu.VMEM((MOBA_HEADS, SUBLANES, blk), jnp.float32),
            pltpu.VMEM((blk, TAIL_GATE_COLS), jnp.float32),
        ],
        compiler_params=pltpu.CompilerParams(
            dimension_semantics=("arbitrary", "arbitrary"),
            vmem_limit_bytes=VMEM_LIMIT_MIXER),
        name="mixer",
    )(x, vec, w_main, w_qvt, qgt, gsum, memk, memv, _bf16(w_br_conv), _bf16(w_br_moba), _bf16(w_br_mem), _bf16(w_o))


def _ffn_kernel(x_ref, g_ref, wup_ref, cw_ref, cb_ref, wdown_ref, out_ref, aext_s, act_s):
    t = pl.program_id(1)
    tm = FFN_TILE
    x = x_ref[0]
    hb = _bf16(_rmsnorm(x, g_ref[...]))

    @pl.when(t == 0)
    def _():
        aext_s[0:SUBLANES, :] = jnp.zeros((SUBLANES, D_FF), jnp.float32)

    for c0, cw in FFN_CHUNKS:
        cs = slice(c0, c0 + cw)
        a = _dot(hb, wup_ref[:, c0:c0 + cw])
        b = _dot(hb, wup_ref[:, D_FF + c0:D_FF + c0 + cw])
        aext_s[SUBLANES:SUBLANES + tm, cs] = a
        a1 = aext_s[SUBLANES - 1:SUBLANES - 1 + tm, cs]
        a2 = aext_s[SUBLANES - 2:SUBLANES - 2 + tm, cs]
        aext_s[0:SUBLANES, cs] = a[tm - SUBLANES:, :]
        ac = a2 * cw_ref[0:1, cs] + a1 * cw_ref[1:2, cs] + a * cw_ref[2:3, cs] + cb_ref[:, cs]
        act_s[:, cs] = _bf16(ac * jax.nn.sigmoid(ac) * b)
    out_ref[0] = x + _dot(act_s[...], wdown_ref[...])


def _ffn(x, g_ffn, w_up, ffn_conv_w, ffn_conv_b, w_down):
    bsz, s, d = x.shape
    tm = FFN_TILE
    assert s % tm == 0
    return pl.pallas_call(
        _ffn_kernel,
        grid=(bsz, s // tm),
        in_specs=[
            pl.BlockSpec((1, tm, d), lambda b, t: (b, t, 0)),
            _resident((1, d)),
            _resident((d, 2 * D_FF)),
            _resident((3, D_FF)),
            _resident((1, D_FF)),
            _resident((D_FF, d)),
        ],
        out_specs=pl.BlockSpec((1, tm, d), lambda b, t: (b, t, 0)),
        out_shape=jax.ShapeDtypeStruct((bsz, s, d), jnp.float32),
        scratch_shapes=[
            pltpu.VMEM((SUBLANES + tm, D_FF), jnp.float32),
            pltpu.VMEM((tm, D_FF), jnp.bfloat16),
        ],
        compiler_params=pltpu.CompilerParams(
            dimension_semantics=("arbitrary", "arbitrary"),
            vmem_limit_bytes=VMEM_LIMIT_FFN),
        name="ffn",
    )(x, g_ffn.reshape(1, d), _bf16(w_up), ffn_conv_w, ffn_conv_b.reshape(1, D_FF), _bf16(w_down))


def kernel(x, mem, g_mix, w_in, b_gate, conv_w, conv_b, moba_q_gain, moba_k_gain, g_mem, w_mem_kv, memq_gain,
           memk_gain, w_br_conv, w_br_moba, w_br_mem, w_o, g_ffn, w_up, ffn_conv_w, ffn_conv_b, w_down):
    memk, memv = _memkv(mem, g_mem, w_mem_kv, memk_gain)
    x = _mixer(x, memk, memv, g_mix, w_in, b_gate, conv_w, conv_b, moba_q_gain, moba_k_gain, memq_gain,
               w_br_conv, w_br_moba, w_br_mem, w_o)
    return _ffn(x, g_ffn, w_up, ffn_conv_w, ffn_conv_b, w_down)
```
